```python
import math
import jax, jax.numpy as jnp
from jax import lax
import numpy as np

D_MODEL = 1024
BATCH = 8
SEQ = 4096
DEPTH = 2

N_EVEN = (DEPTH + 1) // 2
N_ODD = DEPTH // 2
GRID_W = 64
EPS = 1e-6
MIX_A = D_MODEL // 2
FNET_HEAD_DIM = 64
FNET_HEADS = MIX_A // FNET_HEAD_DIM
HEAD_DIM = 64
N_Q_HEADS = (D_MODEL - MIX_A) // HEAD_DIM
N_KV_HEADS = 2
GQA_GROUP = N_Q_HEADS // N_KV_HEADS
IN_WIDTH = MIX_A + (N_Q_HEADS + 2 * N_KV_HEADS) * HEAD_DIM
Q_BLOCK = 128
ROPE_THETA = 10000.0
ROPE_AXIS_DIM = HEAD_DIM // 2
SSM_GROUP_CH = 16
SSM_GROUPS = D_MODEL // SSM_GROUP_CH
SSM_STATE = 64
DT_MIN = 0.001
DT_MAX = 0.1
D_FF = 4 * D_MODEL

kernel_name = "hybrid_fnet_gqa_s5_encoder"


def _rmsnorm(x, g):
    x32 = x.astype(jnp.float32)
    y = x32 * lax.rsqrt(jnp.mean(x32 * x32, axis=-1, keepdims=True) + EPS)
    return (y * g.astype(jnp.float32)).astype(x.dtype)


def _axial_angles(seq_len):
    rows = seq_len // GRID_W
    row = jnp.broadcast_to(jnp.arange(rows, dtype=jnp.float32)[:, None], (rows, GRID_W)).reshape(seq_len)
    col = jnp.broadcast_to(jnp.arange(GRID_W, dtype=jnp.float32)[None, :], (rows, GRID_W)).reshape(seq_len)
    inv_freq = ROPE_THETA ** (-jnp.arange(0, ROPE_AXIS_DIM, 2, dtype=jnp.float32) / ROPE_AXIS_DIM)
    ang = jnp.stack([row[:, None] * inv_freq, col[:, None] * inv_freq], axis=1)
    return jnp.cos(ang), jnp.sin(ang)


def _apply_axial_rope(x, cos, sin):
    b, s, h, d = x.shape
    xr = x.astype(jnp.float32).reshape(b, s, h, 2, 2, ROPE_AXIS_DIM // 2)
    x1, x2 = xr[..., 0, :], xr[..., 1, :]
    c = cos[None, :, None]
    sn = sin[None, :, None]
    out = jnp.stack([x1 * c - x2 * sn, x2 * c + x1 * sn], axis=-2)
    return out.reshape(b, s, h, d).astype(x.dtype)


def _fourier_mixer(f, w_fnet):
    b, s, _ = f.shape
    fh = f.astype(jnp.float32).reshape(b, s, FNET_HEADS, FNET_HEAD_DIM)
    spec = jnp.fft.fft2(fh, axes=(1, 3), norm="ortho").real
    out = jnp.einsum('bshc,hcd->bshd', spec, w_fnet.astype(jnp.float32))
    return out.reshape(b, s, MIX_A).astype(f.dtype)


def _gqa_axial(q, k, v, q_norm, k_norm):
    b, s, _ = q.shape
    q = _rmsnorm(q.reshape(b, s, N_Q_HEADS, HEAD_DIM), q_norm)
    k = _rmsnorm(k.reshape(b, s, N_KV_HEADS, HEAD_DIM), k_norm)
    v = v.reshape(b, s, N_KV_HEADS, HEAD_DIM)
    cos, sin = _axial_angles(s)
    q = _apply_axial_rope(q, cos, sin) * (HEAD_DIM ** -0.5)
    k = _apply_axial_rope(k, cos, sin)
    nb = s // Q_BLOCK
    qb = q.reshape(b, nb, Q_BLOCK, N_KV_HEADS, GQA_GROUP, HEAD_DIM).transpose(1, 0, 2, 3, 4, 5)

    def block(qblk):
        sc = jnp.einsum('bqkgd,bskd->bkgqs', qblk, k).astype(jnp.float32)
        p = jax.nn.softmax(sc, axis=-1).astype(v.dtype)
        return jnp.einsum('bkgqs,bskd->bqkgd', p, v)

    o = lax.map(block, qb)
    return o.transpose(1, 0, 2, 3, 4, 5).reshape(b, s, N_Q_HEADS * HEAD_DIM)


def _even_mixer(h, w_in, w_fnet, q_norm, k_norm, w_out):
    z = h @ w_in
    qo = MIX_A
    ko = qo + N_Q_HEADS * HEAD_DIM
    vo = ko + N_KV_HEADS * HEAD_DIM
    fa = _fourier_mixer(z[..., :qo], w_fnet)
    att = _gqa_axial(z[..., qo:ko], z[..., ko:vo], z[..., vo:], q_norm, k_norm)
    return jnp.concatenate([fa, att], axis=-1) @ w_out


def _ssm_combine(e1, e2):
    a1, b1 = e1
    a2, b2 = e2
    return a1 * a2, a2 * b1 + b2


def _s5_mixer(u, lam_re, lam_im, log_dt, b_re, b_im, c_re, c_im, d_skip, w_gate, b_gate):
    f32 = jnp.float32
    b, s, d = u.shape
    u32 = u.astype(f32)
    ug = u32.reshape(b, s, SSM_GROUPS, SSM_GROUP_CH).transpose(1, 0, 2, 3).astype(jnp.complex64)
    y = jnp.zeros((s, b, SSM_GROUPS, SSM_GROUP_CH), f32)
    for direction in range(2):
        lam = lax.complex(lam_re[direction].astype(f32), lam_im[direction].astype(f32))
        dt = jnp.exp(log_dt[direction].astype(f32))[:, None]
        lam_bar = jnp.exp(lam * dt)
        b_mat = lax.complex(b_re[direction].astype(f32), b_im[direction].astype(f32))
        b_bar = ((lam_bar - 1.0) / lam)[..., None] * b_mat
        bu = jnp.einsum('sbgc,gpc->sbgp', ug, b_bar)
        a = jnp.broadcast_to(lam_bar, (s, 1, SSM_GROUPS, SSM_STATE))
        _, states = lax.associative_scan(_ssm_combine, (a, bu), axis=0, reverse=(direction == 1))
        c_mat = lax.complex(c_re[direction].astype(f32), c_im[direction].astype(f32))
        y = y + jnp.einsum('sbgp,gcp->sbgc', states, c_mat).real
    y = y.transpose(1, 0, 2, 3).reshape(b, s, d) + d_skip.astype(f32) * u32
    g = jax.nn.gelu(y)
    out = g * jax.nn.sigmoid(g @ w_gate.astype(f32) + b_gate.astype(f32))
    return out.astype(u.dtype)


def _mlp(h, w1, w2):
    a = jax.nn.relu(h @ w1)
    return (a * a) @ w2


def setup_inputs(seed: int = 0) -> dict:
    key = jax.random.key(seed)
    ks = jax.random.split(key, 24)
    nrm = jax.random.normal
    f32 = jnp.float32
    x = nrm(ks[0], (BATCH, SEQ, D_MODEL), f32)
    norm_mix = 1.0 + 0.02 * nrm(ks[1], (DEPTH, D_MODEL), f32)
    norm_mlp = 1.0 + 0.02 * nrm(ks[2], (DEPTH, D_MODEL), f32)
    mlp_w1 = nrm(ks[3], (DEPTH, D_MODEL, D_FF), f32) * D_MODEL ** -0.5
    mlp_w2 = nrm(ks[4], (DEPTH, D_FF, D_MODEL), f32) * D_FF ** -0.5
    w_in = nrm(ks[5], (N_EVEN, D_MODEL, IN_WIDTH), f32) * D_MODEL ** -0.5
    w_fnet = nrm(ks[6], (N_EVEN, FNET_HEADS, FNET_HEAD_DIM, FNET_HEAD_DIM), f32) * FNET_HEAD_DIM ** -0.5
    q_norm = 1.0 + 0.02 * nrm(ks[7], (N_EVEN, HEAD_DIM), f32)
    k_norm = 1.0 + 0.02 * nrm(ks[8], (N_EVEN, HEAD_DIM), f32)
    w_out = nrm(ks[9], (N_EVEN, D_MODEL, D_MODEL), f32) * D_MODEL ** -0.5
    lam_re = -0.5 + 0.01 * nrm(ks[10], (N_ODD, 2, SSM_GROUPS, SSM_STATE), f32)
    lam_im = jnp.pi * jnp.arange(SSM_STATE, dtype=f32) + 0.01 * nrm(ks[11], (N_ODD, 2, SSM_GROUPS, SSM_STATE), f32)
    log_dt = jax.random.uniform(ks[12], (N_ODD, 2, SSM_GROUPS), f32, math.log(DT_MIN), math.log(DT_MAX))
    b_re = nrm(ks[13], (N_ODD, 2, SSM_GROUPS, SSM_STATE, SSM_GROUP_CH), f32) * (2 * SSM_GROUP_CH) ** -0.5
    b_im = nrm(ks[14], (N_ODD, 2, SSM_GROUPS, SSM_STATE, SSM_GROUP_CH), f32) * (2 * SSM_GROUP_CH) ** -0.5
    c_re = nrm(ks[15], (N_ODD, 2, SSM_GROUPS, SSM_GROUP_CH, SSM_STATE), f32) * SSM_STATE ** -0.5
    c_im = nrm(ks[16], (N_ODD, 2, SSM_GROUPS, SSM_GROUP_CH, SSM_STATE), f32) * SSM_STATE ** -0.5
    d_skip = nrm(ks[17], (N_ODD, D_MODEL), f32)
    w_gate = nrm(ks[18], (N_ODD, D_MODEL, D_MODEL), f32) * D_MODEL ** -0.5
    b_gate = 0.01 * nrm(ks[19], (N_ODD, D_MODEL), f32)
    final_norm = 1.0 + 0.02 * nrm(ks[20], (D_MODEL,), f32)
    return {"x": x, "norm_mix": norm_mix, "norm_mlp": norm_mlp, "mlp_w1": mlp_w1, "mlp_w2": mlp_w2,
            "w_in": w_in, "w_fnet": w_fnet, "q_norm": q_norm, "k_norm": k_norm, "w_out": w_out,
            "lam_re": lam_re, "lam_im": lam_im, "log_dt": log_dt, "b_re": b_re, "b_im": b_im,
            "c_re": c_re, "c_im": c_im, "d_skip": d_skip, "w_gate": w_gate, "b_gate": b_gate,
            "final_norm": final_norm}


def reference(x, norm_mix, norm_mlp, mlp_w1, mlp_w2, w_in, w_fnet, q_norm, k_norm, w_out,
              lam_re, lam_im, log_dt, b_re, b_im, c_re, c_im, d_skip, w_gate, b_gate, final_norm):
    for i in range(DEPTH):
        j = i // 2
        h = _rmsnorm(x, norm_mix[i])
        if i % 2 == 0:
            x = x + _even_mixer(h, w_in[j], w_fnet[j], q_norm[j], k_norm[j], w_out[j])
        else:
            x = x + _s5_mixer(h, lam_re[j], lam_im[j], log_dt[j], b_re[j], b_im[j], c_re[j], c_im[j],
                              d_skip[j], w_gate[j], b_gate[j])
        x = x + _mlp(_rmsnorm(x, norm_mlp[i]), mlp_w1[i], mlp_w2[i])
    return _rmsnorm(x, final_norm)
```

```python
import functools
import math

import numpy as np
import jax
import jax.numpy as jnp
from jax import lax
from jax.experimental import pallas as pl
from jax.experimental.pallas import tpu as pltpu

F32 = jnp.float32
BF16 = jnp.bfloat16

EPS = 1e-6
D_MODEL = 1024
MIX_A = 512
FNET_HEAD_DIM = 64
FNET_HEADS = MIX_A // FNET_HEAD_DIM
HEAD_DIM = 64
N_Q_HEADS = 8
N_KV_HEADS = 2
GQA_GROUP = N_Q_HEADS // N_KV_HEADS
Q_WIDTH = N_Q_HEADS * HEAD_DIM
KV_WIDTH = N_KV_HEADS * HEAD_DIM
IN_WIDTH = MIX_A + Q_WIDTH + 2 * KV_WIDTH
GRID_W = 64
ROPE_THETA = 10000.0
ROPE_AXIS_DIM = HEAD_DIM // 2
SSM_GROUP_CH = 16
SSM_GROUPS = D_MODEL // SSM_GROUP_CH
SSM_STATE = 64
D_FF = 4 * D_MODEL
CHUNK = 128
LANES = 128
VMEM_LIMIT = 56 * 1024 * 1024


def _cparams(n_grid_dims):
    return pltpu.CompilerParams(
        dimension_semantics=("arbitrary",) * n_grid_dims, vmem_limit_bytes=VMEM_LIMIT)


def _dot(a, b):
    return jnp.dot(a, b, preferred_element_type=F32)


def _split_bf16(a):
    hi = a.astype(BF16)
    lo = (a - hi.astype(F32)).astype(BF16)
    return hi, lo


def _dot3(a, b):
    a_hi, a_lo = _split_bf16(a)
    b_hi, b_lo = _split_bf16(b)
    return _dot(a_hi, b_hi) + (_dot(a_hi, b_lo) + _dot(a_lo, b_hi))


def _rms_rows(x, g):
    ms = jnp.mean(x * x, axis=-1, keepdims=True)
    return x * lax.rsqrt(ms + EPS) * g


def _fold_kernel(cc_ref, sc_ref, w_ref, g_ref):
    w = w_ref[...]
    g_ref[:, :MIX_A] = _dot3(cc_ref[...], w).astype(BF16)
    g_ref[:, MIX_A:] = (-_dot3(sc_ref[...], w)).astype(BF16)


def _fourier_fold(w_fnet):
    n = np.arange(FNET_HEAD_DIM)
    ang = 2.0 * np.pi * np.outer(n, n) / FNET_HEAD_DIM
    scale = FNET_HEAD_DIM ** -0.5
    eye = np.eye(FNET_HEADS)
    cc = jnp.asarray(np.kron(eye, np.cos(ang) * scale), F32)
    sc = jnp.asarray(np.kron(eye, np.sin(ang) * scale), F32)
    eye_j = jnp.eye(FNET_HEADS, dtype=F32)
    w_bd = (eye_j[:, None, :, None] * w_fnet.astype(F32)[:, :, None, :]).reshape(MIX_A, MIX_A)
    return pl.pallas_call(
        _fold_kernel,
        out_shape=jax.ShapeDtypeStruct((MIX_A, 2 * MIX_A), BF16),
        name="fourier_fold",
    )(cc, sc, w_bd)


def _head_norm(t, ones, gain):
    hi, lo = _split_bf16(t * t)
    ss = _dot(hi, ones) + _dot(lo, ones)
    return t * lax.rsqrt(ss * (1.0 / HEAD_DIM) + EPS) * gain


def _rope(t, cos, sin):
    width = t.shape[1]
    reps = width // LANES
    c = jnp.concatenate([cos] * reps, axis=1) if reps > 1 else cos
    s = jnp.concatenate([sin] * reps, axis=1) if reps > 1 else sin
    lane = lax.broadcasted_iota(jnp.int32, t.shape, 1)
    first_half = (lane % ROPE_AXIS_DIM) < (ROPE_AXIS_DIM // 2)
    partner = jnp.where(first_half,
                        pltpu.roll(t, width - ROPE_AXIS_DIM // 2, 1),
                        pltpu.roll(t, ROPE_AXIS_DIM // 2, 1))
    return t * c + partner * s


def _inproj_kernel(x_ref, g_ref, win_ref, gmat_ref, ones_ref, qg_ref, kg_ref,
                   cq_ref, sq_ref, ck_ref, sk_ref,
                   u_ref, v_ref, q_ref, k_ref, vv_ref):
    h = _rms_rows(x_ref[0], g_ref[...]).astype(BF16)
    z = _dot(h, win_ref[...])
    uv = _dot(z[:, :MIX_A].astype(BF16), gmat_ref[...])
    u_ref[0] = uv[:, :MIX_A].astype(BF16)
    v_ref[0] = uv[:, MIX_A:].astype(BF16)
    q = z[:, MIX_A:MIX_A + Q_WIDTH]
    k = z[:, MIX_A + Q_WIDTH:MIX_A + Q_WIDTH + KV_WIDTH]
    v = z[:, MIX_A + Q_WIDTH + KV_WIDTH:]
    ones = ones_ref[...]
    q = _rope(_head_norm(q, ones, qg_ref[...]), cq_ref[...], sq_ref[...])
    k = _rope(_head_norm(k, ones[:KV_WIDTH, :KV_WIDTH], kg_ref[...]), ck_ref[...], sk_ref[...])
    for hh in range(N_Q_HEADS):
        q_ref[0, hh] = q[:, hh * HEAD_DIM:(hh + 1) * HEAD_DIM].astype(BF16)
    for hh in range(N_KV_HEADS):
        k_ref[0, hh] = k[:, hh * HEAD_DIM:(hh + 1) * HEAD_DIM].astype(BF16)
        vv_ref[0, hh] = v[:, hh * HEAD_DIM:(hh + 1) * HEAD_DIM].astype(BF16)


def _rope_tables(seq_len):
    rows = seq_len // GRID_W
    row = jnp.broadcast_to(jnp.arange(rows, dtype=F32)[:, None], (rows, GRID_W)).reshape(seq_len)
    col = jnp.broadcast_to(jnp.arange(GRID_W, dtype=F32)[None, :], (rows, GRID_W)).reshape(seq_len)
    inv_freq = ROPE_THETA ** (-jnp.arange(0, ROPE_AXIS_DIM, 2, dtype=F32) / ROPE_AXIS_DIM)
    ar = row[:, None] * inv_freq
    ac = col[:, None] * inv_freq
    cos = jnp.concatenate([jnp.cos(ar), jnp.cos(ar), jnp.cos(ac), jnp.cos(ac)], axis=1)
    sin = jnp.concatenate([-jnp.sin(ar), jnp.sin(ar), -jnp.sin(ac), jnp.sin(ac)], axis=1)
    cos = jnp.concatenate([cos, cos], axis=1)
    sin = jnp.concatenate([sin, sin], axis=1)
    scale = HEAD_DIM ** -0.5
    return cos * scale, sin * scale, cos, sin


def _inproj(x, g, w_in, gmat, q_norm, k_norm, tm):
    b, s, d = x.shape
    cq, sq, ck, sk = _rope_tables(s)
    head = np.arange(Q_WIDTH) // HEAD_DIM
    ones = jnp.asarray(head[:, None] == head[None, :], BF16)
    qg = jnp.tile(q_norm.astype(F32), N_Q_HEADS)[None, :]
    kg = jnp.tile(k_norm.astype(F32), N_KV_HEADS)[None, :]
    const = lambda shape: pl.BlockSpec(shape, lambda i, j: (0,) * len(shape))
    tab = pl.BlockSpec((tm, LANES), lambda i, j: (j, 0))
    return pl.pallas_call(
        _inproj_kernel,
        grid=(b, s // tm),
        in_specs=[
            pl.BlockSpec((1, tm, d), lambda i, j: (i, j, 0)),
            const((1, d)), const((d, IN_WIDTH)), const((MIX_A, 2 * MIX_A)), const((Q_WIDTH, Q_WIDTH)),
            const((1, Q_WIDTH)), const((1, KV_WIDTH)), tab, tab, tab, tab,
        ],
        out_specs=[
            pl.BlockSpec((1, tm, MIX_A), lambda i, j: (i, j, 0)),
            pl.BlockSpec((1, tm, MIX_A), lambda i, j: (i, j, 0)),
            pl.BlockSpec((1, N_Q_HEADS, tm, HEAD_DIM), lambda i, j: (i, 0, j, 0)),
            pl.BlockSpec((1, N_KV_HEADS, tm, HEAD_DIM), lambda i, j: (i, 0, j, 0)),
            pl.BlockSpec((1, N_KV_HEADS, tm, HEAD_DIM), lambda i, j: (i, 0, j, 0)),
        ],
        out_shape=[
            jax.ShapeDtypeStruct((b, s, MIX_A), BF16),
            jax.ShapeDtypeStruct((b, s, MIX_A), BF16),
            jax.ShapeDtypeStruct((b, N_Q_HEADS, s, HEAD_DIM), BF16),
            jax.ShapeDtypeStruct((b, N_KV_HEADS, s, HEAD_DIM), BF16),
            jax.ShapeDtypeStruct((b, N_KV_HEADS, s, HEAD_DIM), BF16),
        ],
        compiler_params=_cparams(2),
        name="inproj",
    )(x, g[None, :], w_in.astype(BF16), gmat, ones, qg, kg, cq, sq, ck, sk)


def _seqdft_kernel(cs_ref, ss_ref, u_ref, v_ref, o_ref):
    o_ref[0] = (_dot(cs_ref[...], u_ref[0]) + _dot(ss_ref[...], v_ref[0])).astype(BF16)


def _dft_tables(n):
    idx = jnp.arange(n, dtype=jnp.int32)
    prod = (idx[:, None] * idx[None, :]) % n
    ang = prod.astype(F32) * (2.0 * math.pi / n)
    scale = n ** -0.5
    return (jnp.cos(ang) * scale).astype(BF16), (jnp.sin(ang) * scale).astype(BF16)


def _seqdft(u, v, tm):
    b, s, w = u.shape
    cs, ss = _dft_tables(s)
    return pl.pallas_call(
        _seqdft_kernel,
        grid=(b, s // tm),
        in_specs=[
            pl.BlockSpec((tm, s), lambda i, j: (j, 0)),
            pl.BlockSpec((tm, s), lambda i, j: (j, 0)),
            pl.BlockSpec((1, s, w), lambda i, j: (i, 0, 0)),
            pl.BlockSpec((1, s, w), lambda i, j: (i, 0, 0)),
        ],
        out_specs=pl.BlockSpec((1, tm, w), lambda i, j: (i, j, 0)),
        out_shape=jax.ShapeDtypeStruct((b, s, w), BF16),
        compiler_params=_cparams(2),
        name="seqdft",
    )(cs, ss, u, v)


def _attn_kernel(q_ref, k_ref, v_ref, o_ref):
    tq = q_ref.shape[2]
    q = q_ref[0].reshape(GQA_GROUP * tq, HEAD_DIM)
    s = lax.dot_general(q, k_ref[0, 0], (((1,), (1,)), ((), ())), preferred_element_type=F32)
    m = jnp.max(s, axis=-1, keepdims=True)
    p = jnp.exp(s - m)
    l = jnp.sum(p, axis=-1, keepdims=True)
    o = _dot(p.astype(BF16), v_ref[0, 0]) / l
    o_ref[0] = jnp.concatenate(
        [o[g * tq:(g + 1) * tq, :] for g in range(GQA_GROUP)], axis=1).astype(BF16)


def _attention(q, k, v, tq):
    b, _, s, _ = q.shape
    return pl.pallas_call(
        _attn_kernel,
        grid=(b, N_KV_HEADS, s // tq),
        in_specs=[
            pl.BlockSpec((1, GQA_GROUP, tq, HEAD_DIM), lambda i, g, j: (i, g, j, 0)),
            pl.BlockSpec((1, 1, s, HEAD_DIM), lambda i, g, j: (i, g, 0, 0)),
            pl.BlockSpec((1, 1, s, HEAD_DIM), lambda i, g, j: (i, g, 0, 0)),
        ],
        out_specs=pl.BlockSpec((1, tq, GQA_GROUP * HEAD_DIM), lambda i, g, j: (i, j, g)),
        out_shape=jax.ShapeDtypeStruct((b, s, Q_WIDTH), BF16),
        compiler_params=_cparams(3),
        name="gqa_attention",
    )(q, k, v)


def _outproj_kernel(fa_ref, att_ref, x_ref, wa_ref, wb_ref, o_ref):
    o_ref[0] = x_ref[0] + (_dot(fa_ref[0], wa_ref[...]) + _dot(att_ref[0], wb_ref[...]))


def _outproj(fa, att, x, w_out, tm):
    b, s, d = x.shape
    w = w_out.astype(BF16)
    const = lambda shape: pl.BlockSpec(shape, lambda i, j: (0,) * len(shape))
    return pl.pallas_call(
        _outproj_kernel,
        grid=(b, s // tm),
        in_specs=[
            pl.BlockSpec((1, tm, MIX_A), lambda i, j: (i, j, 0)),
            pl.BlockSpec((1, tm, Q_WIDTH), lambda i, j: (i, j, 0)),
            pl.BlockSpec((1, tm, d), lambda i, j: (i, j, 0)),
            const((MIX_A, d)), const((Q_WIDTH, d)),
        ],
        out_specs=pl.BlockSpec((1, tm, d), lambda i, j: (i, j, 0)),
        out_shape=jax.ShapeDtypeStruct((b, s, d), F32),
        compiler_params=_cparams(2),
        name="outproj",
    )(fa, att, x, w[:MIX_A], w[MIX_A:])


def _mlp_kernel(x_ref, g_ref, w1_ref, w2_ref, gf_ref, o_ref, *, final_norm):
    x = x_ref[0]
    h = _rms_rows(x, g_ref[...]).astype(BF16)
    a = jnp.maximum(_dot(h, w1_ref[...]), 0.0)
    y = x + _dot((a * a).astype(BF16), w2_ref[...])
    if final_norm:
        y = _rms_rows(y, gf_ref[...])
    o_ref[0] = y


def _mlp(x, g, w1, w2, g_final, tm, final_norm):
    b, s, d = x.shape
    const = lambda shape: pl.BlockSpec(shape, lambda i, j: (0,) * len(shape),
                                       pipeline_mode=pl.Buffered(1))
    return pl.pallas_call(
        functools.partial(_mlp_kernel, final_norm=final_norm),
        grid=(b, s // tm),
        in_specs=[
            pl.BlockSpec((1, tm, d), lambda i, j: (i, j, 0)),
            const((1, d)), const((d, D_FF)), const((D_FF, d)), const((1, d)),
        ],
        out_specs=pl.BlockSpec((1, tm, d), lambda i, j: (i, j, 0)),
        out_shape=jax.ShapeDtypeStruct((b, s, d), F32),
        compiler_params=_cparams(2),
        name="mlp_final" if final_norm else "mlp",
    )(x, g[None, :], w1.astype(BF16), w2.astype(BF16), g_final[None, :])


def _normT_kernel(x_ref, g_ref, o_ref):
    o_ref[...] = _rms_rows(x_ref[0], g_ref[...]).T.astype(BF16)


def _norm_transpose(x, g, ts):
    b, s, d = x.shape
    nblk = s // ts
    return pl.pallas_call(
        _normT_kernel,
        grid=(b, nblk),
        in_specs=[pl.BlockSpec((1, ts, d), lambda i, j: (i, j, 0)),
                  pl.BlockSpec((1, d), lambda i, j: (0, 0))],
        out_specs=pl.BlockSpec((d, ts), lambda i, j: (0, i * nblk + j)),
        out_shape=jax.ShapeDtypeStruct((d, b * s), BF16),
        compiler_params=_cparams(2),
        name="norm_transpose",
    )(x, g[None, :])


def _cmul(ar, ai, br, bi):
    return ar * br - ai * bi, ar * bi + ai * br


def _cpow(a, theta, e):
    mag = jnp.exp(e * a)
    ang = e * theta
    return mag * jnp.cos(ang), mag * jnp.sin(ang)


def _ssm_kernel(ut_ref, prow_ref, pcol_ref, crow_ref, ccol_ref, brow_ref, y_ref,
                ktab_ref, mmat_ref, pmat_ref, cmat_ref, *, n_chunks):
    P = SSM_STATE
    C = SSM_GROUP_CH
    L = CHUNK
    rows = ut_ref.shape[2]

    prow = prow_ref[0]
    lre, lim = prow[0:1, :], prow[1:2, :]
    dt = jnp.exp(prow[2:3, :])
    a_row, th_row = lre * dt, lim * dt
    lbr, lbi = _cpow(a_row, th_row, 1.0)
    inv = 1.0 / (lre * lre + lim * lim)
    coef_r, coef_i = _cmul(lbr - 1.0, lbi, lre * inv, -lim * inv)
    bbr, bbi = _cmul(coef_r, coef_i, brow_ref[0, 0], brow_ref[0, 1])

    lane = lax.broadcasted_iota(jnp.int32, (L, 2 * P), 1)
    jj = lax.broadcasted_iota(jnp.int32, (L, 2 * P), 0)
    e_p = jnp.where(lane < P, L - 1 - jj, jj).astype(F32)
    pwr, pwi = _cpow(a_row, th_row, e_p)
    for c in range(C):
        xr, xi = _cmul(pwr, pwi, bbr[c:c + 1, :], bbi[c:c + 1, :])
        pmat_ref[c * L:(c + 1) * L, :2 * P] = xr.astype(BF16)
        pmat_ref[c * L:(c + 1) * L, 2 * P:] = xi.astype(BF16)

    pcol = pcol_ref[0]
    lre_c, lim_c = pcol[:, 0:1], pcol[:, 1:2]
    dt_c = jnp.exp(pcol[:, 2:3])
    a_col, th_col = lre_c * dt_c, lim_c * dt_c

    sub = lax.broadcasted_iota(jnp.int32, (2 * P, L), 0)
    ii = lax.broadcasted_iota(jnp.int32, (2 * P, L), 1)
    e_c = jnp.where(sub < P, ii + 1, L - ii).astype(F32)
    qwr, qwi = _cpow(a_col, th_col, e_c)
    ccr, cci = ccol_ref[0, 0], ccol_ref[0, 1]
    for c in range(C):
        xr, xi = _cmul(qwr, qwi, ccr[:, c:c + 1], cci[:, c:c + 1])
        cmat_ref[:2 * P, c * L:(c + 1) * L] = xr.astype(BF16)
        cmat_ref[2 * P:, c * L:(c + 1) * L] = (-xi).astype(BF16)

    sub2 = lax.broadcasted_iota(jnp.int32, (2 * P, 2 * L), 0)
    mm = lax.broadcasted_iota(jnp.int32, (2 * P, 2 * L), 1)
    fwd = sub2 < P
    e_k = jnp.where(fwd, mm, (2 * L - mm) % (2 * L)).astype(F32)
    keep = jnp.where(fwd, jnp.where(mm < L, 1.0, 0.0),
                     jnp.where(mm >= L, 1.0, jnp.where(mm == 0, 1.0, 0.0)))
    kwr, kwi = _cpow(a_col, th_col, e_k * keep)
    kwr = kwr * keep
    kwi = kwi * keep
    crr, cri = crow_ref[0, 0], crow_ref[0, 1]
    cb_r, cb_i = [], []
    for c in range(C):
        xr, xi = _cmul(crr, cri, bbr[c:c + 1, :], bbi[c:c + 1, :])
        cb_r.append(xr)
        cb_i.append(xi)
    cb_r = jnp.concatenate(cb_r, axis=0)
    cb_i = jnp.concatenate(cb_i, axis=0)
    ktab_ref[...] = _dot3(cb_r, kwr) - _dot3(cb_i, kwi)

    def fill(cp, carry):
        for c in range(C):
            row = ktab_ref[pl.ds(cp * C + c, 1), :]
            blk = pltpu.roll(jnp.broadcast_to(row, (L, 2 * L)), 0, 1, stride=1, stride_axis=0)
            mmat_ref[pl.ds(pl.multiple_of(cp * L, L), L), c * L:(c + 1) * L] = blk[:, :L].astype(BF16)
        return carry
    lax.fori_loop(0, C, fill, 0)

    u = jnp.concatenate([ut_ref[0, c] for c in range(C)], axis=1)
    y = _dot(u, mmat_ref[...])
    xloc = _dot(u, pmat_ref[...])
    xr, xi = xloc[:, :2 * P], xloc[:, 2 * P:]

    k_idx = lax.broadcasted_iota(jnp.int32, (rows, 2 * P), 0) % n_chunks
    is_f = lax.broadcasted_iota(jnp.int32, (rows, 2 * P), 1) < P
    ar, ai = _cpow(a_row, th_row, float(L))

    def shifted(t, step):
        down = pltpu.roll(t, step, 0)
        up = pltpu.roll(t, rows - step, 0)
        return jnp.where(is_f, jnp.where(k_idx >= step, down, 0.0),
                         jnp.where(k_idx < n_chunks - step, up, 0.0))

    step = 1
    while step < n_chunks:
        sr, si = shifted(xr, step), shifted(xi, step)
        pr, pi = _cmul(sr, si, ar, ai)
        xr, xi = xr + pr, xi + pi
        ar, ai = _cmul(ar, ai, ar, ai)
        step *= 2
    xprev = jnp.concatenate([shifted(xr, 1), shifted(xi, 1)], axis=1).astype(BF16)
    y = y + _dot(xprev, cmat_ref[...])
    for c in range(C):
        y_ref[0, c] = y[:, c * L:(c + 1) * L]


def _ssm(ut, lam_re, lam_im, log_dt, b_re, b_im, c_re, c_im, batch, seq):
    G, C, P, L = SSM_GROUPS, SSM_GROUP_CH, SSM_STATE, CHUNK
    n_chunks = seq // L
    rows = batch * n_chunks
    ut4 = ut.reshape(G, C, rows, L)
    f32 = lambda t: t.astype(F32)
    cat_p = lambda t: jnp.concatenate([f32(t[0]), f32(t[1])], axis=-1)
    ldt = jnp.broadcast_to(f32(log_dt)[:, :, None], (2, G, P))
    fields = jnp.stack([cat_p(lam_re), cat_p(lam_im), cat_p(ldt)], axis=1)
    prow = jnp.concatenate([fields, jnp.zeros((G, 5, 2 * P), F32)], axis=1)
    pcol = jnp.swapaxes(prow, 1, 2)
    crow = jnp.stack([jnp.concatenate([f32(c_re[0]), f32(c_re[1])], axis=-1),
                      jnp.concatenate([f32(c_im[0]), f32(c_im[1])], axis=-1)], axis=1)
    ccol = jnp.swapaxes(crow, 2, 3)
    brow = jnp.stack([jnp.concatenate([f32(b_re[0]), f32(b_re[1])], axis=1),
                      jnp.concatenate([f32(b_im[0]), f32(b_im[1])], axis=1)], axis=1)
    brow = jnp.swapaxes(brow, 2, 3)
    y4 = pl.pallas_call(
        functools.partial(_ssm_kernel, n_chunks=n_chunks),
        grid=(G,),
        in_specs=[
            pl.BlockSpec((1, C, rows, L), lambda g: (g, 0, 0, 0)),
            pl.BlockSpec((1, 8, 2 * P), lambda g: (g, 0, 0)),
            pl.BlockSpec((1, 2 * P, 8), lambda g: (g, 0, 0)),
            pl.BlockSpec((1, 2, C, 2 * P), lambda g: (g, 0, 0, 0)),
            pl.BlockSpec((1, 2, 2 * P, C), lambda g: (g, 0, 0, 0)),
            pl.BlockSpec((1, 2, C, 2 * P), lambda g: (g, 0, 0, 0)),
        ],
        out_specs=pl.BlockSpec((1, C, rows, L), lambda g: (g, 0, 0, 0)),
        out_shape=jax.ShapeDtypeStruct((G, C, rows, L), F32),
        scratch_shapes=[
            pltpu.VMEM((C * C, 2 * L), F32),
            pltpu.VMEM((C * L, C * L), BF16),
            pltpu.VMEM((C * L, 4 * P), BF16),
            pltpu.VMEM((4 * P, C * L), BF16),
        ],
        compiler_params=_cparams(1),
        name="s5_scan",
    )(ut4, prow, pcol, crow, ccol, brow)
    return y4.reshape(G * C, batch * seq)


def _gelu_tanh(y):
    return 0.5 * y * (1.0 + jnp.tanh(math.sqrt(2.0 / math.pi) * (y + 0.044715 * (y * y * y))))


def _gate_kernel(yt_ref, x_ref, g_ref, dsk_ref, wg_ref, bg_ref, o_ref):
    x = x_ref[0]
    u = _rms_rows(x, g_ref[...])
    y = yt_ref[...].T + dsk_ref[...] * u
    gl = _gelu_tanh(y)
    gate = _dot(gl.astype(BF16), wg_ref[...]) + bg_ref[...]
    o_ref[0] = x + gl * (1.0 / (1.0 + jnp.exp(-gate)))


def _gate(yt, x, g, d_skip, w_gate, b_gate, ts):
    b, s, d = x.shape
    nblk = s // ts
    const = lambda shape: pl.BlockSpec(shape, lambda i, j: (0,) * len(shape))
    return pl.pallas_call(
        _gate_kernel,
        grid=(b, nblk),
        in_specs=[
            pl.BlockSpec((d, ts), lambda i, j: (0, i * nblk + j)),
            pl.BlockSpec((1, ts, d), lambda i, j: (i, j, 0)),
            const((1, d)), const((1, d)), const((d, d)), const((1, d)),
        ],
        out_specs=pl.BlockSpec((1, ts, d), lambda i, j: (i, j, 0)),
        out_shape=jax.ShapeDtypeStruct((b, s, d), F32),
        compiler_params=_cparams(2),
        name="s5_gate",
    )(yt, x, g[None, :], d_skip.astype(F32)[None, :], w_gate.astype(BF16), b_gate.astype(F32)[None, :])


def _pick(n, pref):
    t = min(n, pref)
    while n % t:
        t //= 2
    return t


def kernel(x, norm_mix, norm_mlp, mlp_w1, mlp_w2, w_in, w_fnet, q_norm, k_norm, w_out, lam_re, lam_im, log_dt, b_re, b_im, c_re, c_im, d_skip, w_gate, b_gate, final_norm):
    b, s, d = x.shape
    assert d == D_MODEL and s % CHUNK == 0 and s % GRID_W == 0
    tm = _pick(s, 512)
    gmat = _fourier_fold(w_fnet[0])
    u, v, q, k, vv = _inproj(x, norm_mix[0], w_in[0], gmat, q_norm[0], k_norm[0], tm)
    fa = _seqdft(u, v, tm)
    att = _attention(q, k, vv, _pick(s, 128))
    x = _outproj(fa, att, x, w_out[0], tm)
    x = _mlp(x, norm_mlp[0], mlp_w1[0], mlp_w2[0], final_norm, tm, False)
    ut = _norm_transpose(x, norm_mix[1], tm)
    yt = _ssm(ut, lam_re[0], lam_im[0], log_dt[0], b_re[0], b_im[0], c_re[0], c_im[0], b, s)
    x = _gate(yt, x, norm_mix[1], d_skip[0], w_gate[0], b_gate[0], tm)
    return _mlp(x, norm_mlp[1], mlp_w1[1], mlp_w2[1], final_norm, tm, True)
```

```python
import functools
import math

import numpy as np
import jax
import jax.numpy as jnp
from jax import lax
from jax.experimental import pallas as pl
from jax.experimental.pallas import tpu as pltpu

F32 = jnp.float32
BF16 = jnp.bfloat16

EPS = 1e-6
D_MODEL = 1024
MIX_A = 512
FNET_HEAD_DIM = 64
FNET_HEADS = MIX_A // FNET_HEAD_DIM
HEAD_DIM = 64
N_Q_HEADS = 8
N_KV_HEADS = 2
GQA_GROUP = N_Q_HEADS // N_KV_HEADS
Q_WIDTH = N_Q_HEADS * HEAD_DIM
KV_WIDTH = N_KV_HEADS * HEAD_DIM
IN_WIDTH = MIX_A + Q_WIDTH + 2 * KV_WIDTH
GRID_W = 64
ROPE_THETA = 10000.0
ROPE_AXIS_DIM = HEAD_DIM // 2
SSM_GROUP_CH = 16
SSM_GROUPS = D_MODEL // SSM_GROUP_CH
SSM_STATE = 64
D_FF = 4 * D_MODEL
CHUNK = 128
LANES = 128
LOG2_E = 1.4426950408889634
V_ROWS = HEAD_DIM + 16
VMEM_LIMIT = 56 * 1024 * 1024


def _cparams(n_grid_dims):
    return pltpu.CompilerParams(
        dimension_semantics=("arbitrary",) * n_grid_dims, vmem_limit_bytes=VMEM_LIMIT)


def _dot(a, b):
    return jnp.dot(a, b, preferred_element_type=F32)


def _split_bf16(a):
    hi = a.astype(BF16)
    lo = (a - hi.astype(F32)).astype(BF16)
    return hi, lo


def _dot3(a, b):
    a_hi, a_lo = _split_bf16(a)
    b_hi, b_lo = _split_bf16(b)
    return _dot(a_hi, b_hi) + (_dot(a_hi, b_lo) + _dot(a_lo, b_hi))


def _rms_rows(x, g):
    ms = jnp.mean(x * x, axis=-1, keepdims=True)
    return x * lax.rsqrt(ms + EPS) * g


def _fold_kernel(cc_ref, sc_ref, w_ref, g_ref):
    w = w_ref[...]
    g_ref[:, :MIX_A] = _dot3(cc_ref[...], w).astype(BF16)
    g_ref[:, MIX_A:] = (-_dot3(sc_ref[...], w)).astype(BF16)


def _fourier_fold(w_fnet):
    n = np.arange(FNET_HEAD_DIM)
    ang = 2.0 * np.pi * np.outer(n, n) / FNET_HEAD_DIM
    scale = FNET_HEAD_DIM ** -0.5
    eye = np.eye(FNET_HEADS)
    cc = jnp.asarray(np.kron(eye, np.cos(ang) * scale), F32)
    sc = jnp.asarray(np.kron(eye, np.sin(ang) * scale), F32)
    eye_j = jnp.eye(FNET_HEADS, dtype=F32)
    w_bd = (eye_j[:, None, :, None] * w_fnet.astype(F32)[:, :, None, :]).reshape(MIX_A, MIX_A)
    return pl.pallas_call(
        _fold_kernel,
        out_shape=jax.ShapeDtypeStruct((MIX_A, 2 * MIX_A), BF16),
        name="fourier_fold",
    )(cc, sc, w_bd)


def _head_norm(t, ones, gain):
    hi, lo = _split_bf16(t * t)
    ss = _dot(hi, ones) + _dot(lo, ones)
    return t * lax.rsqrt(ss * (1.0 / HEAD_DIM) + EPS) * gain


def _rope(t, cos, sin):
    width = t.shape[1]
    reps = width // LANES
    c = jnp.concatenate([cos] * reps, axis=1) if reps > 1 else cos
    s = jnp.concatenate([sin] * reps, axis=1) if reps > 1 else sin
    lane = lax.broadcasted_iota(jnp.int32, t.shape, 1)
    first_half = (lane % ROPE_AXIS_DIM) < (ROPE_AXIS_DIM // 2)
    partner = jnp.where(first_half,
                        pltpu.roll(t, width - ROPE_AXIS_DIM // 2, 1),
                        pltpu.roll(t, ROPE_AXIS_DIM // 2, 1))
    return t * c + partner * s


def _inproj_kernel(x_ref, g_ref, win_ref, gmat_ref, ones_ref, qg_ref, kg_ref,
                   cq_ref, sq_ref, ck_ref, sk_ref,
                   u_ref, v_ref, q_ref, k_ref, vv_ref):
    h = _rms_rows(x_ref[0], g_ref[...]).astype(BF16)
    z = _dot(h, win_ref[...])
    uv = _dot(z[:, :MIX_A].astype(BF16), gmat_ref[...])
    u_ref[0] = uv[:, :MIX_A].astype(BF16)
    v_ref[0] = uv[:, MIX_A:].astype(BF16)
    q = z[:, MIX_A:MIX_A + Q_WIDTH]
    k = z[:, MIX_A + Q_WIDTH:MIX_A + Q_WIDTH + KV_WIDTH]
    v = z[:, MIX_A + Q_WIDTH + KV_WIDTH:]
    ones = ones_ref[...]
    q = _rope(_head_norm(q, ones, qg_ref[...]), cq_ref[...], sq_ref[...])
    k = _rope(_head_norm(k, ones[:KV_WIDTH, :KV_WIDTH], kg_ref[...]), ck_ref[...], sk_ref[...])
    tm = q.shape[0]
    q_ref[0] = q.T.reshape(N_Q_HEADS, HEAD_DIM, tm).astype(BF16)
    for hh in range(N_KV_HEADS):
        k_ref[0, hh] = k[:, hh * HEAD_DIM:(hh + 1) * HEAD_DIM].astype(BF16)
    vv_ref[0, :, :HEAD_DIM, :] = v.T.reshape(N_KV_HEADS, HEAD_DIM, tm).astype(BF16)
    vv_ref[0, :, HEAD_DIM:, :] = jnp.ones((N_KV_HEADS, V_ROWS - HEAD_DIM, tm), BF16)


def _rope_tables(seq_len):
    rows = seq_len // GRID_W
    row = jnp.broadcast_to(jnp.arange(rows, dtype=F32)[:, None], (rows, GRID_W)).reshape(seq_len)
    col = jnp.broadcast_to(jnp.arange(GRID_W, dtype=F32)[None, :], (rows, GRID_W)).reshape(seq_len)
    inv_freq = ROPE_THETA ** (-jnp.arange(0, ROPE_AXIS_DIM, 2, dtype=F32) / ROPE_AXIS_DIM)
    ar = row[:, None] * inv_freq
    ac = col[:, None] * inv_freq
    cos = jnp.concatenate([jnp.cos(ar), jnp.cos(ar), jnp.cos(ac), jnp.cos(ac)], axis=1)
    sin = jnp.concatenate([-jnp.sin(ar), jnp.sin(ar), -jnp.sin(ac), jnp.sin(ac)], axis=1)
    cos = jnp.concatenate([cos, cos], axis=1)
    sin = jnp.concatenate([sin, sin], axis=1)
    scale = HEAD_DIM ** -0.5 * LOG2_E
    return cos * scale, sin * scale, cos, sin


def _inproj(x, g, w_in, gmat, q_norm, k_norm, tm):
    b, s, d = x.shape
    cq, sq, ck, sk = _rope_tables(s)
    head = np.arange(Q_WIDTH) // HEAD_DIM
    ones = jnp.asarray(head[:, None] == head[None, :], BF16)
    qg = jnp.tile(q_norm.astype(F32), N_Q_HEADS)[None, :]
    kg = jnp.tile(k_norm.astype(F32), N_KV_HEADS)[None, :]
    const = lambda shape: pl.BlockSpec(shape, lambda i, j: (0,) * len(shape))
    tab = pl.BlockSpec((tm, LANES), lambda i, j: (j, 0))
    return pl.pallas_call(
        _inproj_kernel,
        grid=(b, s // tm),
        in_specs=[
            pl.BlockSpec((1, tm, d), lambda i, j: (i, j, 0)),
            const((1, d)), const((d, IN_WIDTH)), const((MIX_A, 2 * MIX_A)), const((Q_WIDTH, Q_WIDTH)),
            const((1, Q_WIDTH)), const((1, KV_WIDTH)), tab, tab, tab, tab,
        ],
        out_specs=[
            pl.BlockSpec((1, tm, MIX_A), lambda i, j: (i, j, 0)),
            pl.BlockSpec((1, tm, MIX_A), lambda i, j: (i, j, 0)),
            pl.BlockSpec((1, N_Q_HEADS, HEAD_DIM, tm), lambda i, j: (i, 0, 0, j)),
            pl.BlockSpec((1, N_KV_HEADS, tm, HEAD_DIM), lambda i, j: (i, 0, j, 0)),
            pl.BlockSpec((1, N_KV_HEADS, V_ROWS, tm), lambda i, j: (i, 0, 0, j)),
        ],
        out_shape=[
            jax.ShapeDtypeStruct((b, s, MIX_A), BF16),
            jax.ShapeDtypeStruct((b, s, MIX_A), BF16),
            jax.ShapeDtypeStruct((b, N_Q_HEADS, HEAD_DIM, s), BF16),
            jax.ShapeDtypeStruct((b, N_KV_HEADS, s, HEAD_DIM), BF16),
            jax.ShapeDtypeStruct((b, N_KV_HEADS, V_ROWS, s), BF16),
        ],
        compiler_params=_cparams(2),
        name="inproj",
    )(x, g[None, :], w_in.astype(BF16), gmat, ones, qg, kg, cq, sq, ck, sk)


def _seqdft_kernel(cs_ref, ss_ref, u_ref, v_ref, o_ref):
    o_ref[0] = (_dot(cs_ref[...], u_ref[0]) + _dot(ss_ref[...], v_ref[0])).astype(BF16)


def _dft_tables(n):
    idx = jnp.arange(n, dtype=jnp.int32)
    prod = (idx[:, None] * idx[None, :]) % n
    ang = prod.astype(F32) * (2.0 * math.pi / n)
    scale = n ** -0.5
    return (jnp.cos(ang) * scale).astype(BF16), (jnp.sin(ang) * scale).astype(BF16)


def _seqdft(u, v, tm):
    b, s, w = u.shape
    cs, ss = _dft_tables(s)
    return pl.pallas_call(
        _seqdft_kernel,
        grid=(b, s // tm),
        in_specs=[
            pl.BlockSpec((tm, s), lambda i, j: (j, 0)),
            pl.BlockSpec((tm, s), lambda i, j: (j, 0)),
            pl.BlockSpec((1, s, w), lambda i, j: (i, 0, 0)),
            pl.BlockSpec((1, s, w), lambda i, j: (i, 0, 0)),
        ],
        out_specs=pl.BlockSpec((1, tm, w), lambda i, j: (i, j, 0)),
        out_shape=jax.ShapeDtypeStruct((b, s, w), BF16),
        compiler_params=_cparams(2),
        name="seqdft",
    )(cs, ss, u, v)


def _attn_kernel(qt_ref, k_ref, vt_ref, o_ref, sa_ref, sb_ref, ma_ref, mb_ref, *, tq):
    seq = k_ref.shape[2]
    nblk = seq // tq

    def scores(i, s_ref, m_ref):
        off = pl.multiple_of(i * tq, tq)
        qt = jnp.concatenate([qt_ref[0, g, :, pl.ds(off, tq)] for g in range(GQA_GROUP)], axis=1)
        s = _dot(k_ref[0, 0], qt)
        s_ref[...] = s
        m_ref[...] = jnp.max(s, axis=0, keepdims=True)

    def finish(i, s_ref, m_ref):
        p = jnp.exp2(s_ref[...] - m_ref[...]).astype(BF16)
        acc = _dot(vt_ref[0, 0], p)
        o = acc[:HEAD_DIM] * (1.0 / acc[HEAD_DIM:HEAD_DIM + 1])
        off = pl.multiple_of(i * tq, tq)
        for g in range(GQA_GROUP):
            o_ref[0, g * HEAD_DIM:(g + 1) * HEAD_DIM, pl.ds(off, tq)] = (
                o[:, g * tq:(g + 1) * tq].astype(BF16))

    scores(0, sa_ref, ma_ref)

    def pair(j, carry):
        scores(2 * j + 1, sb_ref, mb_ref)
        finish(2 * j, sa_ref, ma_ref)
        scores(2 * j + 2, sa_ref, ma_ref)
        finish(2 * j + 1, sb_ref, mb_ref)
        return carry

    lax.fori_loop(0, nblk // 2 - 1, pair, 0)
    scores(nblk - 1, sb_ref, mb_ref)
    finish(nblk - 2, sa_ref, ma_ref)
    finish(nblk - 1, sb_ref, mb_ref)


def _attention(qt, k, vt, tq):
    b, _, _, s = qt.shape
    assert (s // tq) % 2 == 0
    width = GQA_GROUP * tq
    return pl.pallas_call(
        functools.partial(_attn_kernel, tq=tq),
        grid=(b, N_KV_HEADS),
        in_specs=[
            pl.BlockSpec((1, GQA_GROUP, HEAD_DIM, s), lambda i, g: (i, g, 0, 0)),
            pl.BlockSpec((1, 1, s, HEAD_DIM), lambda i, g: (i, g, 0, 0)),
            pl.BlockSpec((1, 1, V_ROWS, s), lambda i, g: (i, g, 0, 0)),
        ],
        out_specs=pl.BlockSpec((1, GQA_GROUP * HEAD_DIM, s), lambda i, g: (i, g, 0)),
        out_shape=jax.ShapeDtypeStruct((b, Q_WIDTH, s), BF16),
        scratch_shapes=[pltpu.VMEM((s, width), F32), pltpu.VMEM((s, width), F32),
                        pltpu.VMEM((1, width), F32), pltpu.VMEM((1, width), F32)],
        compiler_params=_cparams(2),
        name="gqa_attention",
    )(qt, k, vt)


def _outproj_kernel(fa_ref, att_ref, x_ref, wa_ref, wb_ref, o_ref):
    att = lax.dot_general(att_ref[0], wb_ref[...], (((0,), (0,)), ((), ())),
                          preferred_element_type=F32)
    o_ref[0] = x_ref[0] + (_dot(fa_ref[0], wa_ref[...]) + att)


def _outproj(fa, att, x, w_out, tm):
    b, s, d = x.shape
    w = w_out.astype(BF16)
    const = lambda shape: pl.BlockSpec(shape, lambda i, j: (0,) * len(shape))
    return pl.pallas_call(
        _outproj_kernel,
        grid=(b, s // tm),
        in_specs=[
            pl.BlockSpec((1, tm, MIX_A), lambda i, j: (i, j, 0)),
            pl.BlockSpec((1, Q_WIDTH, tm), lambda i, j: (i, 0, j)),
            pl.BlockSpec((1, tm, d), lambda i, j: (i, j, 0)),
            const((MIX_A, d)), const((Q_WIDTH, d)),
        ],
        out_specs=pl.BlockSpec((1, tm, d), lambda i, j: (i, j, 0)),
        out_shape=jax.ShapeDtypeStruct((b, s, d), F32),
        compiler_params=_cparams(2),
        name="outproj",
    )(fa, att, x, w[:MIX_A], w[MIX_A:])


def _mlp_kernel(x_ref, g_ref, w1_ref, w2_ref, gf_ref, o_ref, *, final_norm):
    x = x_ref[0]
    h = _rms_rows(x, g_ref[...]).astype(BF16)
    a = jnp.maximum(_dot(h, w1_ref[...]), 0.0)
    y = x + _dot((a * a).astype(BF16), w2_ref[...])
    if final_norm:
        y = _rms_rows(y, gf_ref[...])
    o_ref[0] = y


def _mlp(x, g, w1, w2, g_final, tm, final_norm):
    b, s, d = x.shape
    const = lambda shape: pl.BlockSpec(shape, lambda i, j: (0,) * len(shape),
                                       pipeline_mode=pl.Buffered(1))
    return pl.pallas_call(
        functools.partial(_mlp_kernel, final_norm=final_norm),
        grid=(b, s // tm),
        in_specs=[
            pl.BlockSpec((1, tm, d), lambda i, j: (i, j, 0)),
            const((1, d)), const((d, D_FF)), const((D_FF, d)), const((1, d)),
        ],
        out_specs=pl.BlockSpec((1, tm, d), lambda i, j: (i, j, 0)),
        out_shape=jax.ShapeDtypeStruct((b, s, d), F32),
        compiler_params=_cparams(2),
        name="mlp_final" if final_norm else "mlp",
    )(x, g[None, :], w1.astype(BF16), w2.astype(BF16), g_final[None, :])


def _normT_kernel(x_ref, g_ref, o_ref):
    o_ref[...] = _rms_rows(x_ref[0], g_ref[...]).T.astype(BF16)


def _norm_transpose(x, g, ts):
    b, s, d = x.shape
    nblk = s // ts
    return pl.pallas_call(
        _normT_kernel,
        grid=(b, nblk),
        in_specs=[pl.BlockSpec((1, ts, d), lambda i, j: (i, j, 0)),
                  pl.BlockSpec((1, d), lambda i, j: (0, 0))],
        out_specs=pl.BlockSpec((d, ts), lambda i, j: (0, i * nblk + j)),
        out_shape=jax.ShapeDtypeStruct((d, b * s), BF16),
        compiler_params=_cparams(2),
        name="norm_transpose",
    )(x, g[None, :])


def _cmul(ar, ai, br, bi):
    return ar * br - ai * bi, ar * bi + ai * br


def _cpow(a, theta, e):
    mag = jnp.exp(e * a)
    ang = e * theta
    return mag * jnp.cos(ang), mag * jnp.sin(ang)


def _ssm_kernel(ut_ref, prow_ref, pcol_ref, crow_ref, ccol_ref, brow_ref, y_ref,
                ktab_ref, mmat_ref, pmat_ref, cmat_ref, *, n_chunks):
    P = SSM_STATE
    C = SSM_GROUP_CH
    L = CHUNK
    rows = ut_ref.shape[2]

    prow = prow_ref[0]
    lre, lim = prow[0:1, :], prow[1:2, :]
    dt = jnp.exp(prow[2:3, :])
    a_row, th_row = lre * dt, lim * dt
    lbr, lbi = _cpow(a_row, th_row, 1.0)
    inv = 1.0 / (lre * lre + lim * lim)
    coef_r, coef_i = _cmul(lbr - 1.0, lbi, lre * inv, -lim * inv)
    bbr, bbi = _cmul(coef_r, coef_i, brow_ref[0, 0], brow_ref[0, 1])

    lane = lax.broadcasted_iota(jnp.int32, (L, 2 * P), 1)
    jj = lax.broadcasted_iota(jnp.int32, (L, 2 * P), 0)
    e_p = jnp.where(lane < P, L - 1 - jj, jj).astype(F32)
    pwr, pwi = _cpow(a_row, th_row, e_p)
    for c in range(C):
        xr, xi = _cmul(pwr, pwi, bbr[c:c + 1, :], bbi[c:c + 1, :])
        pmat_ref[c * L:(c + 1) * L, :2 * P] = xr.astype(BF16)
        pmat_ref[c * L:(c + 1) * L, 2 * P:] = xi.astype(BF16)

    pcol = pcol_ref[0]
    lre_c, lim_c = pcol[:, 0:1], pcol[:, 1:2]
    dt_c = jnp.exp(pcol[:, 2:3])
    a_col, th_col = lre_c * dt_c, lim_c * dt_c

    sub = lax.broadcasted_iota(jnp.int32, (2 * P, L), 0)
    ii = lax.broadcasted_iota(jnp.int32, (2 * P, L), 1)
    e_c = jnp.where(sub < P, ii + 1, L - ii).astype(F32)
    qwr, qwi = _cpow(a_col, th_col, e_c)
    ccr, cci = ccol_ref[0, 0], ccol_ref[0, 1]
    for c in range(C):
        xr, xi = _cmul(qwr, qwi, ccr[:, c:c + 1], cci[:, c:c + 1])
        cmat_ref[:2 * P, c * L:(c + 1) * L] = xr.astype(BF16)
        cmat_ref[2 * P:, c * L:(c + 1) * L] = (-xi).astype(BF16)

    sub2 = lax.broadcasted_iota(jnp.int32, (2 * P, 2 * L), 0)
    mm = lax.broadcasted_iota(jnp.int32, (2 * P, 2 * L), 1)
    fwd = sub2 < P
    e_k = jnp.where(fwd, mm, (2 * L - mm) % (2 * L)).astype(F32)
    keep = jnp.where(fwd, jnp.where(mm < L, 1.0, 0.0),
                     jnp.where(mm >= L, 1.0, jnp.where(mm == 0, 1.0, 0.0)))
    kwr, kwi = _cpow(a_col, th_col, e_k * keep)
    kwr = kwr * keep
    kwi = kwi * keep
    crr, cri = crow_ref[0, 0], crow_ref[0, 1]
    cb_r, cb_i = [], []
    for c in range(C):
        xr, xi = _cmul(crr, cri, bbr[c:c + 1, :], bbi[c:c + 1, :])
        cb_r.append(xr)
        cb_i.append(xi)
    cb_r = jnp.concatenate(cb_r, axis=0)
    cb_i = jnp.concatenate(cb_i, axis=0)
    ktab_ref[...] = _dot3(cb_r, kwr) - _dot3(cb_i, kwi)

    def fill(cp, carry):
        for c in range(C):
            row = ktab_ref[pl.ds(cp * C + c, 1), :]
            blk = pltpu.roll(jnp.broadcast_to(row, (L, 2 * L)), 0, 1, stride=1, stride_axis=0)
            mmat_ref[pl.ds(pl.multiple_of(cp * L, L), L), c * L:(c + 1) * L] = blk[:, :L].astype(BF16)
        return carry
    lax.fori_loop(0, C, fill, 0)

    u = jnp.concatenate([ut_ref[0, c] for c in range(C)], axis=1)
    y = _dot(u, mmat_ref[...])
    xloc = _dot(u, pmat_ref[...])
    xr, xi = xloc[:, :2 * P], xloc[:, 2 * P:]

    k_idx = lax.broadcasted_iota(jnp.int32, (rows, 2 * P), 0) % n_chunks
    is_f = lax.broadcasted_iota(jnp.int32, (rows, 2 * P), 1) < P
    ar, ai = _cpow(a_row, th_row, float(L))

    def shifted(t, step):
        down = pltpu.roll(t, step, 0)
        up = pltpu.roll(t, rows - step, 0)
        return jnp.where(is_f, jnp.where(k_idx >= step, down, 0.0),
                         jnp.where(k_idx < n_chunks - step, up, 0.0))

    step = 1
    while step < n_chunks:
        sr, si = shifted(xr, step), shifted(xi, step)
        pr, pi = _cmul(sr, si, ar, ai)
        xr, xi = xr + pr, xi + pi
        ar, ai = _cmul(ar, ai, ar, ai)
        step *= 2
    xprev = jnp.concatenate([shifted(xr, 1), shifted(xi, 1)], axis=1).astype(BF16)
    y = y + _dot(xprev, cmat_ref[...])
    for c in range(C):
        y_ref[0, c] = y[:, c * L:(c + 1) * L]


def _ssm(ut, lam_re, lam_im, log_dt, b_re, b_im, c_re, c_im, batch, seq):
    G, C, P, L = SSM_GROUPS, SSM_GROUP_CH, SSM_STATE, CHUNK
    n_chunks = seq // L
    rows = batch * n_chunks
    ut4 = ut.reshape(G, C, rows, L)
    f32 = lambda t: t.astype(F32)
    cat_p = lambda t: jnp.concatenate([f32(t[0]), f32(t[1])], axis=-1)
    ldt = jnp.broadcast_to(f32(log_dt)[:, :, None], (2, G, P))
    fields = jnp.stack([cat_p(lam_re), cat_p(lam_im), cat_p(ldt)], axis=1)
    prow = jnp.concatenate([fields, jnp.zeros((G, 5, 2 * P), F32)], axis=1)
    pcol = jnp.swapaxes(prow, 1, 2)
    crow = jnp.stack([jnp.concatenate([f32(c_re[0]), f32(c_re[1])], axis=-1),
                      jnp.concatenate([f32(c_im[0]), f32(c_im[1])], axis=-1)], axis=1)
    ccol = jnp.swapaxes(crow, 2, 3)
    brow = jnp.stack([jnp.concatenate([f32(b_re[0]), f32(b_re[1])], axis=1),
                      jnp.concatenate([f32(b_im[0]), f32(b_im[1])], axis=1)], axis=1)
    brow = jnp.swapaxes(brow, 2, 3)
    y4 = pl.pallas_call(
        functools.partial(_ssm_kernel, n_chunks=n_chunks),
        grid=(G,),
        in_specs=[
            pl.BlockSpec((1, C, rows, L), lambda g: (g, 0, 0, 0)),
            pl.BlockSpec((1, 8, 2 * P), lambda g: (g, 0, 0)),
            pl.BlockSpec((1, 2 * P, 8), lambda g: (g, 0, 0)),
            pl.BlockSpec((1, 2, C, 2 * P), lambda g: (g, 0, 0, 0)),
            pl.BlockSpec((1, 2, 2 * P, C), lambda g: (g, 0, 0, 0)),
            pl.BlockSpec((1, 2, C, 2 * P), lambda g: (g, 0, 0, 0)),
        ],
        out_specs=pl.BlockSpec((1, C, rows, L), lambda g: (g, 0, 0, 0)),
        out_shape=jax.ShapeDtypeStruct((G, C, rows, L), F32),
        scratch_shapes=[
            pltpu.VMEM((C * C, 2 * L), F32),
            pltpu.VMEM((C * L, C * L), BF16),
            pltpu.VMEM((C * L, 4 * P), BF16),
            pltpu.VMEM((4 * P, C * L), BF16),
        ],
        compiler_params=_cparams(1),
        name="s5_scan",
    )(ut4, prow, pcol, crow, ccol, brow)
    return y4.reshape(G * C, batch * seq)


def _gelu_tanh(y):
    return 0.5 * y * (1.0 + jnp.tanh(math.sqrt(2.0 / math.pi) * (y + 0.044715 * (y * y * y))))


def _gate_kernel(yt_ref, x_ref, g_ref, dsk_ref, wg_ref, bg_ref, o_ref):
    x = x_ref[0]
    u = _rms_rows(x, g_ref[...])
    y = yt_ref[...].T + dsk_ref[...] * u
    gl = _gelu_tanh(y)
    gate = _dot(gl.astype(BF16), wg_ref[...]) + bg_ref[...]
    o_ref[0] = x + gl * (1.0 / (1.0 + jnp.exp(-gate)))


def _gate(yt, x, g, d_skip, w_gate, b_gate, ts):
    b, s, d = x.shape
    nblk = s // ts
    const = lambda shape: pl.BlockSpec(shape, lambda i, j: (0,) * len(shape))
    return pl.pallas_call(
        _gate_kernel,
        grid=(b, nblk),
        in_specs=[
            pl.BlockSpec((d, ts), lambda i, j: (0, i * nblk + j)),
            pl.BlockSpec((1, ts, d), lambda i, j: (i, j, 0)),
            const((1, d)), const((1, d)), const((d, d)), const((1, d)),
        ],
        out_specs=pl.BlockSpec((1, ts, d), lambda i, j: (i, j, 0)),
        out_shape=jax.ShapeDtypeStruct((b, s, d), F32),
        compiler_params=_cparams(2),
        name="s5_gate",
    )(yt, x, g[None, :], d_skip.astype(F32)[None, :], w_gate.astype(BF16), b_gate.astype(F32)[None, :])


def _pick(n, pref):
    t = min(n, pref)
    while n % t:
        t //= 2
    return t


def kernel(x, norm_mix, norm_mlp, mlp_w1, mlp_w2, w_in, w_fnet, q_norm, k_norm, w_out, lam_re, lam_im, log_dt, b_re, b_im, c_re, c_im, d_skip, w_gate, b_gate, final_norm):
    b, s, d = x.shape
    assert d == D_MODEL and s % CHUNK == 0 and s % GRID_W == 0
    tm = _pick(s, 512)
    gmat = _fourier_fold(w_fnet[0])
    u, v, q, k, vv = _inproj(x, norm_mix[0], w_in[0], gmat, q_norm[0], k_norm[0], tm)
    fa = _seqdft(u, v, tm)
    att = _attention(q, k, vv, _pick(s, 128))
    x = _outproj(fa, att, x, w_out[0], tm)
    x = _mlp(x, norm_mlp[0], mlp_w1[0], mlp_w2[0], final_norm, tm, False)
    ut = _norm_transpose(x, norm_mix[1], tm)
    yt = _ssm(ut, lam_re[0], lam_im[0], log_dt[0], b_re[0], b_im[0], c_re[0], c_im[0], b, s)
    x = _gate(yt, x, norm_mix[1], d_skip[0], w_gate[0], b_gate[0], tm)
    return _mlp(x, norm_mlp[1], mlp_w1[1], mlp_w2[1], final_norm, tm, True)
```

```python
import functools
import math

import numpy as np
import jax
import jax.numpy as jnp
from jax import lax
from jax.experimental import pallas as pl
from jax.experimental.pallas import tpu as pltpu

F32 = jnp.float32
BF16 = jnp.bfloat16

EPS = 1e-6
D_MODEL = 1024
MIX_A = 512
FNET_HEAD_DIM = 64
FNET_HEADS = MIX_A // FNET_HEAD_DIM
HEAD_DIM = 64
N_Q_HEADS = 8
N_KV_HEADS = 2
GQA_GROUP = N_Q_HEADS // N_KV_HEADS
Q_WIDTH = N_Q_HEADS * HEAD_DIM
KV_WIDTH = N_KV_HEADS * HEAD_DIM
IN_WIDTH = MIX_A + Q_WIDTH + 2 * KV_WIDTH
GRID_W = 64
ROPE_THETA = 10000.0
ROPE_AXIS_DIM = HEAD_DIM // 2
SSM_GROUP_CH = 16
SSM_GROUPS = D_MODEL // SSM_GROUP_CH
SSM_STATE = 64
D_FF = 4 * D_MODEL
CHUNK = 128
LANES = 128
LOG2_E = 1.4426950408889634
V_ROWS = HEAD_DIM + 16
VMEM_LIMIT = 56 * 1024 * 1024


def _cparams(n_grid_dims):
    return pltpu.CompilerParams(
        dimension_semantics=("arbitrary",) * n_grid_dims, vmem_limit_bytes=VMEM_LIMIT)


def _dot(a, b):
    return jnp.dot(a, b, preferred_element_type=F32)


def _split_bf16(a):
    hi = a.astype(BF16)
    lo = (a - hi.astype(F32)).astype(BF16)
    return hi, lo


def _dot3(a, b):
    a_hi, a_lo = _split_bf16(a)
    b_hi, b_lo = _split_bf16(b)
    return _dot(a_hi, b_hi) + (_dot(a_hi, b_lo) + _dot(a_lo, b_hi))


def _rms_rows(x, g):
    ms = jnp.mean(x * x, axis=-1, keepdims=True)
    return x * lax.rsqrt(ms + EPS) * g


def _fold_kernel(cc_ref, sc_ref, w_ref, g_ref):
    w = w_ref[...]
    g_ref[:, :MIX_A] = _dot3(cc_ref[...], w).astype(BF16)
    g_ref[:, MIX_A:] = (-_dot3(sc_ref[...], w)).astype(BF16)


def _fourier_fold(w_fnet):
    n = np.arange(FNET_HEAD_DIM)
    ang = 2.0 * np.pi * np.outer(n, n) / FNET_HEAD_DIM
    scale = FNET_HEAD_DIM ** -0.5
    eye = np.eye(FNET_HEADS)
    cc = jnp.asarray(np.kron(eye, np.cos(ang) * scale), F32)
    sc = jnp.asarray(np.kron(eye, np.sin(ang) * scale), F32)
    eye_j = jnp.eye(FNET_HEADS, dtype=F32)
    w_bd = (eye_j[:, None, :, None] * w_fnet.astype(F32)[:, :, None, :]).reshape(MIX_A, MIX_A)
    return pl.pallas_call(
        _fold_kernel,
        out_shape=jax.ShapeDtypeStruct((MIX_A, 2 * MIX_A), BF16),
        name="fourier_fold",
    )(cc, sc, w_bd)


def _head_norm(t, ones, gain):
    hi, lo = _split_bf16(t * t)
    ss = _dot(hi, ones) + _dot(lo, ones)
    return t * lax.rsqrt(ss * (1.0 / HEAD_DIM) + EPS) * gain


def _rope(t, cos, sin):
    width = t.shape[1]
    reps = width // LANES
    c = jnp.concatenate([cos] * reps, axis=1) if reps > 1 else cos
    s = jnp.concatenate([sin] * reps, axis=1) if reps > 1 else sin
    lane = lax.broadcasted_iota(jnp.int32, t.shape, 1)
    first_half = (lane % ROPE_AXIS_DIM) < (ROPE_AXIS_DIM // 2)
    partner = jnp.where(first_half,
                        pltpu.roll(t, width - ROPE_AXIS_DIM // 2, 1),
                        pltpu.roll(t, ROPE_AXIS_DIM // 2, 1))
    return t * c + partner * s


def _inproj_kernel(x_ref, g_ref, win_ref, gmat_ref, ones_ref, qg_ref, kg_ref,
                   cq_ref, sq_ref, ck_ref, sk_ref,
                   u_ref, v_ref, q_ref, k_ref, vv_ref):
    h = _rms_rows(x_ref[0], g_ref[...]).astype(BF16)
    z = _dot(h, win_ref[...])
    uv = _dot(z[:, :MIX_A].astype(BF16), gmat_ref[...])
    u_ref[0] = uv[:, :MIX_A].astype(BF16)
    v_ref[0] = uv[:, MIX_A:].astype(BF16)
    q = z[:, MIX_A:MIX_A + Q_WIDTH]
    k = z[:, MIX_A + Q_WIDTH:MIX_A + Q_WIDTH + KV_WIDTH]
    v = z[:, MIX_A + Q_WIDTH + KV_WIDTH:]
    ones = ones_ref[...]
    q = _rope(_head_norm(q, ones, qg_ref[...]), cq_ref[...], sq_ref[...])
    k = _rope(_head_norm(k, ones[:KV_WIDTH, :KV_WIDTH], kg_ref[...]), ck_ref[...], sk_ref[...])
    tm = q.shape[0]
    q_ref[0] = q.T.reshape(N_Q_HEADS, HEAD_DIM, tm).astype(BF16)
    for hh in range(N_KV_HEADS):
        k_ref[0, hh] = k[:, hh * HEAD_DIM:(hh + 1) * HEAD_DIM].astype(BF16)
    vv_ref[0, :, :HEAD_DIM, :] = v.T.reshape(N_KV_HEADS, HEAD_DIM, tm).astype(BF16)
    vv_ref[0, :, HEAD_DIM:, :] = jnp.ones((N_KV_HEADS, V_ROWS - HEAD_DIM, tm), BF16)


def _rope_tables(seq_len):
    rows = seq_len // GRID_W
    row = jnp.broadcast_to(jnp.arange(rows, dtype=F32)[:, None], (rows, GRID_W)).reshape(seq_len)
    col = jnp.broadcast_to(jnp.arange(GRID_W, dtype=F32)[None, :], (rows, GRID_W)).reshape(seq_len)
    inv_freq = ROPE_THETA ** (-jnp.arange(0, ROPE_AXIS_DIM, 2, dtype=F32) / ROPE_AXIS_DIM)
    ar = row[:, None] * inv_freq
    ac = col[:, None] * inv_freq
    cos = jnp.concatenate([jnp.cos(ar), jnp.cos(ar), jnp.cos(ac), jnp.cos(ac)], axis=1)
    sin = jnp.concatenate([-jnp.sin(ar), jnp.sin(ar), -jnp.sin(ac), jnp.sin(ac)], axis=1)
    cos = jnp.concatenate([cos, cos], axis=1)
    sin = jnp.concatenate([sin, sin], axis=1)
    scale = HEAD_DIM ** -0.5 * LOG2_E
    return cos * scale, sin * scale, cos, sin


def _inproj(x, g, w_in, gmat, q_norm, k_norm, tm):
    b, s, d = x.shape
    cq, sq, ck, sk = _rope_tables(s)
    head = np.arange(Q_WIDTH) // HEAD_DIM
    ones = jnp.asarray(head[:, None] == head[None, :], BF16)
    qg = jnp.tile(q_norm.astype(F32), N_Q_HEADS)[None, :]
    kg = jnp.tile(k_norm.astype(F32), N_KV_HEADS)[None, :]
    const = lambda shape: pl.BlockSpec(shape, lambda i, j: (0,) * len(shape))
    tab = pl.BlockSpec((tm, LANES), lambda i, j: (j, 0))
    return pl.pallas_call(
        _inproj_kernel,
        grid=(b, s // tm),
        in_specs=[
            pl.BlockSpec((1, tm, d), lambda i, j: (i, j, 0)),
            const((1, d)), const((d, IN_WIDTH)), const((MIX_A, 2 * MIX_A)), const((Q_WIDTH, Q_WIDTH)),
            const((1, Q_WIDTH)), const((1, KV_WIDTH)), tab, tab, tab, tab,
        ],
        out_specs=[
            pl.BlockSpec((1, tm, MIX_A), lambda i, j: (i, j, 0)),
            pl.BlockSpec((1, tm, MIX_A), lambda i, j: (i, j, 0)),
            pl.BlockSpec((1, N_Q_HEADS, HEAD_DIM, tm), lambda i, j: (i, 0, 0, j)),
            pl.BlockSpec((1, N_KV_HEADS, tm, HEAD_DIM), lambda i, j: (i, 0, j, 0)),
            pl.BlockSpec((1, N_KV_HEADS, V_ROWS, tm), lambda i, j: (i, 0, 0, j)),
        ],
        out_shape=[
            jax.ShapeDtypeStruct((b, s, MIX_A), BF16),
            jax.ShapeDtypeStruct((b, s, MIX_A), BF16),
            jax.ShapeDtypeStruct((b, N_Q_HEADS, HEAD_DIM, s), BF16),
            jax.ShapeDtypeStruct((b, N_KV_HEADS, s, HEAD_DIM), BF16),
            jax.ShapeDtypeStruct((b, N_KV_HEADS, V_ROWS, s), BF16),
        ],
        compiler_params=_cparams(2),
        name="inproj",
    )(x, g[None, :], w_in.astype(BF16), gmat, ones, qg, kg, cq, sq, ck, sk)


def _dft_tables_kernel(bc_ref, bs_ref, rot_ref, c_ref, s_ref):
    re, im = bc_ref[...], bs_ref[...]
    c_ref[:, :LANES] = re.astype(BF16)
    s_ref[:, :LANES] = im.astype(BF16)
    w, stage = LANES, 0
    while w < c_ref.shape[1]:
        cr = rot_ref[:, 2 * stage:2 * stage + 1]
        sr = rot_ref[:, 2 * stage + 1:2 * stage + 2]
        nre, nim = _cmul(re, im, cr, sr)
        c_ref[:, w:2 * w] = nre.astype(BF16)
        s_ref[:, w:2 * w] = nim.astype(BF16)
        re = jnp.concatenate([re, nre], axis=1)
        im = jnp.concatenate([im, nim], axis=1)
        w, stage = 2 * w, stage + 1


def _dft_tables(seq, tr):
    h = seq // 2
    scale = seq ** -0.5
    j = jnp.arange(h, dtype=jnp.int32)[:, None]
    k = jnp.arange(LANES, dtype=jnp.int32)[None, :]
    unit = 2.0 * math.pi / seq
    ang = ((j * k) % seq).astype(F32) * unit
    n_stage = int(round(math.log2(h // LANES)))
    widths = jnp.asarray([LANES << st for st in range(n_stage)] + [0] * (4 - n_stage), jnp.int32)
    rang = ((j * widths[None, :]) % seq).astype(F32) * unit
    rot = jnp.stack([jnp.cos(rang), jnp.sin(rang)], axis=-1).reshape(h, 8)
    return pl.pallas_call(
        _dft_tables_kernel,
        grid=(h // tr,),
        in_specs=[pl.BlockSpec((tr, LANES), lambda i: (i, 0)),
                  pl.BlockSpec((tr, LANES), lambda i: (i, 0)),
                  pl.BlockSpec((tr, 8), lambda i: (i, 0))],
        out_specs=[pl.BlockSpec((tr, h), lambda i: (i, 0)), pl.BlockSpec((tr, h), lambda i: (i, 0))],
        out_shape=[jax.ShapeDtypeStruct((h, h), BF16), jax.ShapeDtypeStruct((h, h), BF16)],
        compiler_params=_cparams(1),
        name="dft_tables",
    )(jnp.cos(ang) * scale, jnp.sin(ang) * scale, rot)


def _seqdft_kernel(ct_ref, st_ref, u_ref, v_ref, o_ref, ue_ref, vo_ref, d_ref, *, tm, blk):
    t = pl.program_id(1)
    seq = u_ref.shape[1]
    h = seq // 2
    width = u_ref.shape[2]
    scale = seq ** -0.5
    rr = lax.broadcasted_iota(jnp.int32, (blk, blk), 0)
    cc = lax.broadcasted_iota(jnp.int32, (blk, blk), 1)
    flip = jnp.where(rr + cc == blk - 1, 1.0, 0.0).astype(BF16)
    row = lax.broadcasted_iota(jnp.int32, (h, width), 0)

    @pl.when(t == 0)
    def _fold_halves():
        for src_ref, dst_ref, sign in ((u_ref, ue_ref, 1.0), (v_ref, vo_ref, -1.0)):
            rev = jnp.concatenate(
                [_dot(flip, src_ref[0, seq - blk * (a + 1):seq - blk * a, :]) for a in range(h // blk)],
                axis=0)
            mirror = jnp.where(row == 0, 0.0, pltpu.roll(rev, 1, 0))
            dst_ref[...] = (src_ref[0, :h, :].astype(F32) + sign * mirror).astype(BF16)

    ue = ue_ref[...]
    x_mid = u_ref[0, h:h + 1, :].astype(F32) * scale
    p = _dot(ct_ref[...], ue)
    q = _dot(st_ref[...], vo_ref[...])
    j = t * tm + lax.broadcasted_iota(jnp.int32, (tm, 1), 0)
    p = p + (1 - 2 * (j & 1)).astype(F32) * x_mid
    off = pl.multiple_of(t * tm, tm)
    o_ref[0, pl.ds(off, tm), :] = (p + q).astype(BF16)
    d_ref[pl.ds(off, tm), :] = p - q

    @pl.when(t == pl.num_programs(1) - 1)
    def _mirror_half():
        kk = lax.broadcasted_iota(jnp.int32, (8, h), 1)
        alt = ((1 - 2 * (kk & 1)).astype(F32) * scale).astype(BF16)
        p_mid = _dot(alt, ue)[0:1, :] + x_mid
        d_up = jnp.where(row == h - 1, p_mid, pltpu.roll(d_ref[...], h - 1, 0)).astype(BF16)
        for a in range(h // blk):
            o_ref[0, h + blk * a:h + blk * (a + 1), :] = _dot(
                flip, d_up[h - blk * (a + 1):h - blk * a, :]).astype(BF16)


def _seqdft(u, v):
    b, s, w = u.shape
    h = s // 2
    tm = _pick(h, 512)
    blk = _pick(h, 256)
    ct, st = _dft_tables(s, _pick(h, 256))
    return pl.pallas_call(
        functools.partial(_seqdft_kernel, tm=tm, blk=blk),
        grid=(b, h // tm),
        in_specs=[
            pl.BlockSpec((tm, h), lambda i, j: (j, 0)),
            pl.BlockSpec((tm, h), lambda i, j: (j, 0)),
            pl.BlockSpec((1, s, w), lambda i, j: (i, 0, 0)),
            pl.BlockSpec((1, s, w), lambda i, j: (i, 0, 0)),
        ],
        out_specs=pl.BlockSpec((1, s, w), lambda i, j: (i, 0, 0)),
        out_shape=jax.ShapeDtypeStruct((b, s, w), BF16),
        scratch_shapes=[pltpu.VMEM((h, w), BF16), pltpu.VMEM((h, w), BF16), pltpu.VMEM((h, w), F32)],
        compiler_params=_cparams(2),
        name="seqdft",
    )(ct, st, u, v)


def _attn_kernel(qt_ref, k_ref, vt_ref, o_ref, sa_ref, sb_ref, ma_ref, mb_ref, *, tq):
    seq = k_ref.shape[2]
    nblk = seq // tq

    def scores(i, s_ref, m_ref):
        off = pl.multiple_of(i * tq, tq)
        qt = jnp.concatenate([qt_ref[0, g, :, pl.ds(off, tq)] for g in range(GQA_GROUP)], axis=1)
        s = _dot(k_ref[0, 0], qt)
        s_ref[...] = s
        m_ref[...] = jnp.max(s, axis=0, keepdims=True)

    def finish(i, s_ref, m_ref):
        p = jnp.exp2(s_ref[...] - m_ref[...]).astype(BF16)
        acc = _dot(vt_ref[0, 0], p)
        o = acc[:HEAD_DIM] * (1.0 / acc[HEAD_DIM:HEAD_DIM + 1])
        off = pl.multiple_of(i * tq, tq)
        for g in range(GQA_GROUP):
            o_ref[0, g * HEAD_DIM:(g + 1) * HEAD_DIM, pl.ds(off, tq)] = (
                o[:, g * tq:(g + 1) * tq].astype(BF16))

    scores(0, sa_ref, ma_ref)

    def pair(j, carry):
        scores(2 * j + 1, sb_ref, mb_ref)
        finish(2 * j, sa_ref, ma_ref)
        scores(2 * j + 2, sa_ref, ma_ref)
        finish(2 * j + 1, sb_ref, mb_ref)
        return carry

    lax.fori_loop(0, nblk // 2 - 1, pair, 0)
    scores(nblk - 1, sb_ref, mb_ref)
    finish(nblk - 2, sa_ref, ma_ref)
    finish(nblk - 1, sb_ref, mb_ref)


def _attention(qt, k, vt, tq):
    b, _, _, s = qt.shape
    assert (s // tq) % 2 == 0
    width = GQA_GROUP * tq
    return pl.pallas_call(
        functools.partial(_attn_kernel, tq=tq),
        grid=(b, N_KV_HEADS),
        in_specs=[
            pl.BlockSpec((1, GQA_GROUP, HEAD_DIM, s), lambda i, g: (i, g, 0, 0)),
            pl.BlockSpec((1, 1, s, HEAD_DIM), lambda i, g: (i, g, 0, 0)),
            pl.BlockSpec((1, 1, V_ROWS, s), lambda i, g: (i, g, 0, 0)),
        ],
        out_specs=pl.BlockSpec((1, GQA_GROUP * HEAD_DIM, s), lambda i, g: (i, g, 0)),
        out_shape=jax.ShapeDtypeStruct((b, Q_WIDTH, s), BF16),
        scratch_shapes=[pltpu.VMEM((s, width), F32), pltpu.VMEM((s, width), F32),
                        pltpu.VMEM((1, width), F32), pltpu.VMEM((1, width), F32)],
        compiler_params=_cparams(2),
        name="gqa_attention",
    )(qt, k, vt)


def _mlp_block(x, g, w1_ref, w2_ref):
    h = _rms_rows(x, g).astype(BF16)
    a = jnp.maximum(_dot(h, w1_ref[...]), 0.0)
    return x + _dot((a * a).astype(BF16), w2_ref[...])


def _tail0_kernel(fa_ref, att_ref, x_ref, wa_ref, wb_ref, g_ref, w1_ref, w2_ref, gn_ref,
                  o_ref, ut_ref):
    att = lax.dot_general(att_ref[0], wb_ref[...], (((0,), (0,)), ((), ())),
                          preferred_element_type=F32)
    x1 = x_ref[0] + (_dot(fa_ref[0], wa_ref[...]) + att)
    x2 = _mlp_block(x1, g_ref[...], w1_ref, w2_ref)
    o_ref[0] = x2
    ut_ref[...] = _rms_rows(x2, gn_ref[...]).T.astype(BF16)


def _tail0(fa, att, x, w_out, g_mlp, w1, w2, g_next, tm):
    b, s, d = x.shape
    nblk = s // tm
    w = w_out.astype(BF16)
    const = lambda shape: pl.BlockSpec(shape, lambda i, j: (0,) * len(shape),
                                       pipeline_mode=pl.Buffered(1))
    return pl.pallas_call(
        _tail0_kernel,
        grid=(b, nblk),
        in_specs=[
            pl.BlockSpec((1, tm, MIX_A), lambda i, j: (i, j, 0)),
            pl.BlockSpec((1, Q_WIDTH, tm), lambda i, j: (i, 0, j)),
            pl.BlockSpec((1, tm, d), lambda i, j: (i, j, 0)),
            const((MIX_A, d)), const((Q_WIDTH, d)),
            const((1, d)), const((d, D_FF)), const((D_FF, d)), const((1, d)),
        ],
        out_specs=[pl.BlockSpec((1, tm, d), lambda i, j: (i, j, 0)),
                   pl.BlockSpec((d, tm), lambda i, j: (0, i * nblk + j))],
        out_shape=[jax.ShapeDtypeStruct((b, s, d), F32),
                   jax.ShapeDtypeStruct((d, b * s), BF16)],
        compiler_params=_cparams(2),
        name="outproj_mlp",
    )(fa, att, x, w[:MIX_A], w[MIX_A:], g_mlp[None, :], w1.astype(BF16), w2.astype(BF16),
      g_next[None, :])


def _mlp_kernel(x_ref, g_ref, w1_ref, w2_ref, gf_ref, o_ref, *, final_norm):
    y = _mlp_block(x_ref[0], g_ref[...], w1_ref, w2_ref)
    if final_norm:
        y = _rms_rows(y, gf_ref[...])
    o_ref[0] = y


def _mlp(x, g, w1, w2, g_final, tm, final_norm):
    b, s, d = x.shape
    const = lambda shape: pl.BlockSpec(shape, lambda i, j: (0,) * len(shape),
                                       pipeline_mode=pl.Buffered(1))
    return pl.pallas_call(
        functools.partial(_mlp_kernel, final_norm=final_norm),
        grid=(b, s // tm),
        in_specs=[
            pl.BlockSpec((1, tm, d), lambda i, j: (i, j, 0)),
            const((1, d)), const((d, D_FF)), const((D_FF, d)), const((1, d)),
        ],
        out_specs=pl.BlockSpec((1, tm, d), lambda i, j: (i, j, 0)),
        out_shape=jax.ShapeDtypeStruct((b, s, d), F32),
        compiler_params=_cparams(2),
        name="mlp_final" if final_norm else "mlp",
    )(x, g[None, :], w1.astype(BF16), w2.astype(BF16), g_final[None, :])


def _cmul(ar, ai, br, bi):
    return ar * br - ai * bi, ar * bi + ai * br


def _cpow(a, theta, e):
    mag = jnp.exp(e * a)
    ang = e * theta
    return mag * jnp.cos(ang), mag * jnp.sin(ang)


def _ssm_kernel(ut_ref, prow_ref, pcol_ref, crow_ref, ccol_ref, brow_ref, y_ref,
                ktab_ref, mmat_ref, pmat_ref, cmat_ref, *, n_chunks):
    P = SSM_STATE
    C = SSM_GROUP_CH
    L = CHUNK
    rows = ut_ref.shape[2]

    prow = prow_ref[0]
    lre, lim = prow[0:1, :], prow[1:2, :]
    dt = jnp.exp(prow[2:3, :])
    a_row, th_row = lre * dt, lim * dt
    lbr, lbi = _cpow(a_row, th_row, 1.0)
    inv = 1.0 / (lre * lre + lim * lim)
    coef_r, coef_i = _cmul(lbr - 1.0, lbi, lre * inv, -lim * inv)
    bbr, bbi = _cmul(coef_r, coef_i, brow_ref[0, 0], brow_ref[0, 1])

    lane = lax.broadcasted_iota(jnp.int32, (L, 2 * P), 1)
    jj = lax.broadcasted_iota(jnp.int32, (L, 2 * P), 0)
    e_p = jnp.where(lane < P, L - 1 - jj, jj).astype(F32)
    pwr, pwi = _cpow(a_row, th_row, e_p)
    for c in range(C):
        xr, xi = _cmul(pwr, pwi, bbr[c:c + 1, :], bbi[c:c + 1, :])
        pmat_ref[c * L:(c + 1) * L, :2 * P] = xr.astype(BF16)
        pmat_ref[c * L:(c + 1) * L, 2 * P:] = xi.astype(BF16)

    pcol = pcol_ref[0]
    lre_c, lim_c = pcol[:, 0:1], pcol[:, 1:2]
    dt_c = jnp.exp(pcol[:, 2:3])
    a_col, th_col = lre_c * dt_c, lim_c * dt_c

    sub = lax.broadcasted_iota(jnp.int32, (2 * P, L), 0)
    ii = lax.broadcasted_iota(jnp.int32, (2 * P, L), 1)
    e_c = jnp.where(sub < P, ii + 1, L - ii).astype(F32)
    qwr, qwi = _cpow(a_col, th_col, e_c)
    ccr, cci = ccol_ref[0, 0], ccol_ref[0, 1]
    for c in range(C):
        xr, xi = _cmul(qwr, qwi, ccr[:, c:c + 1], cci[:, c:c + 1])
        cmat_ref[:2 * P, c * L:(c + 1) * L] = xr.astype(BF16)
        cmat_ref[2 * P:, c * L:(c + 1) * L] = (-xi).astype(BF16)

    sub2 = lax.broadcasted_iota(jnp.int32, (2 * P, 2 * L), 0)
    mm = lax.broadcasted_iota(jnp.int32, (2 * P, 2 * L), 1)
    fwd = sub2 < P
    e_k = jnp.where(fwd, mm, (2 * L - mm) % (2 * L)).astype(F32)
    keep = jnp.where(fwd, jnp.where(mm < L, 1.0, 0.0),
                     jnp.where(mm >= L, 1.0, jnp.where(mm == 0, 1.0, 0.0)))
    kwr, kwi = _cpow(a_col, th_col, e_k * keep)
    kwr = kwr * keep
    kwi = kwi * keep
    crr, cri = crow_ref[0, 0], crow_ref[0, 1]
    cb_r, cb_i = [], []
    for c in range(C):
        xr, xi = _cmul(crr, cri, bbr[c:c + 1, :], bbi[c:c + 1, :])
        cb_r.append(xr)
        cb_i.append(xi)
    cb_r = jnp.concatenate(cb_r, axis=0)
    cb_i = jnp.concatenate(cb_i, axis=0)
    ktab_ref[...] = _dot3(cb_r, kwr) - _dot3(cb_i, kwi)

    def fill(cp, carry):
        for c in range(C):
            row = ktab_ref[pl.ds(cp * C + c, 1), :]
            blk = pltpu.roll(jnp.broadcast_to(row, (L, 2 * L)), 0, 1, stride=1, stride_axis=0)
            mmat_ref[pl.ds(pl.multiple_of(cp * L, L), L), c * L:(c + 1) * L] = blk[:, :L].astype(BF16)
        return carry
    lax.fori_loop(0, C, fill, 0)

    u = jnp.concatenate([ut_ref[0, c] for c in range(C)], axis=1)
    y = _dot(u, mmat_ref[...])
    xloc = _dot(u, pmat_ref[...])
    xr, xi = xloc[:, :2 * P], xloc[:, 2 * P:]

    k_idx = lax.broadcasted_iota(jnp.int32, (rows, 2 * P), 0) % n_chunks
    is_f = lax.broadcasted_iota(jnp.int32, (rows, 2 * P), 1) < P
    ar, ai = _cpow(a_row, th_row, float(L))

    def shifted(t, step):
        down = pltpu.roll(t, step, 0)
        up = pltpu.roll(t, rows - step, 0)
        return jnp.where(is_f, jnp.where(k_idx >= step, down, 0.0),
                         jnp.where(k_idx < n_chunks - step, up, 0.0))

    step = 1
    while step < n_chunks:
        sr, si = shifted(xr, step), shifted(xi, step)
        pr, pi = _cmul(sr, si, ar, ai)
        xr, xi = xr + pr, xi + pi
        ar, ai = _cmul(ar, ai, ar, ai)
        step *= 2
    xprev = jnp.concatenate([shifted(xr, 1), shifted(xi, 1)], axis=1).astype(BF16)
    y = y + _dot(xprev, cmat_ref[...])
    for c in range(C):
        y_ref[0, c] = y[:, c * L:(c + 1) * L]


def _ssm(ut, lam_re, lam_im, log_dt, b_re, b_im, c_re, c_im, batch, seq):
    G, C, P, L = SSM_GROUPS, SSM_GROUP_CH, SSM_STATE, CHUNK
    n_chunks = seq // L
    rows = batch * n_chunks
    ut4 = ut.reshape(G, C, rows, L)
    f32 = lambda t: t.astype(F32)
    cat_p = lambda t: jnp.concatenate([f32(t[0]), f32(t[1])], axis=-1)
    ldt = jnp.broadcast_to(f32(log_dt)[:, :, None], (2, G, P))
    fields = jnp.stack([cat_p(lam_re), cat_p(lam_im), cat_p(ldt)], axis=1)
    prow = jnp.concatenate([fields, jnp.zeros((G, 5, 2 * P), F32)], axis=1)
    pcol = jnp.swapaxes(prow, 1, 2)
    crow = jnp.stack([jnp.concatenate([f32(c_re[0]), f32(c_re[1])], axis=-1),
                      jnp.concatenate([f32(c_im[0]), f32(c_im[1])], axis=-1)], axis=1)
    ccol = jnp.swapaxes(crow, 2, 3)
    brow = jnp.stack([jnp.concatenate([f32(b_re[0]), f32(b_re[1])], axis=1),
                      jnp.concatenate([f32(b_im[0]), f32(b_im[1])], axis=1)], axis=1)
    brow = jnp.swapaxes(brow, 2, 3)
    return pl.pallas_call(
        functools.partial(_ssm_kernel, n_chunks=n_chunks),
        grid=(G,),
        in_specs=[
            pl.BlockSpec((1, C, rows, L), lambda g: (g, 0, 0, 0)),
            pl.BlockSpec((1, 8, 2 * P), lambda g: (g, 0, 0)),
            pl.BlockSpec((1, 2 * P, 8), lambda g: (g, 0, 0)),
            pl.BlockSpec((1, 2, C, 2 * P), lambda g: (g, 0, 0, 0)),
            pl.BlockSpec((1, 2, 2 * P, C), lambda g: (g, 0, 0, 0)),
            pl.BlockSpec((1, 2, C, 2 * P), lambda g: (g, 0, 0, 0)),
        ],
        out_specs=pl.BlockSpec((1, C, rows, L), lambda g: (g, 0, 0, 0)),
        out_shape=jax.ShapeDtypeStruct((G, C, rows, L), F32),
        scratch_shapes=[
            pltpu.VMEM((C * C, 2 * L), F32),
            pltpu.VMEM((C * L, C * L), BF16),
            pltpu.VMEM((C * L, 4 * P), BF16),
            pltpu.VMEM((4 * P, C * L), BF16),
        ],
        compiler_params=_cparams(1),
        name="s5_scan",
    )(ut4, prow, pcol, crow, ccol, brow)


def _gelu_tanh(y):
    return 0.5 * y * (1.0 + jnp.tanh(math.sqrt(2.0 / math.pi) * (y + 0.044715 * (y * y * y))))


def _gate_kernel(yt_ref, x_ref, g_ref, dsk_ref, wg_ref, bg_ref, o_ref):
    x = x_ref[0]
    u = _rms_rows(x, g_ref[...])
    n_rows = yt_ref.shape[2]
    y_rct = jnp.swapaxes(yt_ref[...].reshape(D_MODEL, n_rows, CHUNK), 0, 1)
    y = jnp.concatenate([y_rct[r].T for r in range(n_rows)], axis=0) + dsk_ref[...] * u
    gl = _gelu_tanh(y)
    gate = _dot(gl.astype(BF16), wg_ref[...]) + bg_ref[...]
    o_ref[0] = x + gl * (1.0 / (1.0 + jnp.exp(-gate)))


def _gate(y4, x, g, d_skip, w_gate, b_gate, ts):
    b, s, d = x.shape
    nblk = s // ts
    n_rows = ts // CHUNK
    const = lambda shape: pl.BlockSpec(shape, lambda i, j: (0,) * len(shape))
    return pl.pallas_call(
        _gate_kernel,
        grid=(b, nblk),
        in_specs=[
            pl.BlockSpec((SSM_GROUPS, SSM_GROUP_CH, n_rows, CHUNK), lambda i, j: (0, 0, i * nblk + j, 0)),
            pl.BlockSpec((1, ts, d), lambda i, j: (i, j, 0)),
            const((1, d)), const((1, d)), const((d, d)), const((1, d)),
        ],
        out_specs=pl.BlockSpec((1, ts, d), lambda i, j: (i, j, 0)),
        out_shape=jax.ShapeDtypeStruct((b, s, d), F32),
        compiler_params=_cparams(2),
        name="s5_gate",
    )(y4, x, g[None, :], d_skip.astype(F32)[None, :], w_gate.astype(BF16), b_gate.astype(F32)[None, :])


def _pick(n, pref):
    t = min(n, pref)
    while n % t:
        t //= 2
    return t


def kernel(x, norm_mix, norm_mlp, mlp_w1, mlp_w2, w_in, w_fnet, q_norm, k_norm, w_out, lam_re, lam_im, log_dt, b_re, b_im, c_re, c_im, d_skip, w_gate, b_gate, final_norm):
    b, s, d = x.shape
    assert d == D_MODEL and s % CHUNK == 0 and s % GRID_W == 0
    tm = _pick(s, 512)
    gmat = _fourier_fold(w_fnet[0])
    u, v, q, k, vv = _inproj(x, norm_mix[0], w_in[0], gmat, q_norm[0], k_norm[0], tm)
    fa = _seqdft(u, v)
    att = _attention(q, k, vv, _pick(s, 128))
    x, ut = _tail0(fa, att, x, w_out[0], norm_mlp[0], mlp_w1[0], mlp_w2[0], norm_mix[1], tm)
    yt = _ssm(ut, lam_re[0], lam_im[0], log_dt[0], b_re[0], b_im[0], c_re[0], c_im[0], b, s)
    x = _gate(yt, x, norm_mix[1], d_skip[0], w_gate[0], b_gate[0], _pick(s, 1024))
    return _mlp(x, norm_mlp[1], mlp_w1[1], mlp_w2[1], final_norm, tm, True)
```

```python
import functools
import math

import numpy as np
import jax
import jax.numpy as jnp
from jax import lax
from jax.experimental import pallas as pl
from jax.experimental.pallas import tpu as pltpu

F32 = jnp.float32
BF16 = jnp.bfloat16

EPS = 1e-6
D_MODEL = 1024
MIX_A = 512
FNET_HEAD_DIM = 64
FNET_HEADS = MIX_A // FNET_HEAD_DIM
HEAD_DIM = 64
N_Q_HEADS = 8
N_KV_HEADS = 2
GQA_GROUP = N_Q_HEADS // N_KV_HEADS
Q_WIDTH = N_Q_HEADS * HEAD_DIM
KV_WIDTH = N_KV_HEADS * HEAD_DIM
IN_WIDTH = MIX_A + Q_WIDTH + 2 * KV_WIDTH
GRID_W = 64
ROPE_THETA = 10000.0
ROPE_AXIS_DIM = HEAD_DIM // 2
SSM_GROUP_CH = 16
SSM_GROUPS = D_MODEL // SSM_GROUP_CH
SSM_STATE = 64
D_FF = 4 * D_MODEL
CHUNK = 128
LANES = 128
LOG2_E = 1.4426950408889634
V_ROWS = HEAD_DIM + 16
VMEM_LIMIT = 56 * 1024 * 1024


def _cparams(n_grid_dims):
    return pltpu.CompilerParams(
        dimension_semantics=("arbitrary",) * n_grid_dims, vmem_limit_bytes=VMEM_LIMIT)


def _dot(a, b):
    return jnp.dot(a, b, preferred_element_type=F32)


def _split_bf16(a):
    hi = a.astype(BF16)
    lo = (a - hi.astype(F32)).astype(BF16)
    return hi, lo


def _dot3(a, b):
    a_hi, a_lo = _split_bf16(a)
    b_hi, b_lo = _split_bf16(b)
    return _dot(a_hi, b_hi) + (_dot(a_hi, b_lo) + _dot(a_lo, b_hi))


def _rms_rows(x, g):
    ms = jnp.mean(x * x, axis=-1, keepdims=True)
    return x * lax.rsqrt(ms + EPS) * g


def _fold_kernel(cc_ref, sc_ref, w_ref, g_ref):
    w = w_ref[...]
    g_ref[:, :MIX_A] = _dot3(cc_ref[...], w).astype(BF16)
    g_ref[:, MIX_A:] = (-_dot3(sc_ref[...], w)).astype(BF16)


def _fourier_fold(w_fnet):
    n = np.arange(FNET_HEAD_DIM)
    ang = 2.0 * np.pi * np.outer(n, n) / FNET_HEAD_DIM
    scale = FNET_HEAD_DIM ** -0.5
    eye = np.eye(FNET_HEADS)
    cc = jnp.asarray(np.kron(eye, np.cos(ang) * scale), F32)
    sc = jnp.asarray(np.kron(eye, np.sin(ang) * scale), F32)
    eye_j = jnp.eye(FNET_HEADS, dtype=F32)
    w_bd = (eye_j[:, None, :, None] * w_fnet.astype(F32)[:, :, None, :]).reshape(MIX_A, MIX_A)
    return pl.pallas_call(
        _fold_kernel,
        out_shape=jax.ShapeDtypeStruct((MIX_A, 2 * MIX_A), BF16),
        name="fourier_fold",
    )(cc, sc, w_bd)


def _head_norm(t, ones, gain):
    hi, lo = _split_bf16(t * t)
    ss = _dot(hi, ones) + _dot(lo, ones)
    return t * lax.rsqrt(ss * (1.0 / HEAD_DIM) + EPS) * gain


def _rope(t, cos, sin):
    width = t.shape[1]
    reps = width // LANES
    c = jnp.concatenate([cos] * reps, axis=1) if reps > 1 else cos
    s = jnp.concatenate([sin] * reps, axis=1) if reps > 1 else sin
    lane = lax.broadcasted_iota(jnp.int32, t.shape, 1)
    first_half = (lane % ROPE_AXIS_DIM) < (ROPE_AXIS_DIM // 2)
    partner = jnp.where(first_half,
                        pltpu.roll(t, width - ROPE_AXIS_DIM // 2, 1),
                        pltpu.roll(t, ROPE_AXIS_DIM // 2, 1))
    return t * c + partner * s


def _inproj_kernel(x_ref, g_ref, win_ref, gmat_ref, ones_ref, qg_ref, kg_ref,
                   cq_ref, sq_ref, ck_ref, sk_ref,
                   u_ref, v_ref, q_ref, k_ref, vv_ref):
    h = _rms_rows(x_ref[0], g_ref[...]).astype(BF16)
    z = _dot(h, win_ref[...])
    uv = _dot(z[:, :MIX_A].astype(BF16), gmat_ref[...])
    u_ref[0] = uv[:, :MIX_A].astype(BF16)
    v_ref[0] = uv[:, MIX_A:].astype(BF16)
    q = z[:, MIX_A:MIX_A + Q_WIDTH]
    k = z[:, MIX_A + Q_WIDTH:MIX_A + Q_WIDTH + KV_WIDTH]
    v = z[:, MIX_A + Q_WIDTH + KV_WIDTH:]
    ones = ones_ref[...]
    q = _rope(_head_norm(q, ones, qg_ref[...]), cq_ref[...], sq_ref[...])
    k = _rope(_head_norm(k, ones[:KV_WIDTH, :KV_WIDTH], kg_ref[...]), ck_ref[...], sk_ref[...])
    tm = q.shape[0]
    q_ref[0] = q.T.reshape(N_Q_HEADS, HEAD_DIM, tm).astype(BF16)
    for hh in range(N_KV_HEADS):
        k_ref[0, hh] = k[:, hh * HEAD_DIM:(hh + 1) * HEAD_DIM].astype(BF16)
    vv_ref[0, :, :HEAD_DIM, :] = v.T.reshape(N_KV_HEADS, HEAD_DIM, tm).astype(BF16)
    vv_ref[0, :, HEAD_DIM:, :] = jnp.ones((N_KV_HEADS, V_ROWS - HEAD_DIM, tm), BF16)


def _rope_tables(seq_len):
    rows = seq_len // GRID_W
    row = jnp.broadcast_to(jnp.arange(rows, dtype=F32)[:, None], (rows, GRID_W)).reshape(seq_len)
    col = jnp.broadcast_to(jnp.arange(GRID_W, dtype=F32)[None, :], (rows, GRID_W)).reshape(seq_len)
    inv_freq = ROPE_THETA ** (-jnp.arange(0, ROPE_AXIS_DIM, 2, dtype=F32) / ROPE_AXIS_DIM)
    ar = row[:, None] * inv_freq
    ac = col[:, None] * inv_freq
    cos = jnp.concatenate([jnp.cos(ar), jnp.cos(ar), jnp.cos(ac), jnp.cos(ac)], axis=1)
    sin = jnp.concatenate([-jnp.sin(ar), jnp.sin(ar), -jnp.sin(ac), jnp.sin(ac)], axis=1)
    cos = jnp.concatenate([cos, cos], axis=1)
    sin = jnp.concatenate([sin, sin], axis=1)
    scale = HEAD_DIM ** -0.5 * LOG2_E
    return cos * scale, sin * scale, cos, sin


def _inproj(x, g, w_in, gmat, q_norm, k_norm, tm):
    b, s, d = x.shape
    cq, sq, ck, sk = _rope_tables(s)
    head = np.arange(Q_WIDTH) // HEAD_DIM
    ones = jnp.asarray(head[:, None] == head[None, :], BF16)
    qg = jnp.tile(q_norm.astype(F32), N_Q_HEADS)[None, :]
    kg = jnp.tile(k_norm.astype(F32), N_KV_HEADS)[None, :]
    const = lambda shape: pl.BlockSpec(shape, lambda i, j: (0,) * len(shape))
    tab = pl.BlockSpec((tm, LANES), lambda i, j: (j, 0))
    return pl.pallas_call(
        _inproj_kernel,
        grid=(b, s // tm),
        in_specs=[
            pl.BlockSpec((1, tm, d), lambda i, j: (i, j, 0)),
            const((1, d)), const((d, IN_WIDTH)), const((MIX_A, 2 * MIX_A)), const((Q_WIDTH, Q_WIDTH)),
            const((1, Q_WIDTH)), const((1, KV_WIDTH)), tab, tab, tab, tab,
        ],
        out_specs=[
            pl.BlockSpec((1, tm, MIX_A), lambda i, j: (i, j, 0)),
            pl.BlockSpec((1, tm, MIX_A), lambda i, j: (i, j, 0)),
            pl.BlockSpec((1, N_Q_HEADS, HEAD_DIM, tm), lambda i, j: (i, 0, 0, j)),
            pl.BlockSpec((1, N_KV_HEADS, tm, HEAD_DIM), lambda i, j: (i, 0, j, 0)),
            pl.BlockSpec((1, N_KV_HEADS, V_ROWS, tm), lambda i, j: (i, 0, 0, j)),
        ],
        out_shape=[
            jax.ShapeDtypeStruct((b, s, MIX_A), BF16),
            jax.ShapeDtypeStruct((b, s, MIX_A), BF16),
            jax.ShapeDtypeStruct((b, N_Q_HEADS, HEAD_DIM, s), BF16),
            jax.ShapeDtypeStruct((b, N_KV_HEADS, s, HEAD_DIM), BF16),
            jax.ShapeDtypeStruct((b, N_KV_HEADS, V_ROWS, s), BF16),
        ],
        compiler_params=_cparams(2),
        name="inproj",
    )(x, g[None, :], w_in.astype(BF16), gmat, ones, qg, kg, cq, sq, ck, sk)


def _dft_tables_kernel(bc_ref, bs_ref, rot_ref, c_ref, s_ref):
    re, im = bc_ref[...], bs_ref[...]
    c_ref[:, :LANES] = re.astype(BF16)
    s_ref[:, :LANES] = im.astype(BF16)
    w, stage = LANES, 0
    while w < c_ref.shape[1]:
        cr = rot_ref[:, 2 * stage:2 * stage + 1]
        sr = rot_ref[:, 2 * stage + 1:2 * stage + 2]
        nre, nim = _cmul(re, im, cr, sr)
        c_ref[:, w:2 * w] = nre.astype(BF16)
        s_ref[:, w:2 * w] = nim.astype(BF16)
        re = jnp.concatenate([re, nre], axis=1)
        im = jnp.concatenate([im, nim], axis=1)
        w, stage = 2 * w, stage + 1


def _dft_tables(seq, tr):
    h = seq // 2
    scale = seq ** -0.5
    j = jnp.arange(h, dtype=jnp.int32)[:, None]
    k = jnp.arange(LANES, dtype=jnp.int32)[None, :]
    unit = 2.0 * math.pi / seq
    ang = ((j * k) % seq).astype(F32) * unit
    n_stage = int(round(math.log2(h // LANES)))
    widths = jnp.asarray([LANES << st for st in range(n_stage)] + [0] * (4 - n_stage), jnp.int32)
    rang = ((j * widths[None, :]) % seq).astype(F32) * unit
    rot = jnp.stack([jnp.cos(rang), jnp.sin(rang)], axis=-1).reshape(h, 8)
    return pl.pallas_call(
        _dft_tables_kernel,
        grid=(h // tr,),
        in_specs=[pl.BlockSpec((tr, LANES), lambda i: (i, 0)),
                  pl.BlockSpec((tr, LANES), lambda i: (i, 0)),
                  pl.BlockSpec((tr, 8), lambda i: (i, 0))],
        out_specs=[pl.BlockSpec((tr, h), lambda i: (i, 0)), pl.BlockSpec((tr, h), lambda i: (i, 0))],
        out_shape=[jax.ShapeDtypeStruct((h, h), BF16), jax.ShapeDtypeStruct((h, h), BF16)],
        compiler_params=_cparams(1),
        name="dft_tables",
    )(jnp.cos(ang) * scale, jnp.sin(ang) * scale, rot)


def _seqdft_kernel(ct_ref, st_ref, u_ref, v_ref, o_ref, ue_ref, vo_ref, d_ref, *, tm, blk):
    t = pl.program_id(1)
    seq = u_ref.shape[1]
    h = seq // 2
    width = u_ref.shape[2]
    scale = seq ** -0.5
    rr = lax.broadcasted_iota(jnp.int32, (blk, blk), 0)
    cc = lax.broadcasted_iota(jnp.int32, (blk, blk), 1)
    flip = jnp.where(rr + cc == blk - 1, 1.0, 0.0).astype(BF16)
    row = lax.broadcasted_iota(jnp.int32, (h, width), 0)

    @pl.when(t == 0)
    def _fold_halves():
        for src_ref, dst_ref, sign in ((u_ref, ue_ref, 1.0), (v_ref, vo_ref, -1.0)):
            rev = jnp.concatenate(
                [_dot(flip, src_ref[0, seq - blk * (a + 1):seq - blk * a, :]) for a in range(h // blk)],
                axis=0)
            mirror = jnp.where(row == 0, 0.0, pltpu.roll(rev, 1, 0))
            dst_ref[...] = (src_ref[0, :h, :].astype(F32) + sign * mirror).astype(BF16)

    ue = ue_ref[...]
    x_mid = u_ref[0, h:h + 1, :].astype(F32) * scale
    p = _dot(ct_ref[...], ue)
    q = _dot(st_ref[...], vo_ref[...])
    j = t * tm + lax.broadcasted_iota(jnp.int32, (tm, 1), 0)
    p = p + (1 - 2 * (j & 1)).astype(F32) * x_mid
    off = pl.multiple_of(t * tm, tm)
    o_ref[0, pl.ds(off, tm), :] = (p + q).astype(BF16)
    d_ref[pl.ds(off, tm), :] = p - q

    @pl.when(t == pl.num_programs(1) - 1)
    def _mirror_half():
        kk = lax.broadcasted_iota(jnp.int32, (8, h), 1)
        alt = ((1 - 2 * (kk & 1)).astype(F32) * scale).astype(BF16)
        p_mid = _dot(alt, ue)[0:1, :] + x_mid
        d_up = jnp.where(row == h - 1, p_mid, pltpu.roll(d_ref[...], h - 1, 0)).astype(BF16)
        for a in range(h // blk):
            o_ref[0, h + blk * a:h + blk * (a + 1), :] = _dot(
                flip, d_up[h - blk * (a + 1):h - blk * a, :]).astype(BF16)


def _seqdft(u, v):
    b, s, w = u.shape
    h = s // 2
    tm = _pick(h, 512)
    blk = _pick(h, 256)
    ct, st = _dft_tables(s, _pick(h, 256))
    return pl.pallas_call(
        functools.partial(_seqdft_kernel, tm=tm, blk=blk),
        grid=(b, h // tm),
        in_specs=[
            pl.BlockSpec((tm, h), lambda i, j: (j, 0)),
            pl.BlockSpec((tm, h), lambda i, j: (j, 0)),
            pl.BlockSpec((1, s, w), lambda i, j: (i, 0, 0)),
            pl.BlockSpec((1, s, w), lambda i, j: (i, 0, 0)),
        ],
        out_specs=pl.BlockSpec((1, s, w), lambda i, j: (i, 0, 0)),
        out_shape=jax.ShapeDtypeStruct((b, s, w), BF16),
        scratch_shapes=[pltpu.VMEM((h, w), BF16), pltpu.VMEM((h, w), BF16), pltpu.VMEM((h, w), F32)],
        compiler_params=_cparams(2),
        name="seqdft",
    )(ct, st, u, v)


def _attn_kernel(qt_ref, k_ref, vt_ref, o_ref, sa_ref, sb_ref, ma_ref, mb_ref, *, tq):
    seq = k_ref.shape[2]
    nblk = seq // tq

    def scores(i, s_ref, m_ref):
        off = pl.multiple_of(i * tq, tq)
        qt = jnp.concatenate([qt_ref[0, g, :, pl.ds(off, tq)] for g in range(GQA_GROUP)], axis=1)
        s = _dot(k_ref[0, 0], qt)
        s_ref[...] = s
        m_ref[...] = jnp.max(s, axis=0, keepdims=True)

    def finish(i, s_ref, m_ref):
        p = jnp.exp2(s_ref[...] - m_ref[...]).astype(BF16)
        acc = _dot(vt_ref[0, 0], p)
        o = acc[:HEAD_DIM] * (1.0 / acc[HEAD_DIM:HEAD_DIM + 1])
        off = pl.multiple_of(i * tq, tq)
        for g in range(GQA_GROUP):
            o_ref[0, g * HEAD_DIM:(g + 1) * HEAD_DIM, pl.ds(off, tq)] = (
                o[:, g * tq:(g + 1) * tq].astype(BF16))

    scores(0, sa_ref, ma_ref)

    def pair(j, carry):
        scores(2 * j + 1, sb_ref, mb_ref)
        finish(2 * j, sa_ref, ma_ref)
        scores(2 * j + 2, sa_ref, ma_ref)
        finish(2 * j + 1, sb_ref, mb_ref)
        return carry

    lax.fori_loop(0, nblk // 2 - 1, pair, 0)
    scores(nblk - 1, sb_ref, mb_ref)
    finish(nblk - 2, sa_ref, ma_ref)
    finish(nblk - 1, sb_ref, mb_ref)


def _attention(qt, k, vt, tq):
    b, _, _, s = qt.shape
    assert (s // tq) % 2 == 0
    width = GQA_GROUP * tq
    return pl.pallas_call(
        functools.partial(_attn_kernel, tq=tq),
        grid=(b, N_KV_HEADS),
        in_specs=[
            pl.BlockSpec((1, GQA_GROUP, HEAD_DIM, s), lambda i, g: (i, g, 0, 0)),
            pl.BlockSpec((1, 1, s, HEAD_DIM), lambda i, g: (i, g, 0, 0)),
            pl.BlockSpec((1, 1, V_ROWS, s), lambda i, g: (i, g, 0, 0)),
        ],
        out_specs=pl.BlockSpec((1, GQA_GROUP * HEAD_DIM, s), lambda i, g: (i, g, 0)),
        out_shape=jax.ShapeDtypeStruct((b, Q_WIDTH, s), BF16),
        scratch_shapes=[pltpu.VMEM((s, width), F32), pltpu.VMEM((s, width), F32),
                        pltpu.VMEM((1, width), F32), pltpu.VMEM((1, width), F32)],
        compiler_params=_cparams(2),
        name="gqa_attention",
    )(qt, k, vt)


def _mlp_block(x, g, w1_ref, w2_ref):
    h = _rms_rows(x, g).astype(BF16)
    a = jnp.maximum(_dot(h, w1_ref[...]), 0.0)
    return x + _dot((a * a).astype(BF16), w2_ref[...])


def _tail0_kernel(fa_ref, att_ref, x_ref, wa_ref, wb_ref, g_ref, w1_ref, w2_ref, gn_ref,
                  o_ref, ut_ref):
    att = lax.dot_general(att_ref[0], wb_ref[...], (((0,), (0,)), ((), ())),
                          preferred_element_type=F32)
    x1 = x_ref[0] + (_dot(fa_ref[0], wa_ref[...]) + att)
    x2 = _mlp_block(x1, g_ref[...], w1_ref, w2_ref)
    o_ref[0] = x2
    ut_ref[...] = _rms_rows(x2, gn_ref[...]).T.astype(BF16)


def _tail0(fa, att, x, w_out, g_mlp, w1, w2, g_next, tm):
    b, s, d = x.shape
    nblk = s // tm
    w = w_out.astype(BF16)
    const = lambda shape: pl.BlockSpec(shape, lambda i, j: (0,) * len(shape),
                                       pipeline_mode=pl.Buffered(1))
    return pl.pallas_call(
        _tail0_kernel,
        grid=(b, nblk),
        in_specs=[
            pl.BlockSpec((1, tm, MIX_A), lambda i, j: (i, j, 0)),
            pl.BlockSpec((1, Q_WIDTH, tm), lambda i, j: (i, 0, j)),
            pl.BlockSpec((1, tm, d), lambda i, j: (i, j, 0)),
            const((MIX_A, d)), const((Q_WIDTH, d)),
            const((1, d)), const((d, D_FF)), const((D_FF, d)), const((1, d)),
        ],
        out_specs=[pl.BlockSpec((1, tm, d), lambda i, j: (i, j, 0)),
                   pl.BlockSpec((d, tm), lambda i, j: (0, i * nblk + j))],
        out_shape=[jax.ShapeDtypeStruct((b, s, d), F32),
                   jax.ShapeDtypeStruct((d, b * s), BF16)],
        compiler_params=_cparams(2),
        name="outproj_mlp",
    )(fa, att, x, w[:MIX_A], w[MIX_A:], g_mlp[None, :], w1.astype(BF16), w2.astype(BF16),
      g_next[None, :])


def _mlp_kernel(x_ref, g_ref, w1_ref, w2_ref, gf_ref, o_ref, *, final_norm):
    y = _mlp_block(x_ref[0], g_ref[...], w1_ref, w2_ref)
    if final_norm:
        y = _rms_rows(y, gf_ref[...])
    o_ref[0] = y


def _mlp(x, g, w1, w2, g_final, tm, final_norm):
    b, s, d = x.shape
    const = lambda shape: pl.BlockSpec(shape, lambda i, j: (0,) * len(shape),
                                       pipeline_mode=pl.Buffered(1))
    return pl.pallas_call(
        functools.partial(_mlp_kernel, final_norm=final_norm),
        grid=(b, s // tm),
        in_specs=[
            pl.BlockSpec((1, tm, d), lambda i, j: (i, j, 0)),
            const((1, d)), const((d, D_FF)), const((D_FF, d)), const((1, d)),
        ],
        out_specs=pl.BlockSpec((1, tm, d), lambda i, j: (i, j, 0)),
        out_shape=jax.ShapeDtypeStruct((b, s, d), F32),
        compiler_params=_cparams(2),
        name="mlp_final" if final_norm else "mlp",
    )(x, g[None, :], w1.astype(BF16), w2.astype(BF16), g_final[None, :])


def _cmul(ar, ai, br, bi):
    return ar * br - ai * bi, ar * bi + ai * br


def _cpow(a, theta, e):
    mag = jnp.exp(e * a)
    ang = e * theta
    return mag * jnp.cos(ang), mag * jnp.sin(ang)


def _ssm_kernel(ut_ref, prow_ref, crow_ref, ccol_ref, brow_ref, shift_ref, y_ref,
                wall_ref, mmat_ref, pmat_ref, cmat_ref, yacc_ref, *, n_chunks):
    P = SSM_STATE
    C = SSM_GROUP_CH
    L = CHUNK
    rows = ut_ref.shape[2]

    prow = prow_ref[0]
    lre, lim = prow[0:1, :], prow[1:2, :]
    dt = jnp.exp(prow[2:3, :])
    a_row, th_row = lre * dt, lim * dt
    lbr, lbi = _cpow(a_row, th_row, 1.0)
    inv = 1.0 / (lre * lre + lim * lim)
    coef_r, coef_i = _cmul(lbr - 1.0, lbi, lre * inv, -lim * inv)
    bbr, bbi = _cmul(coef_r, coef_i, brow_ref[0, 0], brow_ref[0, 1])

    sq = [(lbr, lbi)]
    while len(sq) <= 7:
        sq.append(_cmul(*sq[-1], *sq[-1]))
    s8 = lax.broadcasted_iota(jnp.int32, (8, 2 * P), 0).astype(F32)
    up = _cpow(a_row, th_row, s8)
    dn = _cpow(a_row, th_row, 7.0 - s8)
    for k in range(3, 7):
        hi_up = _cmul(*up, *sq[k])
        hi_dn = _cmul(*dn, *sq[k])
        up = tuple(jnp.concatenate([lo, hi], axis=0) for lo, hi in zip(up, hi_up))
        dn = tuple(jnp.concatenate([hi, lo], axis=0) for lo, hi in zip(dn, hi_dn))
    up1 = _cmul(*up, lbr, lbi)
    dn1 = _cmul(*dn, lbr, lbi)
    fwd_lane = lax.broadcasted_iota(jnp.int32, (L, 2 * P), 1) < P
    row0 = lax.broadcasted_iota(jnp.int32, (L, 2 * P), 0) == 0

    pwr, pwi = jnp.where(fwd_lane, dn[0], up[0]), jnp.where(fwd_lane, dn[1], up[1])
    for c in range(C):
        xr, xi = _cmul(pwr, pwi, bbr[c:c + 1, :], bbi[c:c + 1, :])
        pmat_ref[c * L:(c + 1) * L, :2 * P] = xr.astype(BF16)
        pmat_ref[c * L:(c + 1) * L, 2 * P:] = xi.astype(BF16)

    qwr = jnp.where(fwd_lane, up1[0], dn1[0]).T
    qwi = jnp.where(fwd_lane, up1[1], dn1[1]).T
    ccr, cci = ccol_ref[0, 0], ccol_ref[0, 1]
    for c in range(C):
        xr, xi = _cmul(qwr, qwi, ccr[:, c:c + 1], cci[:, c:c + 1])
        cmat_ref[:2 * P, c * L:(c + 1) * L] = xr.astype(BF16)
        cmat_ref[2 * P:, c * L:(c + 1) * L] = (-xi).astype(BF16)

    zero = jnp.zeros((L, 2 * P), F32)
    k_lo = (jnp.where(fwd_lane, up[0], jnp.where(row0, 1.0, 0.0)), jnp.where(fwd_lane, up[1], zero))
    k_hi = (jnp.where(fwd_lane, zero, dn1[0]), jnp.where(fwd_lane, zero, dn1[1]))
    kwr = jnp.concatenate([k_lo[0].T, k_hi[0].T], axis=1)
    kwi = jnp.concatenate([k_lo[1].T, k_hi[1].T], axis=1)
    crr, cri = crow_ref[0, 0], crow_ref[0, 1]
    cb_r, cb_i = [], []
    for c in range(C):
        xr, xi = _cmul(crr, cri, bbr[c:c + 1, :], bbi[c:c + 1, :])
        cb_r.append(xr)
        cb_i.append(xi)
    cb_r = jnp.concatenate(cb_r, axis=0)
    cb_i = jnp.concatenate(cb_i, axis=0)
    ktab = _dot3(cb_r, kwr) - _dot3(cb_i, kwi)

    def fine(blk):
        return pltpu.roll(jnp.broadcast_to(ktab[blk:blk + 1, :], (8, 2 * L)), 0, 1,
                          stride=1, stride_axis=0)
    for blk in range(0, C * C, 2):
        wall_ref[blk * 8:blk * 8 + 16, :] = jnp.concatenate(
            [fine(blk), fine(blk + 1)], axis=0).astype(BF16)

    def coarse(hq, carry):
        sh = _dot(wall_ref[...], shift_ref[hq])
        for q2 in range(2):
            lo = 2 * q2 * L
            pair = jnp.concatenate([sh[:, lo:lo + L].reshape(C * C, 8, L),
                                    sh[:, lo + L:lo + 2 * L].reshape(C * C, 8, L)],
                                   axis=1).astype(BF16)
            for cp in range(C):
                r0 = pl.multiple_of(cp * L + 16 * (2 * hq + q2), 16)
                mmat_ref[pl.ds(r0, 16), :] = jnp.concatenate(
                    [pair[cp * C + c] for c in range(C)], axis=1)
        return carry
    lax.fori_loop(0, L // 32, coarse, 0)

    half_w = C * L // 2

    def intra(hf, carry):
        u_in = jnp.concatenate([ut_ref[0, c] for c in range(C)], axis=1)
        off = pl.multiple_of(hf * half_w, half_w)
        yacc_ref[:, pl.ds(off, half_w)] = _dot(u_in, mmat_ref[:, pl.ds(off, half_w)])
        return carry
    lax.fori_loop(0, 2, intra, 0)

    u = jnp.concatenate([ut_ref[0, c] for c in range(C)], axis=1)
    y = yacc_ref[...]
    xloc = _dot(u, pmat_ref[...])
    xr, xi = xloc[:, :2 * P], xloc[:, 2 * P:]

    k_idx = lax.broadcasted_iota(jnp.int32, (rows, 2 * P), 0) % n_chunks
    is_f = lax.broadcasted_iota(jnp.int32, (rows, 2 * P), 1) < P
    ar, ai = sq[7]

    def shifted(t, step):
        down = pltpu.roll(t, step, 0)
        up = pltpu.roll(t, rows - step, 0)
        return jnp.where(is_f, jnp.where(k_idx >= step, down, 0.0),
                         jnp.where(k_idx < n_chunks - step, up, 0.0))

    step = 1
    while step < n_chunks:
        sr, si = shifted(xr, step), shifted(xi, step)
        pr, pi = _cmul(sr, si, ar, ai)
        xr, xi = xr + pr, xi + pi
        ar, ai = _cmul(ar, ai, ar, ai)
        step *= 2
    xprev = jnp.concatenate([shifted(xr, 1), shifted(xi, 1)], axis=1).astype(BF16)
    y = y + _dot(xprev, cmat_ref[...])
    for c in range(C):
        y_ref[0, c] = y[:, c * L:(c + 1) * L]


def _shift_mats():
    L = CHUNK
    m = np.arange(2 * L)[:, None]
    col = np.arange(4 * L)[None, :]
    q, i = col // L, col % L
    mats = [m == (i - 8 * (4 * hq + q)) % (2 * L) for hq in range(L // 32)]
    return jnp.asarray(np.stack(mats), BF16)


def _ssm(ut, lam_re, lam_im, log_dt, b_re, b_im, c_re, c_im, batch, seq):
    G, C, P, L = SSM_GROUPS, SSM_GROUP_CH, SSM_STATE, CHUNK
    n_chunks = seq // L
    rows = batch * n_chunks
    ut4 = ut.reshape(G, C, rows, L)
    f32 = lambda t: t.astype(F32)
    cat_p = lambda t: jnp.concatenate([f32(t[0]), f32(t[1])], axis=-1)
    ldt = jnp.broadcast_to(f32(log_dt)[:, :, None], (2, G, P))
    fields = jnp.stack([cat_p(lam_re), cat_p(lam_im), cat_p(ldt)], axis=1)
    prow = jnp.concatenate([fields, jnp.zeros((G, 5, 2 * P), F32)], axis=1)
    crow = jnp.stack([jnp.concatenate([f32(c_re[0]), f32(c_re[1])], axis=-1),
                      jnp.concatenate([f32(c_im[0]), f32(c_im[1])], axis=-1)], axis=1)
    ccol = jnp.swapaxes(crow, 2, 3)
    brow = jnp.stack([jnp.concatenate([f32(b_re[0]), f32(b_re[1])], axis=1),
                      jnp.concatenate([f32(b_im[0]), f32(b_im[1])], axis=1)], axis=1)
    brow = jnp.swapaxes(brow, 2, 3)
    return pl.pallas_call(
        functools.partial(_ssm_kernel, n_chunks=n_chunks),
        grid=(G,),
        in_specs=[
            pl.BlockSpec((1, C, rows, L), lambda g: (g, 0, 0, 0)),
            pl.BlockSpec((1, 8, 2 * P), lambda g: (g, 0, 0)),
            pl.BlockSpec((1, 2, C, 2 * P), lambda g: (g, 0, 0, 0)),
            pl.BlockSpec((1, 2, 2 * P, C), lambda g: (g, 0, 0, 0)),
            pl.BlockSpec((1, 2, C, 2 * P), lambda g: (g, 0, 0, 0)),
            pl.BlockSpec((L // 32, 2 * L, 4 * L), lambda g: (0, 0, 0), pipeline_mode=pl.Buffered(1)),
        ],
        out_specs=pl.BlockSpec((1, C, rows, L), lambda g: (g, 0, 0, 0)),
        out_shape=jax.ShapeDtypeStruct((G, C, rows, L), F32),
        scratch_shapes=[
            pltpu.VMEM((C * C * 8, 2 * L), BF16),
            pltpu.VMEM((C * L, C * L), BF16),
            pltpu.VMEM((C * L, 4 * P), BF16),
            pltpu.VMEM((4 * P, C * L), BF16),
            pltpu.VMEM((rows, C * L), F32),
        ],
        compiler_params=_cparams(1),
        name="s5_scan",
    )(ut4, prow, crow, ccol, brow, _shift_mats())


def _gelu_tanh(y):
    return 0.5 * y * (1.0 + jnp.tanh(math.sqrt(2.0 / math.pi) * (y + 0.044715 * (y * y * y))))


def _gate_kernel(yt_ref, x_ref, g_ref, dsk_ref, wg_ref, bg_ref, o_ref):
    x = x_ref[0]
    u = _rms_rows(x, g_ref[...])
    n_rows = yt_ref.shape[2]
    y_rct = jnp.swapaxes(yt_ref[...].reshape(D_MODEL, n_rows, CHUNK), 0, 1)
    y = jnp.concatenate([y_rct[r].T for r in range(n_rows)], axis=0) + dsk_ref[...] * u
    gl = _gelu_tanh(y)
    gate = _dot(gl.astype(BF16), wg_ref[...]) + bg_ref[...]
    o_ref[0] = x + gl * (0.5 + 0.5 * jnp.tanh(0.5 * gate))


def _gate(y4, x, g, d_skip, w_gate, b_gate, ts):
    b, s, d = x.shape
    nblk = s // ts
    n_rows = ts // CHUNK
    const = lambda shape: pl.BlockSpec(shape, lambda i, j: (0,) * len(shape))
    return pl.pallas_call(
        _gate_kernel,
        grid=(b, nblk),
        in_specs=[
            pl.BlockSpec((SSM_GROUPS, SSM_GROUP_CH, n_rows, CHUNK), lambda i, j: (0, 0, i * nblk + j, 0)),
            pl.BlockSpec((1, ts, d), lambda i, j: (i, j, 0)),
            const((1, d)), const((1, d)), const((d, d)), const((1, d)),
        ],
        out_specs=pl.BlockSpec((1, ts, d), lambda i, j: (i, j, 0)),
        out_shape=jax.ShapeDtypeStruct((b, s, d), F32),
        compiler_params=_cparams(2),
        name="s5_gate",
    )(y4, x, g[None, :], d_skip.astype(F32)[None, :], w_gate.astype(BF16), b_gate.astype(F32)[None, :])


def _pick(n, pref):
    t = min(n, pref)
    while n % t:
        t //= 2
    return t


def kernel(x, norm_mix, norm_mlp, mlp_w1, mlp_w2, w_in, w_fnet, q_norm, k_norm, w_out, lam_re, lam_im, log_dt, b_re, b_im, c_re, c_im, d_skip, w_gate, b_gate, final_norm):
    b, s, d = x.shape
    assert d == D_MODEL and s % CHUNK == 0 and s % GRID_W == 0
    tm = _pick(s, 512)
    gmat = _fourier_fold(w_fnet[0])
    u, v, q, k, vv = _inproj(x, norm_mix[0], w_in[0], gmat, q_norm[0], k_norm[0], tm)
    fa = _seqdft(u, v)
    att = _attention(q, k, vv, _pick(s, 128))
    x, ut = _tail0(fa, att, x, w_out[0], norm_mlp[0], mlp_w1[0], mlp_w2[0], norm_mix[1], tm)
    yt = _ssm(ut, lam_re[0], lam_im[0], log_dt[0], b_re[0], b_im[0], c_re[0], c_im[0], b, s)
    x = _gate(yt, x, norm_mix[1], d_skip[0], w_gate[0], b_gate[0], _pick(s, 1024))
    return _mlp(x, norm_mlp[1], mlp_w1[1], mlp_w2[1], final_norm, tm, True)
```

```python
import functools
import math

import numpy as np
import jax
import jax.numpy as jnp
from jax import lax
from jax.experimental import pallas as pl
from jax.experimental.pallas import tpu as pltpu

F32 = jnp.float32
BF16 = jnp.bfloat16

EPS = 1e-6
D_MODEL = 1024
MIX_A = 512
FNET_HEAD_DIM = 64
FNET_HEADS = MIX_A // FNET_HEAD_DIM
HEAD_DIM = 64
N_Q_HEADS = 8
N_KV_HEADS = 2
GQA_GROUP = N_Q_HEADS // N_KV_HEADS
Q_WIDTH = N_Q_HEADS * HEAD_DIM
KV_WIDTH = N_KV_HEADS * HEAD_DIM
IN_WIDTH = MIX_A + Q_WIDTH + 2 * KV_WIDTH
GRID_W = 64
ROPE_THETA = 10000.0
ROPE_AXIS_DIM = HEAD_DIM // 2
SSM_GROUP_CH = 16
SSM_GROUPS = D_MODEL // SSM_GROUP_CH
SSM_STATE = 64
D_FF = 4 * D_MODEL
CHUNK = 128
LANES = 128
LOG2_E = 1.4426950408889634
V_ROWS = HEAD_DIM + 16
VMEM_LIMIT = 56 * 1024 * 1024


def _cparams(n_grid_dims):
    return pltpu.CompilerParams(
        dimension_semantics=("arbitrary",) * n_grid_dims, vmem_limit_bytes=VMEM_LIMIT)


def _dot(a, b):
    return jnp.dot(a, b, preferred_element_type=F32)


def _split_bf16(a):
    hi = a.astype(BF16)
    lo = (a - hi.astype(F32)).astype(BF16)
    return hi, lo


def _dot3(a, b):
    a_hi, a_lo = _split_bf16(a)
    b_hi, b_lo = _split_bf16(b)
    return _dot(a_hi, b_hi) + (_dot(a_hi, b_lo) + _dot(a_lo, b_hi))


def _rms_rows(x, g):
    ms = jnp.mean(x * x, axis=-1, keepdims=True)
    return x * lax.rsqrt(ms + EPS) * g


def _fold_kernel(cc_ref, sc_ref, w_ref, g_ref):
    w = w_ref[...]
    g_ref[:, :MIX_A] = _dot3(cc_ref[...], w).astype(BF16)
    g_ref[:, MIX_A:] = (-_dot3(sc_ref[...], w)).astype(BF16)


def _fourier_fold(w_fnet):
    n = np.arange(FNET_HEAD_DIM)
    ang = 2.0 * np.pi * np.outer(n, n) / FNET_HEAD_DIM
    scale = FNET_HEAD_DIM ** -0.5
    eye = np.eye(FNET_HEADS)
    cc = jnp.asarray(np.kron(eye, np.cos(ang) * scale), F32)
    sc = jnp.asarray(np.kron(eye, np.sin(ang) * scale), F32)
    eye_j = jnp.eye(FNET_HEADS, dtype=F32)
    w_bd = (eye_j[:, None, :, None] * w_fnet.astype(F32)[:, :, None, :]).reshape(MIX_A, MIX_A)
    return pl.pallas_call(
        _fold_kernel,
        out_shape=jax.ShapeDtypeStruct((MIX_A, 2 * MIX_A), BF16),
        name="fourier_fold",
    )(cc, sc, w_bd)


INPROJ_SPLIT = 2


def _head_norm(t, ones, gain):
    hi, lo = _split_bf16(t * t)
    ss = _dot(hi, ones) + _dot(lo, ones)
    return t * lax.rsqrt(ss * (1.0 / HEAD_DIM) + EPS) * gain


def _rope(t, cos, sin):
    width = t.shape[1]
    reps = width // LANES
    c = jnp.concatenate([cos] * reps, axis=1) if reps > 1 else cos
    s = jnp.concatenate([sin] * reps, axis=1) if reps > 1 else sin
    lane = lax.broadcasted_iota(jnp.int32, t.shape, 1)
    first_half = (lane % ROPE_AXIS_DIM) < (ROPE_AXIS_DIM // 2)
    partner = jnp.where(first_half,
                        pltpu.roll(t, width - ROPE_AXIS_DIM // 2, 1),
                        pltpu.roll(t, ROPE_AXIS_DIM // 2, 1))
    return t * c + partner * s


def _inproj_kernel(x_ref, g_ref, win_ref, gmat_ref, ones_ref, qg_ref, kg_ref,
                   cq_ref, sq_ref, ck_ref, sk_ref,
                   u_ref, v_ref, q_ref, k_ref, vv_ref):
    tm = x_ref.shape[1]
    ones = ones_ref[...]
    sub = tm // INPROJ_SPLIT
    for r in range(INPROJ_SPLIT):
        rs = slice(r * sub, (r + 1) * sub)
        h = _rms_rows(x_ref[0, rs, :], g_ref[...]).astype(BF16)
        z = _dot(h, win_ref[...])
        uv = _dot(z[:, :MIX_A].astype(BF16), gmat_ref[...])
        u_ref[0, rs, :] = uv[:, :MIX_A].astype(BF16)
        v_ref[0, rs, :] = uv[:, MIX_A:].astype(BF16)
        q = z[:, MIX_A:MIX_A + Q_WIDTH]
        k = z[:, MIX_A + Q_WIDTH:MIX_A + Q_WIDTH + KV_WIDTH]
        v = z[:, MIX_A + Q_WIDTH + KV_WIDTH:]
        q = _rope(_head_norm(q, ones, qg_ref[...]), cq_ref[rs, :], sq_ref[rs, :])
        k = _rope(_head_norm(k, ones[:KV_WIDTH, :KV_WIDTH], kg_ref[...]), ck_ref[rs, :], sk_ref[rs, :])
        q_ref[0, :, :, rs] = q.T.reshape(N_Q_HEADS, HEAD_DIM, sub).astype(BF16)
        for hh in range(N_KV_HEADS):
            k_ref[0, hh, rs, :] = k[:, hh * HEAD_DIM:(hh + 1) * HEAD_DIM].astype(BF16)
        vv_ref[0, :, :HEAD_DIM, rs] = v.T.reshape(N_KV_HEADS, HEAD_DIM, sub).astype(BF16)
    vv_ref[0, :, HEAD_DIM:, :] = jnp.ones((N_KV_HEADS, V_ROWS - HEAD_DIM, tm), BF16)


def _rope_tables(seq_len):
    t = np.arange(seq_len)
    inv_freq = ROPE_THETA ** (-np.arange(0, ROPE_AXIS_DIM, 2, dtype=np.float64) / ROPE_AXIS_DIM)
    ar = (t // GRID_W)[:, None] * inv_freq
    ac = (t % GRID_W)[:, None] * inv_freq
    cos = np.concatenate([np.cos(ar), np.cos(ar), np.cos(ac), np.cos(ac)], axis=1)
    sin = np.concatenate([-np.sin(ar), np.sin(ar), -np.sin(ac), np.sin(ac)], axis=1)
    cos = np.concatenate([cos, cos], axis=1)
    sin = np.concatenate([sin, sin], axis=1)
    scale = HEAD_DIM ** -0.5 * LOG2_E
    return tuple(jnp.asarray(a, F32) for a in (cos * scale, sin * scale, cos, sin))


def _inproj(x, g, w_in, gmat, q_norm, k_norm, tm):
    b, s, d = x.shape
    cq, sq, ck, sk = _rope_tables(s)
    head = np.arange(Q_WIDTH) // HEAD_DIM
    ones = jnp.asarray(head[:, None] == head[None, :], BF16)
    qg = jnp.tile(q_norm.astype(F32), N_Q_HEADS)[None, :]
    kg = jnp.tile(k_norm.astype(F32), N_KV_HEADS)[None, :]
    const = lambda shape: pl.BlockSpec(shape, lambda i, j: (0,) * len(shape))
    tab = pl.BlockSpec((tm, LANES), lambda i, j: (j, 0))
    return pl.pallas_call(
        _inproj_kernel,
        grid=(b, s // tm),
        in_specs=[
            pl.BlockSpec((1, tm, d), lambda i, j: (i, j, 0)),
            const((1, d)), const((d, IN_WIDTH)), const((MIX_A, 2 * MIX_A)), const((Q_WIDTH, Q_WIDTH)),
            const((1, Q_WIDTH)), const((1, KV_WIDTH)), tab, tab, tab, tab,
        ],
        out_specs=[
            pl.BlockSpec((1, tm, MIX_A), lambda i, j: (i, j, 0)),
            pl.BlockSpec((1, tm, MIX_A), lambda i, j: (i, j, 0)),
            pl.BlockSpec((1, N_Q_HEADS, HEAD_DIM, tm), lambda i, j: (i, 0, 0, j)),
            pl.BlockSpec((1, N_KV_HEADS, tm, HEAD_DIM), lambda i, j: (i, 0, j, 0)),
            pl.BlockSpec((1, N_KV_HEADS, V_ROWS, tm), lambda i, j: (i, 0, 0, j)),
        ],
        out_shape=[
            jax.ShapeDtypeStruct((b, s, MIX_A), BF16),
            jax.ShapeDtypeStruct((b, s, MIX_A), BF16),
            jax.ShapeDtypeStruct((b, N_Q_HEADS, HEAD_DIM, s), BF16),
            jax.ShapeDtypeStruct((b, N_KV_HEADS, s, HEAD_DIM), BF16),
            jax.ShapeDtypeStruct((b, N_KV_HEADS, V_ROWS, s), BF16),
        ],
        compiler_params=_cparams(2),
        name="inproj",
    )(x, g[None, :], w_in.astype(BF16), gmat, ones, qg, kg, cq, sq, ck, sk)


def _dft_tables_kernel(bc_ref, bs_ref, rot_ref, c_ref, s_ref):
    re, im = bc_ref[...], bs_ref[...]
    c_ref[:, :LANES] = re.astype(BF16)
    s_ref[:, :LANES] = im.astype(BF16)
    w, stage = LANES, 0
    while w < c_ref.shape[1]:
        cr = rot_ref[:, 2 * stage:2 * stage + 1]
        sr = rot_ref[:, 2 * stage + 1:2 * stage + 2]
        nre, nim = _cmul(re, im, cr, sr)
        c_ref[:, w:2 * w] = nre.astype(BF16)
        s_ref[:, w:2 * w] = nim.astype(BF16)
        re = jnp.concatenate([re, nre], axis=1)
        im = jnp.concatenate([im, nim], axis=1)
        w, stage = 2 * w, stage + 1


def _dft_tables(seq, tr):
    h = seq // 2
    scale = seq ** -0.5
    j = np.arange(h)[:, None]
    k = np.arange(LANES)[None, :]
    unit = 2.0 * math.pi / seq
    ang = ((j * k) % seq) * unit
    n_stage = int(round(math.log2(h // LANES)))
    widths = np.asarray([LANES << st for st in range(n_stage)] + [0] * (4 - n_stage))
    rang = ((j * widths[None, :]) % seq) * unit
    rot = jnp.asarray(np.stack([np.cos(rang), np.sin(rang)], axis=-1).reshape(h, 8), F32)
    return pl.pallas_call(
        _dft_tables_kernel,
        grid=(h // tr,),
        in_specs=[pl.BlockSpec((tr, LANES), lambda i: (i, 0)),
                  pl.BlockSpec((tr, LANES), lambda i: (i, 0)),
                  pl.BlockSpec((tr, 8), lambda i: (i, 0))],
        out_specs=[pl.BlockSpec((tr, h), lambda i: (i, 0)), pl.BlockSpec((tr, h), lambda i: (i, 0))],
        out_shape=[jax.ShapeDtypeStruct((h, h), BF16), jax.ShapeDtypeStruct((h, h), BF16)],
        compiler_params=_cparams(1),
        name="dft_tables",
    )(jnp.asarray(np.cos(ang) * scale, F32), jnp.asarray(np.sin(ang) * scale, F32), rot)


def _seqdft_kernel(ct_ref, st_ref, u_ref, v_ref, o_ref, ue_ref, vo_ref, d_ref, *, tm, blk):
    t = pl.program_id(1)
    seq = u_ref.shape[1]
    h = seq // 2
    width = u_ref.shape[2]
    scale = seq ** -0.5
    rr = lax.broadcasted_iota(jnp.int32, (blk, blk), 0)
    cc = lax.broadcasted_iota(jnp.int32, (blk, blk), 1)
    flip = jnp.where(rr + cc == blk - 1, 1.0, 0.0).astype(BF16)
    row = lax.broadcasted_iota(jnp.int32, (h, width), 0)

    @pl.when(t == 0)
    def _fold_halves():
        for src_ref, dst_ref, sign in ((u_ref, ue_ref, 1.0), (v_ref, vo_ref, -1.0)):
            rev = jnp.concatenate(
                [_dot(flip, src_ref[0, seq - blk * (a + 1):seq - blk * a, :]) for a in range(h // blk)],
                axis=0)
            mirror = jnp.where(row == 0, 0.0, pltpu.roll(rev, 1, 0))
            dst_ref[...] = (src_ref[0, :h, :].astype(F32) + sign * mirror).astype(BF16)

    ue = ue_ref[...]
    x_mid = u_ref[0, h:h + 1, :].astype(F32) * scale
    p = _dot(ct_ref[...], ue)
    q = _dot(st_ref[...], vo_ref[...])
    j = t * tm + lax.broadcasted_iota(jnp.int32, (tm, 1), 0)
    p = p + (1 - 2 * (j & 1)).astype(F32) * x_mid
    off = pl.multiple_of(t * tm, tm)
    o_ref[0, pl.ds(off, tm), :] = (p + q).astype(BF16)
    d_ref[pl.ds(off, tm), :] = p - q

    @pl.when(t == pl.num_programs(1) - 1)
    def _mirror_half():
        kk = lax.broadcasted_iota(jnp.int32, (8, h), 1)
        alt = ((1 - 2 * (kk & 1)).astype(F32) * scale).astype(BF16)
        p_mid = _dot(alt, ue)[0:1, :] + x_mid
        d_up = jnp.where(row == h - 1, p_mid, pltpu.roll(d_ref[...], h - 1, 0)).astype(BF16)
        for a in range(h // blk):
            o_ref[0, h + blk * a:h + blk * (a + 1), :] = _dot(
                flip, d_up[h - blk * (a + 1):h - blk * a, :]).astype(BF16)


def _seqdft(u, v):
    b, s, w = u.shape
    h = s // 2
    tm = _pick(h, 512)
    blk = _pick(h, 256)
    ct, st = _dft_tables(s, _pick(h, 256))
    return pl.pallas_call(
        functools.partial(_seqdft_kernel, tm=tm, blk=blk),
        grid=(b, h // tm),
        in_specs=[
            pl.BlockSpec((tm, h), lambda i, j: (j, 0)),
            pl.BlockSpec((tm, h), lambda i, j: (j, 0)),
            pl.BlockSpec((1, s, w), lambda i, j: (i, 0, 0)),
            pl.BlockSpec((1, s, w), lambda i, j: (i, 0, 0)),
        ],
        out_specs=pl.BlockSpec((1, s, w), lambda i, j: (i, 0, 0)),
        out_shape=jax.ShapeDtypeStruct((b, s, w), BF16),
        scratch_shapes=[pltpu.VMEM((h, w), BF16), pltpu.VMEM((h, w), BF16), pltpu.VMEM((h, w), F32)],
        compiler_params=_cparams(2),
        name="seqdft",
    )(ct, st, u, v)


def _attn_kernel(qt_ref, k_ref, vt_ref, o_ref, sa_ref, sb_ref, ma_ref, mb_ref, *, tq):
    seq = k_ref.shape[2]
    nblk = seq // tq

    def scores(i, s_ref, m_ref):
        off = pl.multiple_of(i * tq, tq)
        qt = jnp.concatenate([qt_ref[0, g, :, pl.ds(off, tq)] for g in range(GQA_GROUP)], axis=1)
        s = _dot(k_ref[0, 0], qt)
        s_ref[...] = s
        m_ref[...] = jnp.max(s, axis=0, keepdims=True)

    def finish(i, s_ref, m_ref):
        p = jnp.exp2(s_ref[...] - m_ref[...]).astype(BF16)
        acc = _dot(vt_ref[0, 0], p)
        o = acc[:HEAD_DIM] * (1.0 / acc[HEAD_DIM:HEAD_DIM + 1])
        off = pl.multiple_of(i * tq, tq)
        for g in range(GQA_GROUP):
            o_ref[0, g * HEAD_DIM:(g + 1) * HEAD_DIM, pl.ds(off, tq)] = (
                o[:, g * tq:(g + 1) * tq].astype(BF16))

    scores(0, sa_ref, ma_ref)

    def pair(j, carry):
        scores(2 * j + 1, sb_ref, mb_ref)
        finish(2 * j, sa_ref, ma_ref)
        scores(2 * j + 2, sa_ref, ma_ref)
        finish(2 * j + 1, sb_ref, mb_ref)
        return carry

    lax.fori_loop(0, nblk // 2 - 1, pair, 0)
    scores(nblk - 1, sb_ref, mb_ref)
    finish(nblk - 2, sa_ref, ma_ref)
    finish(nblk - 1, sb_ref, mb_ref)


def _attention(qt, k, vt, tq):
    b, _, _, s = qt.shape
    assert (s // tq) % 2 == 0
    width = GQA_GROUP * tq
    return pl.pallas_call(
        functools.partial(_attn_kernel, tq=tq),
        grid=(b, N_KV_HEADS),
        in_specs=[
            pl.BlockSpec((1, GQA_GROUP, HEAD_DIM, s), lambda i, g: (i, g, 0, 0)),
            pl.BlockSpec((1, 1, s, HEAD_DIM), lambda i, g: (i, g, 0, 0)),
            pl.BlockSpec((1, 1, V_ROWS, s), lambda i, g: (i, g, 0, 0)),
        ],
        out_specs=pl.BlockSpec((1, GQA_GROUP * HEAD_DIM, s), lambda i, g: (i, g, 0)),
        out_shape=jax.ShapeDtypeStruct((b, Q_WIDTH, s), BF16),
        scratch_shapes=[pltpu.VMEM((s, width), F32), pltpu.VMEM((s, width), F32),
                        pltpu.VMEM((1, width), F32), pltpu.VMEM((1, width), F32)],
        compiler_params=_cparams(2),
        name="gqa_attention",
    )(qt, k, vt)


def _mlp_block(x, g, w1_ref, w2_ref):
    h = _rms_rows(x, g).astype(BF16)
    a = jnp.maximum(_dot(h, w1_ref[...]), 0.0)
    return x + _dot((a * a).astype(BF16), w2_ref[...])


def _tail0_kernel(fa_ref, att_ref, x_ref, wa_ref, wb_ref, g_ref, w1_ref, w2_ref, gn_ref,
                  o_ref, ut_ref):
    att = lax.dot_general(att_ref[0], wb_ref[...], (((0,), (0,)), ((), ())),
                          preferred_element_type=F32)
    x1 = x_ref[0] + (_dot(fa_ref[0], wa_ref[...]) + att)
    x2 = _mlp_block(x1, g_ref[...], w1_ref, w2_ref)
    o_ref[0] = x2
    ut_ref[...] = _rms_rows(x2, gn_ref[...]).T.astype(BF16)


def _tail0(fa, att, x, w_out, g_mlp, w1, w2, g_next, tm):
    b, s, d = x.shape
    nblk = s // tm
    w = w_out.astype(BF16)
    const = lambda shape: pl.BlockSpec(shape, lambda i, j: (0,) * len(shape),
                                       pipeline_mode=pl.Buffered(1))
    return pl.pallas_call(
        _tail0_kernel,
        grid=(b, nblk),
        in_specs=[
            pl.BlockSpec((1, tm, MIX_A), lambda i, j: (i, j, 0)),
            pl.BlockSpec((1, Q_WIDTH, tm), lambda i, j: (i, 0, j)),
            pl.BlockSpec((1, tm, d), lambda i, j: (i, j, 0)),
            const((MIX_A, d)), const((Q_WIDTH, d)),
            const((1, d)), const((d, D_FF)), const((D_FF, d)), const((1, d)),
        ],
        out_specs=[pl.BlockSpec((1, tm, d), lambda i, j: (i, j, 0)),
                   pl.BlockSpec((d, tm), lambda i, j: (0, i * nblk + j))],
        out_shape=[jax.ShapeDtypeStruct((b, s, d), F32),
                   jax.ShapeDtypeStruct((d, b * s), BF16)],
        compiler_params=_cparams(2),
        name="outproj_mlp",
    )(fa, att, x, w[:MIX_A], w[MIX_A:], g_mlp[None, :], w1.astype(BF16), w2.astype(BF16),
      g_next[None, :])


def _mlp_kernel(x_ref, g_ref, w1_ref, w2_ref, gf_ref, o_ref, *, final_norm):
    y = _mlp_block(x_ref[0], g_ref[...], w1_ref, w2_ref)
    if final_norm:
        y = _rms_rows(y, gf_ref[...])
    o_ref[0] = y


def _mlp(x, g, w1, w2, g_final, tm, final_norm):
    b, s, d = x.shape
    const = lambda shape: pl.BlockSpec(shape, lambda i, j: (0,) * len(shape),
                                       pipeline_mode=pl.Buffered(1))
    return pl.pallas_call(
        functools.partial(_mlp_kernel, final_norm=final_norm),
        grid=(b, s // tm),
        in_specs=[
            pl.BlockSpec((1, tm, d), lambda i, j: (i, j, 0)),
            const((1, d)), const((d, D_FF)), const((D_FF, d)), const((1, d)),
        ],
        out_specs=pl.BlockSpec((1, tm, d), lambda i, j: (i, j, 0)),
        out_shape=jax.ShapeDtypeStruct((b, s, d), F32),
        compiler_params=_cparams(2),
        name="mlp_final" if final_norm else "mlp",
    )(x, g[None, :], w1.astype(BF16), w2.astype(BF16), g_final[None, :])


def _cmul(ar, ai, br, bi):
    return ar * br - ai * bi, ar * bi + ai * br


def _cpow(a, theta, e):
    mag = jnp.exp(e * a)
    ang = e * theta
    return mag * jnp.cos(ang), mag * jnp.sin(ang)


def _ssm_kernel(ut_ref, prow_ref, crow_ref, ccol_ref, brow_ref, shift_ref, y_ref,
                wall_ref, mmat_ref, cmat_ref, yacc_ref, xprev_ref, *, n_chunks):
    P = SSM_STATE
    C = SSM_GROUP_CH
    L = CHUNK
    rows = ut_ref.shape[2]

    prow = prow_ref[0]
    lre, lim = prow[0:1, :], prow[1:2, :]
    dt = jnp.exp(prow[2:3, :])
    a_row, th_row = lre * dt, lim * dt
    lbr, lbi = _cpow(a_row, th_row, 1.0)
    inv = 1.0 / (lre * lre + lim * lim)
    coef_r, coef_i = _cmul(lbr - 1.0, lbi, lre * inv, -lim * inv)
    bbr, bbi = _cmul(coef_r, coef_i, brow_ref[0, 0], brow_ref[0, 1])

    sq = [(lbr, lbi)]
    while len(sq) <= 7:
        sq.append(_cmul(*sq[-1], *sq[-1]))
    s8 = lax.broadcasted_iota(jnp.int32, (8, 2 * P), 0).astype(F32)
    up = _cpow(a_row, th_row, s8)
    dn = _cpow(a_row, th_row, 7.0 - s8)
    for k in range(3, 7):
        hi_up = _cmul(*up, *sq[k])
        hi_dn = _cmul(*dn, *sq[k])
        up = tuple(jnp.concatenate([lo, hi], axis=0) for lo, hi in zip(up, hi_up))
        dn = tuple(jnp.concatenate([hi, lo], axis=0) for lo, hi in zip(dn, hi_dn))
    up1 = _cmul(*up, lbr, lbi)
    dn1 = _cmul(*dn, lbr, lbi)
    fwd_lane = lax.broadcasted_iota(jnp.int32, (L, 2 * P), 1) < P
    row0 = lax.broadcasted_iota(jnp.int32, (L, 2 * P), 0) == 0

    pwr, pwi = jnp.where(fwd_lane, dn[0], up[0]), jnp.where(fwd_lane, dn[1], up[1])
    for c in range(C):
        xr, xi = _cmul(pwr, pwi, bbr[c:c + 1, :], bbi[c:c + 1, :])
        mmat_ref[c * L:(c + 1) * L, C * L:C * L + 2 * P] = xr.astype(BF16)
        mmat_ref[c * L:(c + 1) * L, C * L + 2 * P:] = xi.astype(BF16)

    qwr = jnp.where(fwd_lane, up1[0], dn1[0]).T
    qwi = jnp.where(fwd_lane, up1[1], dn1[1]).T
    ccr, cci = ccol_ref[0, 0], ccol_ref[0, 1]
    for c in range(C):
        xr, xi = _cmul(qwr, qwi, ccr[:, c:c + 1], cci[:, c:c + 1])
        cmat_ref[:2 * P, c * L:(c + 1) * L] = xr.astype(BF16)
        cmat_ref[2 * P:, c * L:(c + 1) * L] = (-xi).astype(BF16)

    zero = jnp.zeros((L, 2 * P), F32)
    k_lo = (jnp.where(fwd_lane, up[0], jnp.where(row0, 1.0, 0.0)), jnp.where(fwd_lane, up[1], zero))
    k_hi = (jnp.where(fwd_lane, zero, dn1[0]), jnp.where(fwd_lane, zero, dn1[1]))
    kwr = jnp.concatenate([k_lo[0].T, k_hi[0].T], axis=1)
    kwi = jnp.concatenate([k_lo[1].T, k_hi[1].T], axis=1)
    crr, cri = crow_ref[0, 0], crow_ref[0, 1]
    cb_r, cb_i = [], []
    for c in range(C):
        xr, xi = _cmul(crr, cri, bbr[c:c + 1, :], bbi[c:c + 1, :])
        cb_r.append(xr)
        cb_i.append(xi)
    cb_r = jnp.concatenate(cb_r, axis=0)
    cb_i = jnp.concatenate(cb_i, axis=0)
    ktab = _dot3(cb_r, kwr) - _dot3(cb_i, kwi)

    def fine(blk):
        return pltpu.roll(jnp.broadcast_to(ktab[blk:blk + 1, :], (8, 2 * L)), 0, 1,
                          stride=1, stride_axis=0)
    for blk in range(0, C * C, 2):
        wall_ref[blk * 8:blk * 8 + 16, :] = jnp.concatenate(
            [fine(blk), fine(blk + 1)], axis=0).astype(BF16)

    def coarse(hq, carry):
        sh = _dot(wall_ref[...], shift_ref[hq])
        for q2 in range(2):
            lo = 2 * q2 * L
            pair = jnp.concatenate([sh[:, lo:lo + L].reshape(C * C, 8, L),
                                    sh[:, lo + L:lo + 2 * L].reshape(C * C, 8, L)],
                                   axis=1).astype(BF16)
            for cp in range(C):
                r0 = pl.multiple_of(cp * L + 16 * (2 * hq + q2), 16)
                mmat_ref[pl.ds(r0, 16), :C * L] = jnp.concatenate(
                    [pair[cp * C + c] for c in range(C)], axis=1)
        return carry
    lax.fori_loop(0, L // 32, coarse, 0)

    part_w = (C * L + 4 * P) // 3

    def intra(t, carry):
        u = jnp.concatenate([ut_ref[0, c] for c in range(C)], axis=1)
        off = pl.multiple_of(t * part_w, part_w)
        yacc_ref[:, pl.ds(off, part_w)] = _dot(u, mmat_ref[:, pl.ds(off, part_w)])
        return carry
    lax.fori_loop(0, 3, intra, 0)
    xr = yacc_ref[:, C * L:C * L + 2 * P]
    xi = yacc_ref[:, C * L + 2 * P:]

    k_idx = lax.broadcasted_iota(jnp.int32, (rows, 2 * P), 0) % n_chunks
    is_f = lax.broadcasted_iota(jnp.int32, (rows, 2 * P), 1) < P
    ar, ai = sq[7]

    def shifted(t, step):
        down = pltpu.roll(t, step, 0)
        up = pltpu.roll(t, rows - step, 0)
        return jnp.where(is_f, jnp.where(k_idx >= step, down, 0.0),
                         jnp.where(k_idx < n_chunks - step, up, 0.0))

    step = 1
    while step < n_chunks:
        sr, si = shifted(xr, step), shifted(xi, step)
        pr, pi = _cmul(sr, si, ar, ai)
        xr, xi = xr + pr, xi + pi
        ar, ai = _cmul(ar, ai, ar, ai)
        step *= 2
    xprev_ref[...] = jnp.concatenate([shifted(xr, 1), shifted(xi, 1)], axis=1).astype(BF16)

    half_w = C * L // 2

    def inter(hf, carry):
        off = pl.multiple_of(hf * half_w, half_w)
        y = yacc_ref[:, pl.ds(off, half_w)] + _dot(xprev_ref[...], cmat_ref[:, pl.ds(off, half_w)])
        y_ref[0, pl.ds(hf * (C // 2), C // 2)] = jnp.stack(
            [y[:, c * L:(c + 1) * L] for c in range(C // 2)], axis=0)
        return carry
    lax.fori_loop(0, 2, inter, 0)


def _shift_mats():
    L = CHUNK
    m = np.arange(2 * L)[:, None]
    col = np.arange(4 * L)[None, :]
    q, i = col // L, col % L
    mats = [m == (i - 8 * (4 * hq + q)) % (2 * L) for hq in range(L // 32)]
    return jnp.asarray(np.stack(mats), BF16)


def _ssm(ut, lam_re, lam_im, log_dt, b_re, b_im, c_re, c_im, batch, seq):
    G, C, P, L = SSM_GROUPS, SSM_GROUP_CH, SSM_STATE, CHUNK
    n_chunks = seq // L
    rows = batch * n_chunks
    ut4 = ut.reshape(G, C, rows, L)
    f32 = lambda t: t.astype(F32)
    cat_p = lambda t: jnp.concatenate([f32(t[0]), f32(t[1])], axis=-1)
    ldt = jnp.broadcast_to(f32(log_dt)[:, :, None], (2, G, P))
    fields = jnp.stack([cat_p(lam_re), cat_p(lam_im), cat_p(ldt)], axis=1)
    prow = jnp.concatenate([fields, jnp.zeros((G, 5, 2 * P), F32)], axis=1)
    crow = jnp.stack([jnp.concatenate([f32(c_re[0]), f32(c_re[1])], axis=-1),
                      jnp.concatenate([f32(c_im[0]), f32(c_im[1])], axis=-1)], axis=1)
    ccol = jnp.swapaxes(crow, 2, 3)
    brow = jnp.stack([jnp.concatenate([f32(b_re[0]), f32(b_re[1])], axis=1),
                      jnp.concatenate([f32(b_im[0]), f32(b_im[1])], axis=1)], axis=1)
    brow = jnp.swapaxes(brow, 2, 3)
    return pl.pallas_call(
        functools.partial(_ssm_kernel, n_chunks=n_chunks),
        grid=(G,),
        in_specs=[
            pl.BlockSpec((1, C, rows, L), lambda g: (g, 0, 0, 0)),
            pl.BlockSpec((1, 8, 2 * P), lambda g: (g, 0, 0)),
            pl.BlockSpec((1, 2, C, 2 * P), lambda g: (g, 0, 0, 0)),
            pl.BlockSpec((1, 2, 2 * P, C), lambda g: (g, 0, 0, 0)),
            pl.BlockSpec((1, 2, C, 2 * P), lambda g: (g, 0, 0, 0)),
            pl.BlockSpec((L // 32, 2 * L, 4 * L), lambda g: (0, 0, 0), pipeline_mode=pl.Buffered(1)),
        ],
        out_specs=pl.BlockSpec((1, C, rows, L), lambda g: (g, 0, 0, 0)),
        out_shape=jax.ShapeDtypeStruct((G, C, rows, L), F32),
        scratch_shapes=[
            pltpu.VMEM((C * C * 8, 2 * L), BF16),
            pltpu.VMEM((C * L, C * L + 4 * P), BF16),
            pltpu.VMEM((4 * P, C * L), BF16),
            pltpu.VMEM((rows, C * L + 4 * P), F32),
            pltpu.VMEM((rows, 4 * P), BF16),
        ],
        compiler_params=_cparams(1),
        name="s5_scan",
    )(ut4, prow, crow, ccol, brow, _shift_mats())


def _gelu_tanh(y):
    return 0.5 * y * (1.0 + jnp.tanh(math.sqrt(2.0 / math.pi) * (y + 0.044715 * (y * y * y))))


def _gate_kernel(yt_ref, x_ref, g_ref, dsk_ref, wg_ref, bg_ref, o_ref):
    x = x_ref[0]
    u = _rms_rows(x, g_ref[...])
    n_rows = yt_ref.shape[2]
    y_rct = jnp.swapaxes(yt_ref[...].reshape(D_MODEL, n_rows, CHUNK), 0, 1)
    y = jnp.concatenate([y_rct[r].T for r in range(n_rows)], axis=0) + dsk_ref[...] * u
    gl = _gelu_tanh(y)
    gate = _dot(gl.astype(BF16), wg_ref[...]) + bg_ref[...]
    o_ref[0] = x + gl * (0.5 + 0.5 * jnp.tanh(0.5 * gate))


def _gate(y4, x, g, d_skip, w_gate, b_gate, ts):
    b, s, d = x.shape
    nblk = s // ts
    n_rows = ts // CHUNK
    const = lambda shape: pl.BlockSpec(shape, lambda i, j: (0,) * len(shape))
    return pl.pallas_call(
        _gate_kernel,
        grid=(b, nblk),
        in_specs=[
            pl.BlockSpec((SSM_GROUPS, SSM_GROUP_CH, n_rows, CHUNK), lambda i, j: (0, 0, i * nblk + j, 0)),
            pl.BlockSpec((1, ts, d), lambda i, j: (i, j, 0)),
            const((1, d)), const((1, d)), const((d, d)), const((1, d)),
        ],
        out_specs=pl.BlockSpec((1, ts, d), lambda i, j: (i, j, 0)),
        out_shape=jax.ShapeDtypeStruct((b, s, d), F32),
        compiler_params=_cparams(2),
        name="s5_gate",
    )(y4, x, g[None, :], d_skip.astype(F32)[None, :], w_gate.astype(BF16), b_gate.astype(F32)[None, :])


def _pick(n, pref):
    t = min(n, pref)
    while n % t:
        t //= 2
    return t


def kernel(x, norm_mix, norm_mlp, mlp_w1, mlp_w2, w_in, w_fnet, q_norm, k_norm, w_out, lam_re, lam_im, log_dt, b_re, b_im, c_re, c_im, d_skip, w_gate, b_gate, final_norm):
    b, s, d = x.shape
    assert d == D_MODEL and s % CHUNK == 0 and s % GRID_W == 0
    tm = _pick(s, 512)
    gmat = _fourier_fold(w_fnet[0])
    u, v, q, k, vv = _inproj(x, norm_mix[0], w_in[0], gmat, q_norm[0], k_norm[0], tm)
    fa = _seqdft(u, v)
    att = _attention(q, k, vv, _pick(s, 128))
    x, ut = _tail0(fa, att, x, w_out[0], norm_mlp[0], mlp_w1[0], mlp_w2[0], norm_mix[1], tm)
    yt = _ssm(ut, lam_re[0], lam_im[0], log_dt[0], b_re[0], b_im[0], c_re[0], c_im[0], b, s)
    x = _gate(yt, x, norm_mix[1], d_skip[0], w_gate[0], b_gate[0], _pick(s, 1024))
    return _mlp(x, norm_mlp[1], mlp_w1[1], mlp_w2[1], final_norm, tm, True)
```

```python
import functools
import math

import numpy as np
import jax
import jax.numpy as jnp
from jax import lax
from jax.experimental import pallas as pl
from jax.experimental.pallas import tpu as pltpu

F32 = jnp.float32
BF16 = jnp.bfloat16

EPS = 1e-6
D_MODEL = 1024
MIX_A = 512
FNET_HEAD_DIM = 64
FNET_HEADS = MIX_A // FNET_HEAD_DIM
HEAD_DIM = 64
N_Q_HEADS = 8
N_KV_HEADS = 2
GQA_GROUP = N_Q_HEADS // N_KV_HEADS
Q_WIDTH = N_Q_HEADS * HEAD_DIM
KV_WIDTH = N_KV_HEADS * HEAD_DIM
IN_WIDTH = MIX_A + Q_WIDTH + 2 * KV_WIDTH
GRID_W = 64
ROPE_THETA = 10000.0
ROPE_AXIS_DIM = HEAD_DIM // 2
SSM_GROUP_CH = 16
SSM_GROUPS = D_MODEL // SSM_GROUP_CH
SSM_STATE = 64
D_FF = 4 * D_MODEL
CHUNK = 128
LANES = 128
LOG2_E = 1.4426950408889634
V_ROWS = HEAD_DIM + 16
ATTN_KV_CHUNK = 256
VMEM_LIMIT = 56 * 1024 * 1024


def _cparams(n_grid_dims):
    return pltpu.CompilerParams(
        dimension_semantics=("arbitrary",) * n_grid_dims, vmem_limit_bytes=VMEM_LIMIT)


def _dot(a, b):
    return jnp.dot(a, b, preferred_element_type=F32)


def _split_bf16(a):
    hi = a.astype(BF16)
    lo = (a - hi.astype(F32)).astype(BF16)
    return hi, lo


def _dot3(a, b):
    a_hi, a_lo = _split_bf16(a)
    b_hi, b_lo = _split_bf16(b)
    return _dot(a_hi, b_hi) + (_dot(a_hi, b_lo) + _dot(a_lo, b_hi))


def _rms_rows(x, g):
    ms = jnp.mean(x * x, axis=-1, keepdims=True)
    return x * lax.rsqrt(ms + EPS) * g


def _fold_kernel(cc_ref, sc_ref, w_ref, g_ref):
    w = w_ref[...]
    g_ref[:, :MIX_A] = _dot3(cc_ref[...], w).astype(BF16)
    g_ref[:, MIX_A:] = (-_dot3(sc_ref[...], w)).astype(BF16)


def _fourier_fold(w_fnet):
    n = np.arange(FNET_HEAD_DIM)
    ang = 2.0 * np.pi * np.outer(n, n) / FNET_HEAD_DIM
    scale = FNET_HEAD_DIM ** -0.5
    eye = np.eye(FNET_HEADS)
    cc = jnp.asarray(np.kron(eye, np.cos(ang) * scale), F32)
    sc = jnp.asarray(np.kron(eye, np.sin(ang) * scale), F32)
    eye_j = jnp.eye(FNET_HEADS, dtype=F32)
    w_bd = (eye_j[:, None, :, None] * w_fnet.astype(F32)[:, :, None, :]).reshape(MIX_A, MIX_A)
    return pl.pallas_call(
        _fold_kernel,
        out_shape=jax.ShapeDtypeStruct((MIX_A, 2 * MIX_A), BF16),
        name="fourier_fold",
    )(cc, sc, w_bd)


INPROJ_SPLIT = 2


def _head_norm(t, ones, gain):
    hi, lo = _split_bf16(t * t)
    ss = _dot(hi, ones) + _dot(lo, ones)
    return t * lax.rsqrt(ss * (1.0 / HEAD_DIM) + EPS) * gain


def _rope(t, cos, sin):
    width = t.shape[1]
    reps = width // LANES
    c = jnp.concatenate([cos] * reps, axis=1) if reps > 1 else cos
    s = jnp.concatenate([sin] * reps, axis=1) if reps > 1 else sin
    lane = lax.broadcasted_iota(jnp.int32, t.shape, 1)
    first_half = (lane % ROPE_AXIS_DIM) < (ROPE_AXIS_DIM // 2)
    partner = jnp.where(first_half,
                        pltpu.roll(t, width - ROPE_AXIS_DIM // 2, 1),
                        pltpu.roll(t, ROPE_AXIS_DIM // 2, 1))
    return t * c + partner * s


def _inproj_kernel(x_ref, g_ref, win_ref, gmat_ref, ones_ref, qg_ref, kg_ref,
                   cq_ref, sq_ref, ck_ref, sk_ref,
                   u_ref, v_ref, q_ref, k_ref, vv_ref):
    tm = x_ref.shape[1]
    ones = ones_ref[...]
    sub = tm // INPROJ_SPLIT
    for r in range(INPROJ_SPLIT):
        rs = slice(r * sub, (r + 1) * sub)
        h = _rms_rows(x_ref[0, rs, :], g_ref[...]).astype(BF16)
        z = _dot(h, win_ref[...])
        uv = _dot(z[:, :MIX_A].astype(BF16), gmat_ref[...])
        u_ref[0, rs, :] = uv[:, :MIX_A].astype(BF16)
        v_ref[0, rs, :] = uv[:, MIX_A:].astype(BF16)
        q = z[:, MIX_A:MIX_A + Q_WIDTH]
        k = z[:, MIX_A + Q_WIDTH:MIX_A + Q_WIDTH + KV_WIDTH]
        v = z[:, MIX_A + Q_WIDTH + KV_WIDTH:]
        q = _rope(_head_norm(q, ones, qg_ref[...]), cq_ref[rs, :], sq_ref[rs, :])
        k = _rope(_head_norm(k, ones[:KV_WIDTH, :KV_WIDTH], kg_ref[...]), ck_ref[rs, :], sk_ref[rs, :])
        q_ref[0, :, :, rs] = q.T.reshape(N_Q_HEADS, HEAD_DIM, sub).astype(BF16)
        for hh in range(N_KV_HEADS):
            k_ref[0, hh, rs, :] = k[:, hh * HEAD_DIM:(hh + 1) * HEAD_DIM].astype(BF16)
        vv_ref[0, :, :HEAD_DIM, rs] = v.T.reshape(N_KV_HEADS, HEAD_DIM, sub).astype(BF16)
    vv_ref[0, :, HEAD_DIM:, :] = jnp.ones((N_KV_HEADS, V_ROWS - HEAD_DIM, tm), BF16)


def _rope_tables(seq_len):
    t = np.arange(seq_len)
    inv_freq = ROPE_THETA ** (-np.arange(0, ROPE_AXIS_DIM, 2, dtype=np.float64) / ROPE_AXIS_DIM)
    ar = (t // GRID_W)[:, None] * inv_freq
    ac = (t % GRID_W)[:, None] * inv_freq
    cos = np.concatenate([np.cos(ar), np.cos(ar), np.cos(ac), np.cos(ac)], axis=1)
    sin = np.concatenate([-np.sin(ar), np.sin(ar), -np.sin(ac), np.sin(ac)], axis=1)
    cos = np.concatenate([cos, cos], axis=1)
    sin = np.concatenate([sin, sin], axis=1)
    scale = HEAD_DIM ** -0.5 * LOG2_E
    return tuple(jnp.asarray(a, F32) for a in (cos * scale, sin * scale, cos, sin))


def _inproj(x, g, w_in, gmat, q_norm, k_norm, tm):
    b, s, d = x.shape
    cq, sq, ck, sk = _rope_tables(s)
    head = np.arange(Q_WIDTH) // HEAD_DIM
    ones = jnp.asarray(head[:, None] == head[None, :], BF16)
    qg = jnp.tile(q_norm.astype(F32), N_Q_HEADS)[None, :]
    kg = jnp.tile(k_norm.astype(F32), N_KV_HEADS)[None, :]
    const = lambda shape: pl.BlockSpec(shape, lambda i, j: (0,) * len(shape))
    tab = pl.BlockSpec((tm, LANES), lambda i, j: (j, 0))
    return pl.pallas_call(
        _inproj_kernel,
        grid=(b, s // tm),
        in_specs=[
            pl.BlockSpec((1, tm, d), lambda i, j: (i, j, 0)),
            const((1, d)), const((d, IN_WIDTH)), const((MIX_A, 2 * MIX_A)), const((Q_WIDTH, Q_WIDTH)),
            const((1, Q_WIDTH)), const((1, KV_WIDTH)), tab, tab, tab, tab,
        ],
        out_specs=[
            pl.BlockSpec((1, tm, MIX_A), lambda i, j: (i, j, 0)),
            pl.BlockSpec((1, tm, MIX_A), lambda i, j: (i, j, 0)),
            pl.BlockSpec((1, N_Q_HEADS, HEAD_DIM, tm), lambda i, j: (i, 0, 0, j)),
            pl.BlockSpec((1, N_KV_HEADS, tm, HEAD_DIM), lambda i, j: (i, 0, j, 0)),
            pl.BlockSpec((1, N_KV_HEADS, V_ROWS, tm), lambda i, j: (i, 0, 0, j)),
        ],
        out_shape=[
            jax.ShapeDtypeStruct((b, s, MIX_A), BF16),
            jax.ShapeDtypeStruct((b, s, MIX_A), BF16),
            jax.ShapeDtypeStruct((b, N_Q_HEADS, HEAD_DIM, s), BF16),
            jax.ShapeDtypeStruct((b, N_KV_HEADS, s, HEAD_DIM), BF16),
            jax.ShapeDtypeStruct((b, N_KV_HEADS, V_ROWS, s), BF16),
        ],
        compiler_params=_cparams(2),
        name="inproj",
    )(x, g[None, :], w_in.astype(BF16), gmat, ones, qg, kg, cq, sq, ck, sk)


def _dft_tables_kernel(bc_ref, bs_ref, rot_ref, c_ref, s_ref):
    re, im = bc_ref[...], bs_ref[...]
    c_ref[:, :LANES] = re.astype(BF16)
    s_ref[:, :LANES] = im.astype(BF16)
    w, stage = LANES, 0
    while w < c_ref.shape[1]:
        cr = rot_ref[:, 2 * stage:2 * stage + 1]
        sr = rot_ref[:, 2 * stage + 1:2 * stage + 2]
        nre, nim = _cmul(re, im, cr, sr)
        c_ref[:, w:2 * w] = nre.astype(BF16)
        s_ref[:, w:2 * w] = nim.astype(BF16)
        re = jnp.concatenate([re, nre], axis=1)
        im = jnp.concatenate([im, nim], axis=1)
        w, stage = 2 * w, stage + 1


def _dft_tables(seq, tr):
    h = seq // 2
    scale = seq ** -0.5
    j = np.arange(h)[:, None]
    k = np.arange(LANES)[None, :]
    unit = 2.0 * math.pi / seq
    ang = ((j * k) % seq) * unit
    n_stage = int(round(math.log2(h // LANES)))
    widths = np.asarray([LANES << st for st in range(n_stage)] + [0] * (4 - n_stage))
    rang = ((j * widths[None, :]) % seq) * unit
    rot = jnp.asarray(np.stack([np.cos(rang), np.sin(rang)], axis=-1).reshape(h, 8), F32)
    return pl.pallas_call(
        _dft_tables_kernel,
        grid=(h // tr,),
        in_specs=[pl.BlockSpec((tr, LANES), lambda i: (i, 0)),
                  pl.BlockSpec((tr, LANES), lambda i: (i, 0)),
                  pl.BlockSpec((tr, 8), lambda i: (i, 0))],
        out_specs=[pl.BlockSpec((tr, h), lambda i: (i, 0)), pl.BlockSpec((tr, h), lambda i: (i, 0))],
        out_shape=[jax.ShapeDtypeStruct((h, h), BF16), jax.ShapeDtypeStruct((h, h), BF16)],
        compiler_params=_cparams(1),
        name="dft_tables",
    )(jnp.asarray(np.cos(ang) * scale, F32), jnp.asarray(np.sin(ang) * scale, F32), rot)


def _seqdft_kernel(ct_ref, st_ref, u_ref, v_ref, o_ref, ue_ref, vo_ref, d_ref, *, tm, blk):
    t = pl.program_id(1)
    seq = u_ref.shape[1]
    h = seq // 2
    width = u_ref.shape[2]
    scale = seq ** -0.5
    rr = lax.broadcasted_iota(jnp.int32, (blk, blk), 0)
    cc = lax.broadcasted_iota(jnp.int32, (blk, blk), 1)
    flip = jnp.where(rr + cc == blk - 1, 1.0, 0.0).astype(BF16)
    row = lax.broadcasted_iota(jnp.int32, (h, width), 0)

    @pl.when(t == 0)
    def _fold_halves():
        for src_ref, dst_ref, sign in ((u_ref, ue_ref, 1.0), (v_ref, vo_ref, -1.0)):
            rev = jnp.concatenate(
                [_dot(flip, src_ref[0, seq - blk * (a + 1):seq - blk * a, :]) for a in range(h // blk)],
                axis=0)
            mirror = jnp.where(row == 0, 0.0, pltpu.roll(rev, 1, 0))
            dst_ref[...] = (src_ref[0, :h, :].astype(F32) + sign * mirror).astype(BF16)

    ue = ue_ref[...]
    x_mid = u_ref[0, h:h + 1, :].astype(F32) * scale
    p = _dot(ct_ref[...], ue)
    q = _dot(st_ref[...], vo_ref[...])
    j = t * tm + lax.broadcasted_iota(jnp.int32, (tm, 1), 0)
    p = p + (1 - 2 * (j & 1)).astype(F32) * x_mid
    off = pl.multiple_of(t * tm, tm)
    o_ref[0, pl.ds(off, tm), :] = (p + q).astype(BF16)
    d_ref[pl.ds(off, tm), :] = p - q

    @pl.when(t == pl.num_programs(1) - 1)
    def _mirror_half():
        kk = lax.broadcasted_iota(jnp.int32, (8, h), 1)
        alt = ((1 - 2 * (kk & 1)).astype(F32) * scale).astype(BF16)
        p_mid = _dot(alt, ue)[0:1, :] + x_mid
        d_up = jnp.where(row == h - 1, p_mid, pltpu.roll(d_ref[...], h - 1, 0)).astype(BF16)
        for a in range(h // blk):
            o_ref[0, h + blk * a:h + blk * (a + 1), :] = _dot(
                flip, d_up[h - blk * (a + 1):h - blk * a, :]).astype(BF16)


def _seqdft(u, v):
    b, s, w = u.shape
    h = s // 2
    tm = _pick(h, 512)
    blk = _pick(h, 256)
    ct, st = _dft_tables(s, _pick(h, 256))
    return pl.pallas_call(
        functools.partial(_seqdft_kernel, tm=tm, blk=blk),
        grid=(b, h // tm),
        in_specs=[
            pl.BlockSpec((tm, h), lambda i, j: (j, 0)),
            pl.BlockSpec((tm, h), lambda i, j: (j, 0)),
            pl.BlockSpec((1, s, w), lambda i, j: (i, 0, 0)),
            pl.BlockSpec((1, s, w), lambda i, j: (i, 0, 0)),
        ],
        out_specs=pl.BlockSpec((1, s, w), lambda i, j: (i, 0, 0)),
        out_shape=jax.ShapeDtypeStruct((b, s, w), BF16),
        scratch_shapes=[pltpu.VMEM((h, w), BF16), pltpu.VMEM((h, w), BF16), pltpu.VMEM((h, w), F32)],
        compiler_params=_cparams(2),
        name="seqdft",
    )(ct, st, u, v)


def _attn_kernel(qt_ref, k_ref, vt_ref, o_ref, sa_ref, sb_ref, ma_ref, mb_ref, *, tq):
    seq = k_ref.shape[2]
    nblk = seq // tq
    ck = min(seq, ATTN_KV_CHUNK)
    chunks = [slice(c * ck, (c + 1) * ck) for c in range(seq // ck)]

    def query_block(i):
        off = pl.multiple_of(i * tq, tq)
        return jnp.concatenate([qt_ref[0, g, :, pl.ds(off, tq)] for g in range(GQA_GROUP)], axis=1)

    def score_chunk(qt, rows, s_ref, m):
        s = _dot(k_ref[0, 0, rows, :], qt)
        s_ref[rows, :] = s
        mc = jnp.max(s, axis=0, keepdims=True)
        return mc if m is None else jnp.maximum(m, mc)

    def value_chunk(rows, s_ref, m, acc):
        p = jnp.exp2(s_ref[rows, :] - m).astype(BF16)
        part = _dot(vt_ref[0, 0, :, rows], p)
        return part if acc is None else acc + part

    def write_out(i, acc):
        o = acc[:HEAD_DIM] * (1.0 / acc[HEAD_DIM:HEAD_DIM + 1])
        off = pl.multiple_of(i * tq, tq)
        for g in range(GQA_GROUP):
            o_ref[0, g * HEAD_DIM:(g + 1) * HEAD_DIM, pl.ds(off, tq)] = (
                o[:, g * tq:(g + 1) * tq].astype(BF16))

    def step(i_next, s_next, m_next, i_cur, s_cur, m_cur):
        qt = None if i_next is None else query_block(i_next)
        m_old = None if i_cur is None else m_cur[...]
        m_new, acc = None, None
        for rows in chunks:
            if i_next is not None:
                m_new = score_chunk(qt, rows, s_next, m_new)
            if i_cur is not None:
                acc = value_chunk(rows, s_cur, m_old, acc)
        if i_next is not None:
            m_next[...] = m_new
        if i_cur is not None:
            write_out(i_cur, acc)

    step(0, sa_ref, ma_ref, None, None, None)

    def pair(j, carry):
        step(2 * j + 1, sb_ref, mb_ref, 2 * j, sa_ref, ma_ref)
        step(2 * j + 2, sa_ref, ma_ref, 2 * j + 1, sb_ref, mb_ref)
        return carry

    lax.fori_loop(0, nblk // 2 - 1, pair, 0)
    step(nblk - 1, sb_ref, mb_ref, nblk - 2, sa_ref, ma_ref)
    step(None, None, None, nblk - 1, sb_ref, mb_ref)


def _attention(qt, k, vt, tq):
    b, _, _, s = qt.shape
    assert (s // tq) % 2 == 0
    width = GQA_GROUP * tq
    return pl.pallas_call(
        functools.partial(_attn_kernel, tq=tq),
        grid=(b, N_KV_HEADS),
        in_specs=[
            pl.BlockSpec((1, GQA_GROUP, HEAD_DIM, s), lambda i, g: (i, g, 0, 0)),
            pl.BlockSpec((1, 1, s, HEAD_DIM), lambda i, g: (i, g, 0, 0)),
            pl.BlockSpec((1, 1, V_ROWS, s), lambda i, g: (i, g, 0, 0)),
        ],
        out_specs=pl.BlockSpec((1, GQA_GROUP * HEAD_DIM, s), lambda i, g: (i, g, 0)),
        out_shape=jax.ShapeDtypeStruct((b, Q_WIDTH, s), BF16),
        scratch_shapes=[pltpu.VMEM((s, width), F32), pltpu.VMEM((s, width), F32),
                        pltpu.VMEM((1, width), F32), pltpu.VMEM((1, width), F32)],
        compiler_params=_cparams(2),
        name="gqa_attention",
    )(qt, k, vt)


def _mlp_block(x, g, w1_ref, w2_ref):
    h = _rms_rows(x, g).astype(BF16)
    a = jnp.maximum(_dot(h, w1_ref[...]), 0.0)
    return x + _dot((a * a).astype(BF16), w2_ref[...])


def _tail0_kernel(fa_ref, att_ref, x_ref, wa_ref, wb_ref, g_ref, w1_ref, w2_ref, gn_ref,
                  o_ref, ut_ref):
    att = lax.dot_general(att_ref[0], wb_ref[...], (((0,), (0,)), ((), ())),
                          preferred_element_type=F32)
    x1 = x_ref[0] + (_dot(fa_ref[0], wa_ref[...]) + att)
    x2 = _mlp_block(x1, g_ref[...], w1_ref, w2_ref)
    o_ref[0] = x2
    ut_ref[...] = _rms_rows(x2, gn_ref[...]).T.astype(BF16)


def _tail0(fa, att, x, w_out, g_mlp, w1, w2, g_next, tm):
    b, s, d = x.shape
    nblk = s // tm
    w = w_out.astype(BF16)
    const = lambda shape: pl.BlockSpec(shape, lambda i, j: (0,) * len(shape),
                                       pipeline_mode=pl.Buffered(1))
    return pl.pallas_call(
        _tail0_kernel,
        grid=(b, nblk),
        in_specs=[
            pl.BlockSpec((1, tm, MIX_A), lambda i, j: (i, j, 0)),
            pl.BlockSpec((1, Q_WIDTH, tm), lambda i, j: (i, 0, j)),
            pl.BlockSpec((1, tm, d), lambda i, j: (i, j, 0)),
            const((MIX_A, d)), const((Q_WIDTH, d)),
            const((1, d)), const((d, D_FF)), const((D_FF, d)), const((1, d)),
        ],
        out_specs=[pl.BlockSpec((1, tm, d), lambda i, j: (i, j, 0)),
                   pl.BlockSpec((d, tm), lambda i, j: (0, i * nblk + j))],
        out_shape=[jax.ShapeDtypeStruct((b, s, d), F32),
                   jax.ShapeDtypeStruct((d, b * s), BF16)],
        compiler_params=_cparams(2),
        name="outproj_mlp",
    )(fa, att, x, w[:MIX_A], w[MIX_A:], g_mlp[None, :], w1.astype(BF16), w2.astype(BF16),
      g_next[None, :])


def _mlp_kernel(x_ref, g_ref, w1_ref, w2_ref, gf_ref, o_ref, *, final_norm):
    y = _mlp_block(x_ref[0], g_ref[...], w1_ref, w2_ref)
    if final_norm:
        y = _rms_rows(y, gf_ref[...])
    o_ref[0] = y


def _mlp(x, g, w1, w2, g_final, tm, final_norm):
    b, s, d = x.shape
    const = lambda shape: pl.BlockSpec(shape, lambda i, j: (0,) * len(shape),
                                       pipeline_mode=pl.Buffered(1))
    return pl.pallas_call(
        functools.partial(_mlp_kernel, final_norm=final_norm),
        grid=(b, s // tm),
        in_specs=[
            pl.BlockSpec((1, tm, d), lambda i, j: (i, j, 0)),
            const((1, d)), const((d, D_FF)), const((D_FF, d)), const((1, d)),
        ],
        out_specs=pl.BlockSpec((1, tm, d), lambda i, j: (i, j, 0)),
        out_shape=jax.ShapeDtypeStruct((b, s, d), F32),
        compiler_params=_cparams(2),
        name="mlp_final" if final_norm else "mlp",
    )(x, g[None, :], w1.astype(BF16), w2.astype(BF16), g_final[None, :])


def _cmul(ar, ai, br, bi):
    return ar * br - ai * bi, ar * bi + ai * br


def _cpow(a, theta, e):
    mag = jnp.exp(e * a)
    ang = e * theta
    return mag * jnp.cos(ang), mag * jnp.sin(ang)


def _ssm_kernel(ut_ref, prow_ref, crow_ref, ccol_ref, brow_ref, shift_ref, y_ref,
                wall_ref, mmat_ref, cmat_ref, yacc_ref, xprev_ref, *, n_chunks):
    P = SSM_STATE
    C = SSM_GROUP_CH
    L = CHUNK
    rows = ut_ref.shape[2]

    prow = prow_ref[0]
    lre, lim = prow[0:1, :], prow[1:2, :]
    dt = jnp.exp(prow[2:3, :])
    a_row, th_row = lre * dt, lim * dt
    lbr, lbi = _cpow(a_row, th_row, 1.0)
    inv = 1.0 / (lre * lre + lim * lim)
    coef_r, coef_i = _cmul(lbr - 1.0, lbi, lre * inv, -lim * inv)
    bbr, bbi = _cmul(coef_r, coef_i, brow_ref[0, 0], brow_ref[0, 1])

    sq = [(lbr, lbi)]
    while len(sq) <= 7:
        sq.append(_cmul(*sq[-1], *sq[-1]))
    s8 = lax.broadcasted_iota(jnp.int32, (8, 2 * P), 0).astype(F32)
    up = _cpow(a_row, th_row, s8)
    dn = _cpow(a_row, th_row, 7.0 - s8)
    for k in range(3, 7):
        hi_up = _cmul(*up, *sq[k])
        hi_dn = _cmul(*dn, *sq[k])
        up = tuple(jnp.concatenate([lo, hi], axis=0) for lo, hi in zip(up, hi_up))
        dn = tuple(jnp.concatenate([hi, lo], axis=0) for lo, hi in zip(dn, hi_dn))
    up1 = _cmul(*up, lbr, lbi)
    dn1 = _cmul(*dn, lbr, lbi)
    fwd_lane = lax.broadcasted_iota(jnp.int32, (L, 2 * P), 1) < P
    row0 = lax.broadcasted_iota(jnp.int32, (L, 2 * P), 0) == 0

    pwr, pwi = jnp.where(fwd_lane, dn[0], up[0]), jnp.where(fwd_lane, dn[1], up[1])
    for c in range(C):
        xr, xi = _cmul(pwr, pwi, bbr[c:c + 1, :], bbi[c:c + 1, :])
        mmat_ref[c * L:(c + 1) * L, C * L:C * L + 2 * P] = xr.astype(BF16)
        mmat_ref[c * L:(c + 1) * L, C * L + 2 * P:] = xi.astype(BF16)

    qwr = jnp.where(fwd_lane, up1[0], dn1[0]).T
    qwi = jnp.where(fwd_lane, up1[1], dn1[1]).T
    ccr, cci = ccol_ref[0, 0], ccol_ref[0, 1]
    for c in range(C):
        xr, xi = _cmul(qwr, qwi, ccr[:, c:c + 1], cci[:, c:c + 1])
        cmat_ref[:2 * P, c * L:(c + 1) * L] = xr.astype(BF16)
        cmat_ref[2 * P:, c * L:(c + 1) * L] = (-xi).astype(BF16)

    zero = jnp.zeros((L, 2 * P), F32)
    k_lo = (jnp.where(fwd_lane, up[0], jnp.where(row0, 1.0, 0.0)), jnp.where(fwd_lane, up[1], zero))
    k_hi = (jnp.where(fwd_lane, zero, dn1[0]), jnp.where(fwd_lane, zero, dn1[1]))
    kwr = jnp.concatenate([k_lo[0].T, k_hi[0].T], axis=1)
    kwi = jnp.concatenate([k_lo[1].T, k_hi[1].T], axis=1)
    crr, cri = crow_ref[0, 0], crow_ref[0, 1]
    cb_r, cb_i = [], []
    for c in range(C):
        xr, xi = _cmul(crr, cri, bbr[c:c + 1, :], bbi[c:c + 1, :])
        cb_r.append(xr)
        cb_i.append(xi)
    cb_r = jnp.concatenate(cb_r, axis=0)
    cb_i = jnp.concatenate(cb_i, axis=0)
    ktab = _dot3(cb_r, kwr) - _dot3(cb_i, kwi)

    def fine(blk):
        return pltpu.roll(jnp.broadcast_to(ktab[blk:blk + 1, :], (8, 2 * L)), 0, 1,
                          stride=1, stride_axis=0)
    for blk in range(0, C * C, 2):
        wall_ref[blk * 8:blk * 8 + 16, :] = jnp.concatenate(
            [fine(blk), fine(blk + 1)], axis=0).astype(BF16)

    def coarse(hq, carry):
        sh = _dot(wall_ref[...], shift_ref[hq])
        for q2 in range(2):
            lo = 2 * q2 * L
            pair = jnp.concatenate([sh[:, lo:lo + L].reshape(C * C, 8, L),
                                    sh[:, lo + L:lo + 2 * L].reshape(C * C, 8, L)],
                                   axis=1).astype(BF16)
            for cp in range(C):
                r0 = pl.multiple_of(cp * L + 16 * (2 * hq + q2), 16)
                mmat_ref[pl.ds(r0, 16), :C * L] = jnp.concatenate(
                    [pair[cp * C + c] for c in range(C)], axis=1)
        return carry
    lax.fori_loop(0, L // 32, coarse, 0)

    part_w = (C * L + 4 * P) // 3

    def intra(t, carry):
        u = jnp.concatenate([ut_ref[0, c] for c in range(C)], axis=1)
        off = pl.multiple_of(t * part_w, part_w)
        yacc_ref[:, pl.ds(off, part_w)] = _dot(u, mmat_ref[:, pl.ds(off, part_w)])
        return carry
    lax.fori_loop(0, 3, intra, 0)
    xr = yacc_ref[:, C * L:C * L + 2 * P]
    xi = yacc_ref[:, C * L + 2 * P:]

    k_idx = lax.broadcasted_iota(jnp.int32, (rows, 2 * P), 0) % n_chunks
    is_f = lax.broadcasted_iota(jnp.int32, (rows, 2 * P), 1) < P
    ar, ai = sq[7]

    def shifted(t, step):
        down = pltpu.roll(t, step, 0)
        up = pltpu.roll(t, rows - step, 0)
        return jnp.where(is_f, jnp.where(k_idx >= step, down, 0.0),
                         jnp.where(k_idx < n_chunks - step, up, 0.0))

    step = 1
    while step < n_chunks:
        sr, si = shifted(xr, step), shifted(xi, step)
        pr, pi = _cmul(sr, si, ar, ai)
        xr, xi = xr + pr, xi + pi
        ar, ai = _cmul(ar, ai, ar, ai)
        step *= 2
    xprev_ref[...] = jnp.concatenate([shifted(xr, 1), shifted(xi, 1)], axis=1).astype(BF16)

    half_w = C * L // 2

    def inter(hf, carry):
        off = pl.multiple_of(hf * half_w, half_w)
        y = yacc_ref[:, pl.ds(off, half_w)] + _dot(xprev_ref[...], cmat_ref[:, pl.ds(off, half_w)])
        y_ref[0, pl.ds(hf * (C // 2), C // 2)] = jnp.stack(
            [y[:, c * L:(c + 1) * L] for c in range(C // 2)], axis=0)
        return carry
    lax.fori_loop(0, 2, inter, 0)


def _shift_mats():
    L = CHUNK
    m = np.arange(2 * L)[:, None]
    col = np.arange(4 * L)[None, :]
    q, i = col // L, col % L
    mats = [m == (i - 8 * (4 * hq + q)) % (2 * L) for hq in range(L // 32)]
    return jnp.asarray(np.stack(mats), BF16)


def _ssm(ut, lam_re, lam_im, log_dt, b_re, b_im, c_re, c_im, batch, seq):
    G, C, P, L = SSM_GROUPS, SSM_GROUP_CH, SSM_STATE, CHUNK
    n_chunks = seq // L
    rows = batch * n_chunks
    ut4 = ut.reshape(G, C, rows, L)
    f32 = lambda t: t.astype(F32)
    cat_p = lambda t: jnp.concatenate([f32(t[0]), f32(t[1])], axis=-1)
    ldt = jnp.broadcast_to(f32(log_dt)[:, :, None], (2, G, P))
    fields = jnp.stack([cat_p(lam_re), cat_p(lam_im), cat_p(ldt)], axis=1)
    prow = jnp.concatenate([fields, jnp.zeros((G, 5, 2 * P), F32)], axis=1)
    crow = jnp.stack([jnp.concatenate([f32(c_re[0]), f32(c_re[1])], axis=-1),
                      jnp.concatenate([f32(c_im[0]), f32(c_im[1])], axis=-1)], axis=1)
    ccol = jnp.swapaxes(crow, 2, 3)
    brow = jnp.stack([jnp.concatenate([f32(b_re[0]), f32(b_re[1])], axis=1),
                      jnp.concatenate([f32(b_im[0]), f32(b_im[1])], axis=1)], axis=1)
    brow = jnp.swapaxes(brow, 2, 3)
    return pl.pallas_call(
        functools.partial(_ssm_kernel, n_chunks=n_chunks),
        grid=(G,),
        in_specs=[
            pl.BlockSpec((1, C, rows, L), lambda g: (g, 0, 0, 0)),
            pl.BlockSpec((1, 8, 2 * P), lambda g: (g, 0, 0)),
            pl.BlockSpec((1, 2, C, 2 * P), lambda g: (g, 0, 0, 0)),
            pl.BlockSpec((1, 2, 2 * P, C), lambda g: (g, 0, 0, 0)),
            pl.BlockSpec((1, 2, C, 2 * P), lambda g: (g, 0, 0, 0)),
            pl.BlockSpec((L // 32, 2 * L, 4 * L), lambda g: (0, 0, 0), pipeline_mode=pl.Buffered(1)),
        ],
        out_specs=pl.BlockSpec((1, C, rows, L), lambda g: (g, 0, 0, 0)),
        out_shape=jax.ShapeDtypeStruct((G, C, rows, L), F32),
        scratch_shapes=[
            pltpu.VMEM((C * C * 8, 2 * L), BF16),
            pltpu.VMEM((C * L, C * L + 4 * P), BF16),
            pltpu.VMEM((4 * P, C * L), BF16),
            pltpu.VMEM((rows, C * L + 4 * P), F32),
            pltpu.VMEM((rows, 4 * P), BF16),
        ],
        compiler_params=_cparams(1),
        name="s5_scan",
    )(ut4, prow, crow, ccol, brow, _shift_mats())


def _gelu_tanh(y):
    return 0.5 * y * (1.0 + jnp.tanh(math.sqrt(2.0 / math.pi) * (y + 0.044715 * (y * y * y))))


def _gate_kernel(yt_ref, x_ref, g_ref, dsk_ref, wg_ref, bg_ref, o_ref):
    x = x_ref[0]
    u = _rms_rows(x, g_ref[...])
    n_rows = yt_ref.shape[2]
    y_rct = jnp.swapaxes(yt_ref[...].reshape(D_MODEL, n_rows, CHUNK), 0, 1)
    y = jnp.concatenate([y_rct[r].T for r in range(n_rows)], axis=0) + dsk_ref[...] * u
    gl = _gelu_tanh(y)
    gate = _dot(gl.astype(BF16), wg_ref[...]) + bg_ref[...]
    o_ref[0] = x + gl * (0.5 + 0.5 * jnp.tanh(0.5 * gate))


def _gate(y4, x, g, d_skip, w_gate, b_gate, ts):
    b, s, d = x.shape
    nblk = s // ts
    n_rows = ts // CHUNK
    const = lambda shape: pl.BlockSpec(shape, lambda i, j: (0,) * len(shape))
    return pl.pallas_call(
        _gate_kernel,
        grid=(b, nblk),
        in_specs=[
            pl.BlockSpec((SSM_GROUPS, SSM_GROUP_CH, n_rows, CHUNK), lambda i, j: (0, 0, i * nblk + j, 0)),
            pl.BlockSpec((1, ts, d), lambda i, j: (i, j, 0)),
            const((1, d)), const((1, d)), const((d, d)), const((1, d)),
        ],
        out_specs=pl.BlockSpec((1, ts, d), lambda i, j: (i, j, 0)),
        out_shape=jax.ShapeDtypeStruct((b, s, d), F32),
        compiler_params=_cparams(2),
        name="s5_gate",
    )(y4, x, g[None, :], d_skip.astype(F32)[None, :], w_gate.astype(BF16), b_gate.astype(F32)[None, :])


def _pick(n, pref):
    t = min(n, pref)
    while n % t:
        t //= 2
    return t


def kernel(x, norm_mix, norm_mlp, mlp_w1, mlp_w2, w_in, w_fnet, q_norm, k_norm, w_out, lam_re, lam_im, log_dt, b_re, b_im, c_re, c_im, d_skip, w_gate, b_gate, final_norm):
    b, s, d = x.shape
    assert d == D_MODEL and s % CHUNK == 0 and s % GRID_W == 0
    tm = _pick(s, 512)
    gmat = _fourier_fold(w_fnet[0])
    u, v, q, k, vv = _inproj(x, norm_mix[0], w_in[0], gmat, q_norm[0], k_norm[0], tm)
    fa = _seqdft(u, v)
    att = _attention(q, k, vv, _pick(s, 128))
    x, ut = _tail0(fa, att, x, w_out[0], norm_mlp[0], mlp_w1[0], mlp_w2[0], norm_mix[1], tm)
    yt = _ssm(ut, lam_re[0], lam_im[0], log_dt[0], b_re[0], b_im[0], c_re[0], c_im[0], b, s)
    x = _gate(yt, x, norm_mix[1], d_skip[0], w_gate[0], b_gate[0], _pick(s, 1024))
    return _mlp(x, norm_mlp[1], mlp_w1[1], mlp_w2[1], final_norm, tm, True)
```

```python
import functools
import math

import numpy as np
import jax
import jax.numpy as jnp
from jax import lax
from jax.experimental import pallas as pl
from jax.experimental.pallas import tpu as pltpu

F32 = jnp.float32
BF16 = jnp.bfloat16

EPS = 1e-6
D_MODEL = 1024
MIX_A = 512
FNET_HEAD_DIM = 64
FNET_HEADS = MIX_A // FNET_HEAD_DIM
HEAD_DIM = 64
N_Q_HEADS = 8
N_KV_HEADS = 2
GQA_GROUP = N_Q_HEADS // N_KV_HEADS
Q_WIDTH = N_Q_HEADS * HEAD_DIM
KV_WIDTH = N_KV_HEADS * HEAD_DIM
IN_WIDTH = MIX_A + Q_WIDTH + 2 * KV_WIDTH
GRID_W = 64
ROPE_THETA = 10000.0
ROPE_AXIS_DIM = HEAD_DIM // 2
SSM_GROUP_CH = 16
SSM_GROUPS = D_MODEL // SSM_GROUP_CH
SSM_STATE = 64
D_FF = 4 * D_MODEL
CHUNK = 128
LANES = 128
LOG2_E = 1.4426950408889634
V_ROWS = HEAD_DIM + 16
ATTN_KV_CHUNK = 256
SHIFTS_PER_DOT = 8
VMEM_LIMIT = 56 * 1024 * 1024


def _cparams(n_grid_dims):
    return pltpu.CompilerParams(
        dimension_semantics=("arbitrary",) * n_grid_dims, vmem_limit_bytes=VMEM_LIMIT)


def _dot(a, b):
    return jnp.dot(a, b, preferred_element_type=F32)


def _split_bf16(a):
    hi = a.astype(BF16)
    lo = (a - hi.astype(F32)).astype(BF16)
    return hi, lo


def _dot3(a, b):
    a_hi, a_lo = _split_bf16(a)
    b_hi, b_lo = _split_bf16(b)
    return _dot(a_hi, b_hi) + (_dot(a_hi, b_lo) + _dot(a_lo, b_hi))


def _rms_rows(x, g):
    ms = jnp.mean(x * x, axis=-1, keepdims=True)
    return x * lax.rsqrt(ms + EPS) * g


def _fold_kernel(cc_ref, sc_ref, w_ref, g_ref):
    w = w_ref[...]
    g_ref[:, :MIX_A] = _dot3(cc_ref[...], w).astype(BF16)
    g_ref[:, MIX_A:] = (-_dot3(sc_ref[...], w)).astype(BF16)


def _fourier_fold(w_fnet):
    n = np.arange(FNET_HEAD_DIM)
    ang = 2.0 * np.pi * np.outer(n, n) / FNET_HEAD_DIM
    scale = FNET_HEAD_DIM ** -0.5
    eye = np.eye(FNET_HEADS)
    cc = jnp.asarray(np.kron(eye, np.cos(ang) * scale), F32)
    sc = jnp.asarray(np.kron(eye, np.sin(ang) * scale), F32)
    eye_j = jnp.eye(FNET_HEADS, dtype=F32)
    w_bd = (eye_j[:, None, :, None] * w_fnet.astype(F32)[:, :, None, :]).reshape(MIX_A, MIX_A)
    return pl.pallas_call(
        _fold_kernel,
        out_shape=jax.ShapeDtypeStruct((MIX_A, 2 * MIX_A), BF16),
        name="fourier_fold",
    )(cc, sc, w_bd)


INPROJ_SPLIT = 2


def _head_norm(t, ones, gain):
    hi, lo = _split_bf16(t * t)
    ss = _dot(hi, ones) + _dot(lo, ones)
    return t * lax.rsqrt(ss * (1.0 / HEAD_DIM) + EPS) * gain


def _rope(t, cos, sin):
    width = t.shape[1]
    reps = width // LANES
    c = jnp.concatenate([cos] * reps, axis=1) if reps > 1 else cos
    s = jnp.concatenate([sin] * reps, axis=1) if reps > 1 else sin
    lane = lax.broadcasted_iota(jnp.int32, t.shape, 1)
    first_half = (lane % ROPE_AXIS_DIM) < (ROPE_AXIS_DIM // 2)
    partner = jnp.where(first_half,
                        pltpu.roll(t, width - ROPE_AXIS_DIM // 2, 1),
                        pltpu.roll(t, ROPE_AXIS_DIM // 2, 1))
    return t * c + partner * s


def _inproj_kernel(x_ref, g_ref, win_ref, gmat_ref, ones_ref, qg_ref, kg_ref,
                   cq_ref, sq_ref, ck_ref, sk_ref,
                   u_ref, v_ref, q_ref, k_ref, vv_ref):
    tm = x_ref.shape[1]
    ones = ones_ref[...]
    sub = tm // INPROJ_SPLIT
    for r in range(INPROJ_SPLIT):
        rs = slice(r * sub, (r + 1) * sub)
        h = _rms_rows(x_ref[0, rs, :], g_ref[...]).astype(BF16)
        z = _dot(h, win_ref[...])
        uv = _dot(z[:, :MIX_A].astype(BF16), gmat_ref[...])
        u_ref[0, rs, :] = uv[:, :MIX_A].astype(BF16)
        v_ref[0, rs, :] = uv[:, MIX_A:].astype(BF16)
        q = z[:, MIX_A:MIX_A + Q_WIDTH]
        k = z[:, MIX_A + Q_WIDTH:MIX_A + Q_WIDTH + KV_WIDTH]
        v = z[:, MIX_A + Q_WIDTH + KV_WIDTH:]
        q = _rope(_head_norm(q, ones, qg_ref[...]), cq_ref[rs, :], sq_ref[rs, :])
        k = _rope(_head_norm(k, ones[:KV_WIDTH, :KV_WIDTH], kg_ref[...]), ck_ref[rs, :], sk_ref[rs, :])
        q_ref[0, :, :, rs] = q.T.reshape(N_Q_HEADS, HEAD_DIM, sub).astype(BF16)
        for hh in range(N_KV_HEADS):
            k_ref[0, hh, rs, :] = k[:, hh * HEAD_DIM:(hh + 1) * HEAD_DIM].astype(BF16)
        vv_ref[0, :, :HEAD_DIM, rs] = v.T.reshape(N_KV_HEADS, HEAD_DIM, sub).astype(BF16)
    vv_ref[0, :, HEAD_DIM:, :] = jnp.ones((N_KV_HEADS, V_ROWS - HEAD_DIM, tm), BF16)


def _rope_tables(seq_len):
    t = np.arange(seq_len)
    inv_freq = ROPE_THETA ** (-np.arange(0, ROPE_AXIS_DIM, 2, dtype=np.float64) / ROPE_AXIS_DIM)
    ar = (t // GRID_W)[:, None] * inv_freq
    ac = (t % GRID_W)[:, None] * inv_freq
    cos = np.concatenate([np.cos(ar), np.cos(ar), np.cos(ac), np.cos(ac)], axis=1)
    sin = np.concatenate([-np.sin(ar), np.sin(ar), -np.sin(ac), np.sin(ac)], axis=1)
    cos = np.concatenate([cos, cos], axis=1)
    sin = np.concatenate([sin, sin], axis=1)
    scale = HEAD_DIM ** -0.5 * LOG2_E
    return tuple(jnp.asarray(a, F32) for a in (cos * scale, sin * scale, cos, sin))


def _inproj(x, g, w_in, gmat, q_norm, k_norm, tm):
    b, s, d = x.shape
    cq, sq, ck, sk = _rope_tables(s)
    head = np.arange(Q_WIDTH) // HEAD_DIM
    ones = jnp.asarray(head[:, None] == head[None, :], BF16)
    qg = jnp.tile(q_norm.astype(F32), N_Q_HEADS)[None, :]
    kg = jnp.tile(k_norm.astype(F32), N_KV_HEADS)[None, :]
    const = lambda shape: pl.BlockSpec(shape, lambda i, j: (0,) * len(shape))
    tab = pl.BlockSpec((tm, LANES), lambda i, j: (j, 0))
    return pl.pallas_call(
        _inproj_kernel,
        grid=(b, s // tm),
        in_specs=[
            pl.BlockSpec((1, tm, d), lambda i, j: (i, j, 0)),
            const((1, d)), const((d, IN_WIDTH)), const((MIX_A, 2 * MIX_A)), const((Q_WIDTH, Q_WIDTH)),
            const((1, Q_WIDTH)), const((1, KV_WIDTH)), tab, tab, tab, tab,
        ],
        out_specs=[
            pl.BlockSpec((1, tm, MIX_A), lambda i, j: (i, j, 0)),
            pl.BlockSpec((1, tm, MIX_A), lambda i, j: (i, j, 0)),
            pl.BlockSpec((1, N_Q_HEADS, HEAD_DIM, tm), lambda i, j: (i, 0, 0, j)),
            pl.BlockSpec((1, N_KV_HEADS, tm, HEAD_DIM), lambda i, j: (i, 0, j, 0)),
            pl.BlockSpec((1, N_KV_HEADS, V_ROWS, tm), lambda i, j: (i, 0, 0, j)),
        ],
        out_shape=[
            jax.ShapeDtypeStruct((b, s, MIX_A), BF16),
            jax.ShapeDtypeStruct((b, s, MIX_A), BF16),
            jax.ShapeDtypeStruct((b, N_Q_HEADS, HEAD_DIM, s), BF16),
            jax.ShapeDtypeStruct((b, N_KV_HEADS, s, HEAD_DIM), BF16),
            jax.ShapeDtypeStruct((b, N_KV_HEADS, V_ROWS, s), BF16),
        ],
        compiler_params=_cparams(2),
        name="inproj",
    )(x, g[None, :], w_in.astype(BF16), gmat, ones, qg, kg, cq, sq, ck, sk)


def _dft_tables_kernel(bc_ref, bs_ref, rot_ref, c_ref, s_ref):
    re, im = bc_ref[...], bs_ref[...]
    c_ref[:, :LANES] = re.astype(BF16)
    s_ref[:, :LANES] = im.astype(BF16)
    w, stage = LANES, 0
    while w < c_ref.shape[1]:
        cr = rot_ref[:, 2 * stage:2 * stage + 1]
        sr = rot_ref[:, 2 * stage + 1:2 * stage + 2]
        nre, nim = _cmul(re, im, cr, sr)
        c_ref[:, w:2 * w] = nre.astype(BF16)
        s_ref[:, w:2 * w] = nim.astype(BF16)
        re = jnp.concatenate([re, nre], axis=1)
        im = jnp.concatenate([im, nim], axis=1)
        w, stage = 2 * w, stage + 1


def _dft_tables(seq, tr):
    h = seq // 2
    scale = seq ** -0.5
    j = np.arange(h)[:, None]
    k = np.arange(LANES)[None, :]
    unit = 2.0 * math.pi / seq
    ang = ((j * k) % seq) * unit
    n_stage = int(round(math.log2(h // LANES)))
    widths = np.asarray([LANES << st for st in range(n_stage)] + [0] * (4 - n_stage))
    rang = ((j * widths[None, :]) % seq) * unit
    rot = jnp.asarray(np.stack([np.cos(rang), np.sin(rang)], axis=-1).reshape(h, 8), F32)
    return pl.pallas_call(
        _dft_tables_kernel,
        grid=(h // tr,),
        in_specs=[pl.BlockSpec((tr, LANES), lambda i: (i, 0)),
                  pl.BlockSpec((tr, LANES), lambda i: (i, 0)),
                  pl.BlockSpec((tr, 8), lambda i: (i, 0))],
        out_specs=[pl.BlockSpec((tr, h), lambda i: (i, 0)), pl.BlockSpec((tr, h), lambda i: (i, 0))],
        out_shape=[jax.ShapeDtypeStruct((h, h), BF16), jax.ShapeDtypeStruct((h, h), BF16)],
        compiler_params=_cparams(1),
        name="dft_tables",
    )(jnp.asarray(np.cos(ang) * scale, F32), jnp.asarray(np.sin(ang) * scale, F32), rot)


def _seqdft_kernel(ct_ref, st_ref, u_ref, v_ref, o_ref, ue_ref, vo_ref, d_ref, *, tm, blk):
    t = pl.program_id(1)
    seq = u_ref.shape[1]
    h = seq // 2
    width = u_ref.shape[2]
    scale = seq ** -0.5
    rr = lax.broadcasted_iota(jnp.int32, (blk, blk), 0)
    cc = lax.broadcasted_iota(jnp.int32, (blk, blk), 1)
    flip = jnp.where(rr + cc == blk - 1, 1.0, 0.0).astype(BF16)
    row = lax.broadcasted_iota(jnp.int32, (h, width), 0)

    @pl.when(t == 0)
    def _fold_halves():
        for src_ref, dst_ref, sign in ((u_ref, ue_ref, 1.0), (v_ref, vo_ref, -1.0)):
            rev = jnp.concatenate(
                [_dot(flip, src_ref[0, seq - blk * (a + 1):seq - blk * a, :]) for a in range(h // blk)],
                axis=0)
            mirror = jnp.where(row == 0, 0.0, pltpu.roll(rev, 1, 0))
            dst_ref[...] = (src_ref[0, :h, :].astype(F32) + sign * mirror).astype(BF16)

    ue = ue_ref[...]
    x_mid = u_ref[0, h:h + 1, :].astype(F32) * scale
    p = _dot(ct_ref[...], ue)
    q = _dot(st_ref[...], vo_ref[...])
    j = t * tm + lax.broadcasted_iota(jnp.int32, (tm, 1), 0)
    p = p + (1 - 2 * (j & 1)).astype(F32) * x_mid
    off = pl.multiple_of(t * tm, tm)
    o_ref[0, pl.ds(off, tm), :] = (p + q).astype(BF16)
    d_ref[pl.ds(off, tm), :] = p - q

    @pl.when(t == pl.num_programs(1) - 1)
    def _mirror_half():
        kk = lax.broadcasted_iota(jnp.int32, (8, h), 1)
        alt = ((1 - 2 * (kk & 1)).astype(F32) * scale).astype(BF16)
        p_mid = _dot(alt, ue)[0:1, :] + x_mid
        d_up = jnp.where(row == h - 1, p_mid, pltpu.roll(d_ref[...], h - 1, 0)).astype(BF16)
        for a in range(h // blk):
            o_ref[0, h + blk * a:h + blk * (a + 1), :] = _dot(
                flip, d_up[h - blk * (a + 1):h - blk * a, :]).astype(BF16)


def _seqdft(u, v):
    b, s, w = u.shape
    h = s // 2
    tm = _pick(h, 512)
    blk = _pick(h, 256)
    ct, st = _dft_tables(s, _pick(h, 256))
    return pl.pallas_call(
        functools.partial(_seqdft_kernel, tm=tm, blk=blk),
        grid=(b, h // tm),
        in_specs=[
            pl.BlockSpec((tm, h), lambda i, j: (j, 0)),
            pl.BlockSpec((tm, h), lambda i, j: (j, 0)),
            pl.BlockSpec((1, s, w), lambda i, j: (i, 0, 0)),
            pl.BlockSpec((1, s, w), lambda i, j: (i, 0, 0)),
        ],
        out_specs=pl.BlockSpec((1, s, w), lambda i, j: (i, 0, 0)),
        out_shape=jax.ShapeDtypeStruct((b, s, w), BF16),
        scratch_shapes=[pltpu.VMEM((h, w), BF16), pltpu.VMEM((h, w), BF16), pltpu.VMEM((h, w), F32)],
        compiler_params=_cparams(2),
        name="seqdft",
    )(ct, st, u, v)


def _attn_kernel(qt_ref, k_ref, vt_ref, o_ref, sa_ref, sb_ref, ma_ref, mb_ref, *, tq):
    seq = k_ref.shape[2]
    nblk = seq // tq
    ck = min(seq, ATTN_KV_CHUNK)
    chunks = [slice(c * ck, (c + 1) * ck) for c in range(seq // ck)]

    def query_block(i):
        off = pl.multiple_of(i * tq, tq)
        return jnp.concatenate([qt_ref[0, g, :, pl.ds(off, tq)] for g in range(GQA_GROUP)], axis=1)

    def score_chunk(qt, rows, s_ref, m):
        s = _dot(k_ref[0, 0, rows, :], qt)
        s_ref[rows, :] = s
        mc = jnp.max(s, axis=0, keepdims=True)
        return mc if m is None else jnp.maximum(m, mc)

    def value_chunk(rows, s_ref, m, acc):
        p = jnp.exp2(s_ref[rows, :] - m).astype(BF16)
        part = _dot(vt_ref[0, 0, :, rows], p)
        return part if acc is None else acc + part

    def write_out(i, acc):
        o = acc[:HEAD_DIM] * (1.0 / acc[HEAD_DIM:HEAD_DIM + 1])
        off = pl.multiple_of(i * tq, tq)
        for g in range(GQA_GROUP):
            o_ref[0, g * HEAD_DIM:(g + 1) * HEAD_DIM, pl.ds(off, tq)] = (
                o[:, g * tq:(g + 1) * tq].astype(BF16))

    def step(i_next, s_next, m_next, i_cur, s_cur, m_cur):
        qt = None if i_next is None else query_block(i_next)
        m_old = None if i_cur is None else m_cur[...]
        m_new, acc = None, None
        for rows in chunks:
            if i_next is not None:
                m_new = score_chunk(qt, rows, s_next, m_new)
            if i_cur is not None:
                acc = value_chunk(rows, s_cur, m_old, acc)
        if i_next is not None:
            m_next[...] = m_new
        if i_cur is not None:
            write_out(i_cur, acc)

    step(0, sa_ref, ma_ref, None, None, None)

    def pair(j, carry):
        step(2 * j + 1, sb_ref, mb_ref, 2 * j, sa_ref, ma_ref)
        step(2 * j + 2, sa_ref, ma_ref, 2 * j + 1, sb_ref, mb_ref)
        return carry

    lax.fori_loop(0, nblk // 2 - 1, pair, 0)
    step(nblk - 1, sb_ref, mb_ref, nblk - 2, sa_ref, ma_ref)
    step(None, None, None, nblk - 1, sb_ref, mb_ref)


def _attention(qt, k, vt, tq):
    b, _, _, s = qt.shape
    assert (s // tq) % 2 == 0
    width = GQA_GROUP * tq
    return pl.pallas_call(
        functools.partial(_attn_kernel, tq=tq),
        grid=(b, N_KV_HEADS),
        in_specs=[
            pl.BlockSpec((1, GQA_GROUP, HEAD_DIM, s), lambda i, g: (i, g, 0, 0)),
            pl.BlockSpec((1, 1, s, HEAD_DIM), lambda i, g: (i, g, 0, 0)),
            pl.BlockSpec((1, 1, V_ROWS, s), lambda i, g: (i, g, 0, 0)),
        ],
        out_specs=pl.BlockSpec((1, GQA_GROUP * HEAD_DIM, s), lambda i, g: (i, g, 0)),
        out_shape=jax.ShapeDtypeStruct((b, Q_WIDTH, s), BF16),
        scratch_shapes=[pltpu.VMEM((s, width), F32), pltpu.VMEM((s, width), F32),
                        pltpu.VMEM((1, width), F32), pltpu.VMEM((1, width), F32)],
        compiler_params=_cparams(2),
        name="gqa_attention",
    )(qt, k, vt)


def _mlp_block(x, g, w1_ref, w2_ref):
    h = _rms_rows(x, g).astype(BF16)
    a = jnp.maximum(_dot(h, w1_ref[...]), 0.0)
    return x + _dot((a * a).astype(BF16), w2_ref[...])


def _tail0_kernel(fa_ref, att_ref, x_ref, wa_ref, wb_ref, g_ref, w1_ref, w2_ref, gn_ref,
                  o_ref, ut_ref):
    att = lax.dot_general(att_ref[0], wb_ref[...], (((0,), (0,)), ((), ())),
                          preferred_element_type=F32)
    x1 = x_ref[0] + (_dot(fa_ref[0], wa_ref[...]) + att)
    x2 = _mlp_block(x1, g_ref[...], w1_ref, w2_ref)
    o_ref[0] = x2
    ut_ref[...] = _rms_rows(x2, gn_ref[...]).T.astype(BF16)


def _tail0(fa, att, x, w_out, g_mlp, w1, w2, g_next, tm):
    b, s, d = x.shape
    nblk = s // tm
    w = w_out.astype(BF16)
    const = lambda shape: pl.BlockSpec(shape, lambda i, j: (0,) * len(shape),
                                       pipeline_mode=pl.Buffered(1))
    return pl.pallas_call(
        _tail0_kernel,
        grid=(b, nblk),
        in_specs=[
            pl.BlockSpec((1, tm, MIX_A), lambda i, j: (i, j, 0)),
            pl.BlockSpec((1, Q_WIDTH, tm), lambda i, j: (i, 0, j)),
            pl.BlockSpec((1, tm, d), lambda i, j: (i, j, 0)),
            const((MIX_A, d)), const((Q_WIDTH, d)),
            const((1, d)), const((d, D_FF)), const((D_FF, d)), const((1, d)),
        ],
        out_specs=[pl.BlockSpec((1, tm, d), lambda i, j: (i, j, 0)),
                   pl.BlockSpec((d, tm), lambda i, j: (0, i * nblk + j))],
        out_shape=[jax.ShapeDtypeStruct((b, s, d), F32),
                   jax.ShapeDtypeStruct((d, b * s), BF16)],
        compiler_params=_cparams(2),
        name="outproj_mlp",
    )(fa, att, x, w[:MIX_A], w[MIX_A:], g_mlp[None, :], w1.astype(BF16), w2.astype(BF16),
      g_next[None, :])


def _mlp_kernel(x_ref, g_ref, w1_ref, w2_ref, gf_ref, o_ref, *, final_norm):
    y = _mlp_block(x_ref[0], g_ref[...], w1_ref, w2_ref)
    if final_norm:
        y = _rms_rows(y, gf_ref[...])
    o_ref[0] = y


def _mlp(x, g, w1, w2, g_final, tm, final_norm):
    b, s, d = x.shape
    const = lambda shape: pl.BlockSpec(shape, lambda i, j: (0,) * len(shape),
                                       pipeline_mode=pl.Buffered(1))
    return pl.pallas_call(
        functools.partial(_mlp_kernel, final_norm=final_norm),
        grid=(b, s // tm),
        in_specs=[
            pl.BlockSpec((1, tm, d), lambda i, j: (i, j, 0)),
            const((1, d)), const((d, D_FF)), const((D_FF, d)), const((1, d)),
        ],
        out_specs=pl.BlockSpec((1, tm, d), lambda i, j: (i, j, 0)),
        out_shape=jax.ShapeDtypeStruct((b, s, d), F32),
        compiler_params=_cparams(2),
        name="mlp_final" if final_norm else "mlp",
    )(x, g[None, :], w1.astype(BF16), w2.astype(BF16), g_final[None, :])


def _cmul(ar, ai, br, bi):
    return ar * br - ai * bi, ar * bi + ai * br


def _cpow(a, theta, e):
    mag = jnp.exp(e * a)
    ang = e * theta
    return mag * jnp.cos(ang), mag * jnp.sin(ang)


def _ssm_kernel(ut_ref, prow_ref, crow_ref, bcol_ref, shift_ref, y_ref,
                wall_ref, mmat_ref, cmat_ref, yacc_ref, xprev_ref, *, n_chunks):
    P = SSM_STATE
    C = SSM_GROUP_CH
    L = CHUNK
    rows = ut_ref.shape[2]

    prow = prow_ref[0]
    lre, lim = prow[0:1, :], prow[1:2, :]
    dt = jnp.exp(prow[2:3, :])
    a_row, th_row = lre * dt, lim * dt
    lbr, lbi = _cpow(a_row, th_row, 1.0)
    inv = 1.0 / (lre * lre + lim * lim)
    coef_r, coef_i = _cmul(lbr - 1.0, lbi, lre * inv, -lim * inv)
    bbr, bbi = _cmul(coef_r, coef_i, bcol_ref[0, 0].T, bcol_ref[0, 1].T)

    sq = [(lbr, lbi)]
    while len(sq) <= 7:
        sq.append(_cmul(*sq[-1], *sq[-1]))
    s8 = lax.broadcasted_iota(jnp.int32, (8, 2 * P), 0).astype(F32)
    up = _cpow(a_row, th_row, s8)
    dn = _cpow(a_row, th_row, 7.0 - s8)
    for k in range(3, 7):
        hi_up = _cmul(*up, *sq[k])
        hi_dn = _cmul(*dn, *sq[k])
        up = tuple(jnp.concatenate([lo, hi], axis=0) for lo, hi in zip(up, hi_up))
        dn = tuple(jnp.concatenate([hi, lo], axis=0) for lo, hi in zip(dn, hi_dn))
    up1 = _cmul(*up, lbr, lbi)
    dn1 = _cmul(*dn, lbr, lbi)
    fwd_lane = lax.broadcasted_iota(jnp.int32, (L, 2 * P), 1) < P
    row0 = lax.broadcasted_iota(jnp.int32, (L, 2 * P), 0) == 0

    pwr, pwi = jnp.where(fwd_lane, dn[0], up[0]), jnp.where(fwd_lane, dn[1], up[1])
    for c in range(C):
        xr, xi = _cmul(pwr, pwi, bbr[c:c + 1, :], bbi[c:c + 1, :])
        mmat_ref[c * L:(c + 1) * L, C * L:C * L + 2 * P] = xr.astype(BF16)
        mmat_ref[c * L:(c + 1) * L, C * L + 2 * P:] = xi.astype(BF16)

    qwr = jnp.where(fwd_lane, up1[0], dn1[0]).T
    qwi = jnp.where(fwd_lane, up1[1], dn1[1]).T
    ccr, cci = crow_ref[0, 0].T, crow_ref[0, 1].T
    for c in range(C):
        xr, xi = _cmul(qwr, qwi, ccr[:, c:c + 1], cci[:, c:c + 1])
        cmat_ref[:2 * P, c * L:(c + 1) * L] = xr.astype(BF16)
        cmat_ref[2 * P:, c * L:(c + 1) * L] = (-xi).astype(BF16)

    zero = jnp.zeros((L, 2 * P), F32)
    k_lo = (jnp.where(fwd_lane, up[0], jnp.where(row0, 1.0, 0.0)), jnp.where(fwd_lane, up[1], zero))
    k_hi = (jnp.where(fwd_lane, zero, dn1[0]), jnp.where(fwd_lane, zero, dn1[1]))
    kwr = jnp.concatenate([k_lo[0].T, k_hi[0].T], axis=1)
    kwi = jnp.concatenate([k_lo[1].T, k_hi[1].T], axis=1)
    crr, cri = crow_ref[0, 0], crow_ref[0, 1]
    cb_r, cb_i = [], []
    for c in range(C):
        xr, xi = _cmul(crr, cri, bbr[c:c + 1, :], bbi[c:c + 1, :])
        cb_r.append(xr)
        cb_i.append(xi)
    cb_r = jnp.concatenate(cb_r, axis=0)
    cb_i = jnp.concatenate(cb_i, axis=0)
    ktab = _dot3(cb_r, kwr) - _dot3(cb_i, kwi)

    def fine(blk):
        return pltpu.roll(jnp.broadcast_to(ktab[blk:blk + 1, :], (8, 2 * L)), 0, 1,
                          stride=1, stride_axis=0)
    for blk in range(0, C * C, 2):
        wall_ref[blk * 8:blk * 8 + 16, :] = jnp.concatenate(
            [fine(blk), fine(blk + 1)], axis=0).astype(BF16)

    def coarse(hq, carry):
        sh = _dot(wall_ref[...], shift_ref[hq])
        for q2 in range(SHIFTS_PER_DOT // 2):
            lo = 2 * q2 * L
            pair = jnp.concatenate([sh[:, lo:lo + L].reshape(C * C, 8, L),
                                    sh[:, lo + L:lo + 2 * L].reshape(C * C, 8, L)],
                                   axis=1).astype(BF16)
            for cp in range(C):
                r0 = pl.multiple_of(cp * L + 16 * ((SHIFTS_PER_DOT // 2) * hq + q2), 16)
                mmat_ref[pl.ds(r0, 16), :C * L] = jnp.concatenate(
                    [pair[cp * C + c] for c in range(C)], axis=1)
        return carry
    lax.fori_loop(0, L // (8 * SHIFTS_PER_DOT), coarse, 0)

    part_w = (C * L + 4 * P) // 3

    def intra(t, carry):
        u = jnp.concatenate([ut_ref[0, c] for c in range(C)], axis=1)
        off = pl.multiple_of(t * part_w, part_w)
        yacc_ref[:, pl.ds(off, part_w)] = _dot(u, mmat_ref[:, pl.ds(off, part_w)])
        return carry
    lax.fori_loop(0, 3, intra, 0)
    xr = yacc_ref[:, C * L:C * L + 2 * P]
    xi = yacc_ref[:, C * L + 2 * P:]

    k_idx = lax.broadcasted_iota(jnp.int32, (rows, 2 * P), 0) % n_chunks
    is_f = lax.broadcasted_iota(jnp.int32, (rows, 2 * P), 1) < P
    ar, ai = sq[7]

    def shifted(t, step):
        down = pltpu.roll(t, step, 0)
        up = pltpu.roll(t, rows - step, 0)
        return jnp.where(is_f, jnp.where(k_idx >= step, down, 0.0),
                         jnp.where(k_idx < n_chunks - step, up, 0.0))

    step = 1
    while step < n_chunks:
        sr, si = shifted(xr, step), shifted(xi, step)
        pr, pi = _cmul(sr, si, ar, ai)
        xr, xi = xr + pr, xi + pi
        ar, ai = _cmul(ar, ai, ar, ai)
        step *= 2
    xprev_ref[...] = jnp.concatenate([shifted(xr, 1), shifted(xi, 1)], axis=1).astype(BF16)

    half_w = C * L // 2

    def inter(hf, carry):
        off = pl.multiple_of(hf * half_w, half_w)
        y = yacc_ref[:, pl.ds(off, half_w)] + _dot(xprev_ref[...], cmat_ref[:, pl.ds(off, half_w)])
        y_ref[0, pl.ds(hf * (C // 2), C // 2)] = jnp.stack(
            [y[:, c * L:(c + 1) * L] for c in range(C // 2)], axis=0)
        return carry
    lax.fori_loop(0, 2, inter, 0)


def _shift_mats():
    L = CHUNK
    m = np.arange(2 * L)[:, None]
    col = np.arange(SHIFTS_PER_DOT * L)[None, :]
    q, i = col // L, col % L
    mats = [m == (i - 8 * (SHIFTS_PER_DOT * hq + q)) % (2 * L)
            for hq in range(L // (8 * SHIFTS_PER_DOT))]
    return jnp.asarray(np.stack(mats), BF16)


def _ssm(ut, lam_re, lam_im, log_dt, b_re, b_im, c_re, c_im, batch, seq):
    G, C, P, L = SSM_GROUPS, SSM_GROUP_CH, SSM_STATE, CHUNK
    n_chunks = seq // L
    rows = batch * n_chunks
    ut4 = ut.reshape(G, C, rows, L)
    f32 = lambda t: t.astype(F32)
    ldt = jnp.broadcast_to(f32(log_dt)[:, :, None], (2, G, P))
    prow = jnp.stack([f32(lam_re), f32(lam_im), ldt]).transpose(2, 0, 1, 3).reshape(G, 3, 2 * P)
    crow = jnp.stack([f32(c_re), f32(c_im)]).transpose(2, 0, 3, 1, 4).reshape(G, 2, C, 2 * P)
    bcol = jnp.stack([f32(b_re), f32(b_im)]).transpose(2, 0, 1, 3, 4).reshape(G, 2, 2 * P, C)
    return pl.pallas_call(
        functools.partial(_ssm_kernel, n_chunks=n_chunks),
        grid=(G,),
        in_specs=[
            pl.BlockSpec((1, C, rows, L), lambda g: (g, 0, 0, 0)),
            pl.BlockSpec((1, 3, 2 * P), lambda g: (g, 0, 0)),
            pl.BlockSpec((1, 2, C, 2 * P), lambda g: (g, 0, 0, 0)),
            pl.BlockSpec((1, 2, 2 * P, C), lambda g: (g, 0, 0, 0)),
            pl.BlockSpec((L // (8 * SHIFTS_PER_DOT), 2 * L, SHIFTS_PER_DOT * L), lambda g: (0, 0, 0),
                         pipeline_mode=pl.Buffered(1)),
        ],
        out_specs=pl.BlockSpec((1, C, rows, L), lambda g: (g, 0, 0, 0)),
        out_shape=jax.ShapeDtypeStruct((G, C, rows, L), F32),
        scratch_shapes=[
            pltpu.VMEM((C * C * 8, 2 * L), BF16),
            pltpu.VMEM((C * L, C * L + 4 * P), BF16),
            pltpu.VMEM((4 * P, C * L), BF16),
            pltpu.VMEM((rows, C * L + 4 * P), F32),
            pltpu.VMEM((rows, 4 * P), BF16),
        ],
        compiler_params=_cparams(1),
        name="s5_scan",
    )(ut4, prow, crow, bcol, _shift_mats())


def _gelu_tanh(y):
    return 0.5 * y * (1.0 + jnp.tanh(math.sqrt(2.0 / math.pi) * (y + 0.044715 * (y * y * y))))


def _gate_kernel(yt_ref, x_ref, g_ref, dsk_ref, wg_ref, bg_ref, o_ref):
    n_rows = yt_ref.shape[2]
    y_rct = jnp.swapaxes(yt_ref[...].reshape(D_MODEL, n_rows, CHUNK), 0, 1)
    for r in range(0, n_rows, 2):
        rs = slice(r * CHUNK, (r + 2) * CHUNK)
        x = x_ref[0, rs, :]
        u = _rms_rows(x, g_ref[...])
        y = jnp.concatenate([y_rct[r].T, y_rct[r + 1].T], axis=0) + dsk_ref[...] * u
        gl = _gelu_tanh(y)
        gate = _dot(gl.astype(BF16), wg_ref[...]) + bg_ref[...]
        o_ref[0, rs, :] = x + gl * (0.5 + 0.5 * jnp.tanh(0.5 * gate))


def _gate(y4, x, g, d_skip, w_gate, b_gate, ts):
    b, s, d = x.shape
    nblk = s // ts
    n_rows = ts // CHUNK
    const = lambda shape: pl.BlockSpec(shape, lambda i, j: (0,) * len(shape))
    return pl.pallas_call(
        _gate_kernel,
        grid=(b, nblk),
        in_specs=[
            pl.BlockSpec((SSM_GROUPS, SSM_GROUP_CH, n_rows, CHUNK), lambda i, j: (0, 0, i * nblk + j, 0)),
            pl.BlockSpec((1, ts, d), lambda i, j: (i, j, 0)),
            const((1, d)), const((1, d)), const((d, d)), const((1, d)),
        ],
        out_specs=pl.BlockSpec((1, ts, d), lambda i, j: (i, j, 0)),
        out_shape=jax.ShapeDtypeStruct((b, s, d), F32),
        compiler_params=_cparams(2),
        name="s5_gate",
    )(y4, x, g[None, :], d_skip.astype(F32)[None, :], w_gate.astype(BF16), b_gate.astype(F32)[None, :])


def _pick(n, pref):
    t = min(n, pref)
    while n % t:
        t //= 2
    return t


def kernel(x, norm_mix, norm_mlp, mlp_w1, mlp_w2, w_in, w_fnet, q_norm, k_norm, w_out, lam_re, lam_im, log_dt, b_re, b_im, c_re, c_im, d_skip, w_gate, b_gate, final_norm):
    b, s, d = x.shape
    assert d == D_MODEL and s % CHUNK == 0 and s % GRID_W == 0
    tm = _pick(s, 512)
    gmat = _fourier_fold(w_fnet[0])
    u, v, q, k, vv = _inproj(x, norm_mix[0], w_in[0], gmat, q_norm[0], k_norm[0], tm)
    fa = _seqdft(u, v)
    att = _attention(q, k, vv, _pick(s, 128))
    x, ut = _tail0(fa, att, x, w_out[0], norm_mlp[0], mlp_w1[0], mlp_w2[0], norm_mix[1], tm)
    yt = _ssm(ut, lam_re[0], lam_im[0], log_dt[0], b_re[0], b_im[0], c_re[0], c_im[0], b, s)
    x = _gate(yt, x, norm_mix[1], d_skip[0], w_gate[0], b_gate[0], _pick(s, 1024))
    return _mlp(x, norm_mlp[1], mlp_w1[1], mlp_w2[1], final_norm, tm, True)
```

```python
import functools
import math

import numpy as np
import jax
import jax.numpy as jnp
from jax import lax
from jax.experimental import pallas as pl
from jax.experimental.pallas import tpu as pltpu

F32 = jnp.float32
BF16 = jnp.bfloat16

EPS = 1e-6
D_MODEL = 1024
MIX_A = 512
FNET_HEAD_DIM = 64
FNET_HEADS = MIX_A // FNET_HEAD_DIM
HEAD_DIM = 64
N_Q_HEADS = 8
N_KV_HEADS = 2
GQA_GROUP = N_Q_HEADS // N_KV_HEADS
Q_WIDTH = N_Q_HEADS * HEAD_DIM
KV_WIDTH = N_KV_HEADS * HEAD_DIM
IN_WIDTH = MIX_A + Q_WIDTH + 2 * KV_WIDTH
GRID_W = 64
ROPE_THETA = 10000.0
ROPE_AXIS_DIM = HEAD_DIM // 2
SSM_GROUP_CH = 16
SSM_GROUPS = D_MODEL // SSM_GROUP_CH
SSM_STATE = 64
D_FF = 4 * D_MODEL
CHUNK = 128
LANES = 128
LOG2_E = 1.4426950408889634
V_ROWS = HEAD_DIM + 16
ATTN_KV_CHUNK = 256
SHIFTS_PER_DOT = 8
VMEM_LIMIT = 56 * 1024 * 1024


def _cparams(n_grid_dims):
    return pltpu.CompilerParams(
        dimension_semantics=("arbitrary",) * n_grid_dims, vmem_limit_bytes=VMEM_LIMIT)


def _dot(a, b):
    return jnp.dot(a, b, preferred_element_type=F32)


def _split_bf16(a):
    hi = a.astype(BF16)
    lo = (a - hi.astype(F32)).astype(BF16)
    return hi, lo


def _dot3(a, b):
    a_hi, a_lo = _split_bf16(a)
    b_hi, b_lo = _split_bf16(b)
    return _dot(a_hi, b_hi) + (_dot(a_hi, b_lo) + _dot(a_lo, b_hi))


def _rms_rows(x, g):
    ms = jnp.mean(x * x, axis=-1, keepdims=True)
    return x * lax.rsqrt(ms + EPS) * g


def _fold_kernel(cc_ref, sc_ref, w_ref, g_ref):
    w = w_ref[...]
    g_ref[:, :MIX_A] = _dot3(cc_ref[...], w).astype(BF16)
    g_ref[:, MIX_A:] = (-_dot3(sc_ref[...], w)).astype(BF16)


def _fourier_fold(w_fnet):
    n = np.arange(FNET_HEAD_DIM)
    ang = 2.0 * np.pi * np.outer(n, n) / FNET_HEAD_DIM
    scale = FNET_HEAD_DIM ** -0.5
    eye = np.eye(FNET_HEADS)
    cc = jnp.asarray(np.kron(eye, np.cos(ang) * scale), F32)
    sc = jnp.asarray(np.kron(eye, np.sin(ang) * scale), F32)
    eye_j = jnp.eye(FNET_HEADS, dtype=F32)
    w_bd = (eye_j[:, None, :, None] * w_fnet.astype(F32)[:, :, None, :]).reshape(MIX_A, MIX_A)
    return pl.pallas_call(
        _fold_kernel,
        out_shape=jax.ShapeDtypeStruct((MIX_A, 2 * MIX_A), BF16),
        name="fourier_fold",
    )(cc, sc, w_bd)


INPROJ_SPLIT = 4


def _head_norm(t, ones, gain):
    hi, lo = _split_bf16(t * t)
    ss = _dot(hi, ones) + _dot(lo, ones)
    return t * lax.rsqrt(ss * (1.0 / HEAD_DIM) + EPS) * gain


def _rope(t, cos, sin):
    width = t.shape[1]
    reps = width // LANES
    c = jnp.concatenate([cos] * reps, axis=1) if reps > 1 else cos
    s = jnp.concatenate([sin] * reps, axis=1) if reps > 1 else sin
    lane = lax.broadcasted_iota(jnp.int32, t.shape, 1)
    first_half = (lane % ROPE_AXIS_DIM) < (ROPE_AXIS_DIM // 2)
    partner = jnp.where(first_half,
                        pltpu.roll(t, width - ROPE_AXIS_DIM // 2, 1),
                        pltpu.roll(t, ROPE_AXIS_DIM // 2, 1))
    return t * c + partner * s


def _inproj_kernel(x_ref, g_ref, win_ref, gmat_ref, ones_ref, qg_ref, kg_ref,
                   cq_ref, sq_ref, ck_ref, sk_ref,
                   u_ref, v_ref, q_ref, k_ref, vv_ref):
    tm = x_ref.shape[1]
    ones = ones_ref[...]
    sub = tm // INPROJ_SPLIT
    rows = [slice(r * sub, (r + 1) * sub) for r in range(INPROJ_SPLIT)]

    def project(rs):
        h = _rms_rows(x_ref[0, rs, :], g_ref[...]).astype(BF16)
        return _dot(h, win_ref[...])

    def finish(rs, z):
        uv = _dot(z[:, :MIX_A].astype(BF16), gmat_ref[...])
        u_ref[0, rs, :] = uv[:, :MIX_A].astype(BF16)
        v_ref[0, rs, :] = uv[:, MIX_A:].astype(BF16)
        q = z[:, MIX_A:MIX_A + Q_WIDTH]
        k = z[:, MIX_A + Q_WIDTH:MIX_A + Q_WIDTH + KV_WIDTH]
        v = z[:, MIX_A + Q_WIDTH + KV_WIDTH:]
        q = _rope(_head_norm(q, ones, qg_ref[...]), cq_ref[rs, :], sq_ref[rs, :])
        k = _rope(_head_norm(k, ones[:KV_WIDTH, :KV_WIDTH], kg_ref[...]), ck_ref[rs, :], sk_ref[rs, :])
        q_ref[0, :, :, rs] = q.T.reshape(N_Q_HEADS, HEAD_DIM, sub).astype(BF16)
        for hh in range(N_KV_HEADS):
            k_ref[0, hh, rs, :] = k[:, hh * HEAD_DIM:(hh + 1) * HEAD_DIM].astype(BF16)
        vv_ref[0, :, :HEAD_DIM, rs] = v.T.reshape(N_KV_HEADS, HEAD_DIM, sub).astype(BF16)

    z_prev = project(rows[0])
    for r in range(1, INPROJ_SPLIT):
        z_next = project(rows[r])
        finish(rows[r - 1], z_prev)
        z_prev = z_next
    finish(rows[-1], z_prev)
    vv_ref[0, :, HEAD_DIM:, :] = jnp.ones((N_KV_HEADS, V_ROWS - HEAD_DIM, tm), BF16)


def _rope_tables(seq_len):
    t = np.arange(seq_len)
    inv_freq = ROPE_THETA ** (-np.arange(0, ROPE_AXIS_DIM, 2, dtype=np.float64) / ROPE_AXIS_DIM)
    ar = (t // GRID_W)[:, None] * inv_freq
    ac = (t % GRID_W)[:, None] * inv_freq
    cos = np.concatenate([np.cos(ar), np.cos(ar), np.cos(ac), np.cos(ac)], axis=1)
    sin = np.concatenate([-np.sin(ar), np.sin(ar), -np.sin(ac), np.sin(ac)], axis=1)
    cos = np.concatenate([cos, cos], axis=1)
    sin = np.concatenate([sin, sin], axis=1)
    scale = HEAD_DIM ** -0.5 * LOG2_E
    return tuple(jnp.asarray(a, F32) for a in (cos * scale, sin * scale, cos, sin))


def _inproj(x, g, w_in, gmat, q_norm, k_norm, tm):
    b, s, d = x.shape
    cq, sq, ck, sk = _rope_tables(s)
    head = np.arange(Q_WIDTH) // HEAD_DIM
    ones = jnp.asarray(head[:, None] == head[None, :], BF16)
    qg = jnp.tile(q_norm.astype(F32), N_Q_HEADS)[None, :]
    kg = jnp.tile(k_norm.astype(F32), N_KV_HEADS)[None, :]
    const = lambda shape: pl.BlockSpec(shape, lambda i, j: (0,) * len(shape))
    tab = pl.BlockSpec((tm, LANES), lambda i, j: (j, 0))
    return pl.pallas_call(
        _inproj_kernel,
        grid=(b, s // tm),
        in_specs=[
            pl.BlockSpec((1, tm, d), lambda i, j: (i, j, 0)),
            const((1, d)), const((d, IN_WIDTH)), const((MIX_A, 2 * MIX_A)), const((Q_WIDTH, Q_WIDTH)),
            const((1, Q_WIDTH)), const((1, KV_WIDTH)), tab, tab, tab, tab,
        ],
        out_specs=[
            pl.BlockSpec((1, tm, MIX_A), lambda i, j: (i, j, 0)),
            pl.BlockSpec((1, tm, MIX_A), lambda i, j: (i, j, 0)),
            pl.BlockSpec((1, N_Q_HEADS, HEAD_DIM, tm), lambda i, j: (i, 0, 0, j)),
            pl.BlockSpec((1, N_KV_HEADS, tm, HEAD_DIM), lambda i, j: (i, 0, j, 0)),
            pl.BlockSpec((1, N_KV_HEADS, V_ROWS, tm), lambda i, j: (i, 0, 0, j)),
        ],
        out_shape=[
            jax.ShapeDtypeStruct((b, s, MIX_A), BF16),
            jax.ShapeDtypeStruct((b, s, MIX_A), BF16),
            jax.ShapeDtypeStruct((b, N_Q_HEADS, HEAD_DIM, s), BF16),
            jax.ShapeDtypeStruct((b, N_KV_HEADS, s, HEAD_DIM), BF16),
            jax.ShapeDtypeStruct((b, N_KV_HEADS, V_ROWS, s), BF16),
        ],
        compiler_params=_cparams(2),
        name="inproj",
    )(x, g[None, :], w_in.astype(BF16), gmat, ones, qg, kg, cq, sq, ck, sk)


def _dft_tables_kernel(bc_ref, bs_ref, rot_ref, c_ref, s_ref):
    re, im = bc_ref[...], bs_ref[...]
    c_ref[:, :LANES] = re.astype(BF16)
    s_ref[:, :LANES] = im.astype(BF16)
    w, stage = LANES, 0
    while w < c_ref.shape[1]:
        cr = rot_ref[:, 2 * stage:2 * stage + 1]
        sr = rot_ref[:, 2 * stage + 1:2 * stage + 2]
        nre, nim = _cmul(re, im, cr, sr)
        c_ref[:, w:2 * w] = nre.astype(BF16)
        s_ref[:, w:2 * w] = nim.astype(BF16)
        re = jnp.concatenate([re, nre], axis=1)
        im = jnp.concatenate([im, nim], axis=1)
        w, stage = 2 * w, stage + 1


def _dft_tables(seq, tr):
    h = seq // 2
    scale = seq ** -0.5
    j = np.arange(h)[:, None]
    k = np.arange(LANES)[None, :]
    unit = 2.0 * math.pi / seq
    ang = ((j * k) % seq) * unit
    n_stage = int(round(math.log2(h // LANES)))
    widths = np.asarray([LANES << st for st in range(n_stage)] + [0] * (4 - n_stage))
    rang = ((j * widths[None, :]) % seq) * unit
    rot = jnp.asarray(np.stack([np.cos(rang), np.sin(rang)], axis=-1).reshape(h, 8), F32)
    return pl.pallas_call(
        _dft_tables_kernel,
        grid=(h // tr,),
        in_specs=[pl.BlockSpec((tr, LANES), lambda i: (i, 0)),
                  pl.BlockSpec((tr, LANES), lambda i: (i, 0)),
                  pl.BlockSpec((tr, 8), lambda i: (i, 0))],
        out_specs=[pl.BlockSpec((tr, h), lambda i: (i, 0)), pl.BlockSpec((tr, h), lambda i: (i, 0))],
        out_shape=[jax.ShapeDtypeStruct((h, h), BF16), jax.ShapeDtypeStruct((h, h), BF16)],
        compiler_params=_cparams(1),
        name="dft_tables",
    )(jnp.asarray(np.cos(ang) * scale, F32), jnp.asarray(np.sin(ang) * scale, F32), rot)


def _seqdft_kernel(ct_ref, st_ref, u_ref, v_ref, o_ref, ue_ref, vo_ref, d_ref, *, tm, blk):
    t = pl.program_id(1)
    seq = u_ref.shape[1]
    h = seq // 2
    width = u_ref.shape[2]
    scale = seq ** -0.5
    rr = lax.broadcasted_iota(jnp.int32, (blk, blk), 0)
    cc = lax.broadcasted_iota(jnp.int32, (blk, blk), 1)
    flip = jnp.where(rr + cc == blk - 1, 1.0, 0.0).astype(BF16)
    row = lax.broadcasted_iota(jnp.int32, (h, width), 0)

    @pl.when(t == 0)
    def _fold_halves():
        for src_ref, dst_ref, sign in ((u_ref, ue_ref, 1.0), (v_ref, vo_ref, -1.0)):
            rev = jnp.concatenate(
                [_dot(flip, src_ref[0, seq - blk * (a + 1):seq - blk * a, :]) for a in range(h // blk)],
                axis=0)
            mirror = jnp.where(row == 0, 0.0, pltpu.roll(rev, 1, 0))
            dst_ref[...] = (src_ref[0, :h, :].astype(F32) + sign * mirror).astype(BF16)

    ue = ue_ref[...]
    x_mid = u_ref[0, h:h + 1, :].astype(F32) * scale
    p = _dot(ct_ref[...], ue)
    q = _dot(st_ref[...], vo_ref[...])
    j = t * tm + lax.broadcasted_iota(jnp.int32, (tm, 1), 0)
    p = p + (1 - 2 * (j & 1)).astype(F32) * x_mid
    off = pl.multiple_of(t * tm, tm)
    o_ref[0, pl.ds(off, tm), :] = (p + q).astype(BF16)
    d_ref[pl.ds(off, tm), :] = p - q

    @pl.when(t == pl.num_programs(1) - 1)
    def _mirror_half():
        kk = lax.broadcasted_iota(jnp.int32, (8, h), 1)
        alt = ((1 - 2 * (kk & 1)).astype(F32) * scale).astype(BF16)
        p_mid = _dot(alt, ue)[0:1, :] + x_mid
        d_up = jnp.where(row == h - 1, p_mid, pltpu.roll(d_ref[...], h - 1, 0)).astype(BF16)
        for a in range(h // blk):
            o_ref[0, h + blk * a:h + blk * (a + 1), :] = _dot(
                flip, d_up[h - blk * (a + 1):h - blk * a, :]).astype(BF16)


def _seqdft(u, v):
    b, s, w = u.shape
    h = s // 2
    tm = _pick(h, 512)
    blk = _pick(h, 256)
    ct, st = _dft_tables(s, _pick(h, 256))
    return pl.pallas_call(
        functools.partial(_seqdft_kernel, tm=tm, blk=blk),
        grid=(b, h // tm),
        in_specs=[
            pl.BlockSpec((tm, h), lambda i, j: (j, 0)),
            pl.BlockSpec((tm, h), lambda i, j: (j, 0)),
            pl.BlockSpec((1, s, w), lambda i, j: (i, 0, 0)),
            pl.BlockSpec((1, s, w), lambda i, j: (i, 0, 0)),
        ],
        out_specs=pl.BlockSpec((1, s, w), lambda i, j: (i, 0, 0)),
        out_shape=jax.ShapeDtypeStruct((b, s, w), BF16),
        scratch_shapes=[pltpu.VMEM((h, w), BF16), pltpu.VMEM((h, w), BF16), pltpu.VMEM((h, w), F32)],
        compiler_params=_cparams(2),
        name="seqdft",
    )(ct, st, u, v)


def _attn_kernel(qt_ref, k_ref, vt_ref, o_ref, sa_ref, sb_ref, ma_ref, mb_ref, *, tq):
    seq = k_ref.shape[2]
    per_head = seq // tq
    nblk = N_KV_HEADS * per_head
    ck = min(seq, ATTN_KV_CHUNK)
    chunks = [slice(c * ck, (c + 1) * ck) for c in range(seq // ck)]

    def locate(i):
        head = i // per_head
        return head, pl.multiple_of((i - head * per_head) * tq, tq)

    def query_block(i):
        head, off = locate(i)
        return jnp.concatenate(
            [qt_ref[0, head * GQA_GROUP + g, :, pl.ds(off, tq)] for g in range(GQA_GROUP)], axis=1)

    def score_chunk(i, qt, rows, s_ref, m):
        s = _dot(k_ref[0, locate(i)[0], rows, :], qt)
        s_ref[rows, :] = s
        mc = jnp.max(s, axis=0, keepdims=True)
        return mc if m is None else jnp.maximum(m, mc)

    def value_chunk(i, rows, s_ref, m, acc):
        p = jnp.exp2(s_ref[rows, :] - m).astype(BF16)
        part = _dot(vt_ref[0, locate(i)[0], :, rows], p)
        return part if acc is None else acc + part

    def write_out(i, acc):
        o = acc[:HEAD_DIM] * (1.0 / acc[HEAD_DIM:HEAD_DIM + 1])
        head, off = locate(i)
        for g in range(GQA_GROUP):
            r0 = pl.multiple_of((head * GQA_GROUP + g) * HEAD_DIM, HEAD_DIM)
            o_ref[0, pl.ds(r0, HEAD_DIM), pl.ds(off, tq)] = o[:, g * tq:(g + 1) * tq].astype(BF16)

    def step(i_next, s_next, m_next, i_cur, s_cur, m_cur):
        qt = None if i_next is None else query_block(i_next)
        m_old = None if i_cur is None else m_cur[...]
        m_new, acc = None, None
        for rows in chunks:
            if i_next is not None:
                m_new = score_chunk(i_next, qt, rows, s_next, m_new)
            if i_cur is not None:
                acc = value_chunk(i_cur, rows, s_cur, m_old, acc)
        if i_next is not None:
            m_next[...] = m_new
        if i_cur is not None:
            write_out(i_cur, acc)

    step(0, sa_ref, ma_ref, None, None, None)

    def pair(j, carry):
        step(2 * j + 1, sb_ref, mb_ref, 2 * j, sa_ref, ma_ref)
        step(2 * j + 2, sa_ref, ma_ref, 2 * j + 1, sb_ref, mb_ref)
        return carry

    lax.fori_loop(0, nblk // 2 - 1, pair, 0)
    step(nblk - 1, sb_ref, mb_ref, nblk - 2, sa_ref, ma_ref)
    step(None, None, None, nblk - 1, sb_ref, mb_ref)


def _attention(qt, k, vt, tq):
    b, _, _, s = qt.shape
    assert (s // tq) % 2 == 0
    width = GQA_GROUP * tq
    return pl.pallas_call(
        functools.partial(_attn_kernel, tq=tq),
        grid=(b,),
        in_specs=[
            pl.BlockSpec((1, N_Q_HEADS, HEAD_DIM, s), lambda i: (i, 0, 0, 0)),
            pl.BlockSpec((1, N_KV_HEADS, s, HEAD_DIM), lambda i: (i, 0, 0, 0)),
            pl.BlockSpec((1, N_KV_HEADS, V_ROWS, s), lambda i: (i, 0, 0, 0)),
        ],
        out_specs=pl.BlockSpec((1, Q_WIDTH, s), lambda i: (i, 0, 0)),
        out_shape=jax.ShapeDtypeStruct((b, Q_WIDTH, s), BF16),
        scratch_shapes=[pltpu.VMEM((s, width), F32), pltpu.VMEM((s, width), F32),
                        pltpu.VMEM((1, width), F32), pltpu.VMEM((1, width), F32)],
        compiler_params=_cparams(1),
        name="gqa_attention",
    )(qt, k, vt)


def _mlp_block(x, g, w1_ref, w2_ref):
    h = _rms_rows(x, g).astype(BF16)
    a = jnp.maximum(_dot(h, w1_ref[...]), 0.0)
    return x + _dot((a * a).astype(BF16), w2_ref[...])


def _tail0_kernel(fa_ref, att_ref, x_ref, wa_ref, wb_ref, g_ref, w1_ref, w2_ref, gn_ref,
                  o_ref, ut_ref):
    att = lax.dot_general(att_ref[0], wb_ref[...], (((0,), (0,)), ((), ())),
                          preferred_element_type=F32)
    x1 = x_ref[0] + (_dot(fa_ref[0], wa_ref[...]) + att)
    x2 = _mlp_block(x1, g_ref[...], w1_ref, w2_ref)
    o_ref[0] = x2
    ut_ref[...] = _rms_rows(x2, gn_ref[...]).T.astype(BF16)


def _tail0(fa, att, x, w_out, g_mlp, w1, w2, g_next, tm):
    b, s, d = x.shape
    nblk = s // tm
    w = w_out.astype(BF16)
    const = lambda shape: pl.BlockSpec(shape, lambda i, j: (0,) * len(shape),
                                       pipeline_mode=pl.Buffered(1))
    return pl.pallas_call(
        _tail0_kernel,
        grid=(b, nblk),
        in_specs=[
            pl.BlockSpec((1, tm, MIX_A), lambda i, j: (i, j, 0)),
            pl.BlockSpec((1, Q_WIDTH, tm), lambda i, j: (i, 0, j)),
            pl.BlockSpec((1, tm, d), lambda i, j: (i, j, 0)),
            const((MIX_A, d)), const((Q_WIDTH, d)),
            const((1, d)), const((d, D_FF)), const((D_FF, d)), const((1, d)),
        ],
        out_specs=[pl.BlockSpec((1, tm, d), lambda i, j: (i, j, 0)),
                   pl.BlockSpec((d, tm), lambda i, j: (0, i * nblk + j))],
        out_shape=[jax.ShapeDtypeStruct((b, s, d), F32),
                   jax.ShapeDtypeStruct((d, b * s), BF16)],
        compiler_params=_cparams(2),
        name="outproj_mlp",
    )(fa, att, x, w[:MIX_A], w[MIX_A:], g_mlp[None, :], w1.astype(BF16), w2.astype(BF16),
      g_next[None, :])


def _mlp_kernel(x_ref, g_ref, w1_ref, w2_ref, gf_ref, o_ref, *, final_norm):
    y = _mlp_block(x_ref[0], g_ref[...], w1_ref, w2_ref)
    if final_norm:
        y = _rms_rows(y, gf_ref[...])
    o_ref[0] = y


def _mlp(x, g, w1, w2, g_final, tm, final_norm):
    b, s, d = x.shape
    const = lambda shape: pl.BlockSpec(shape, lambda i, j: (0,) * len(shape),
                                       pipeline_mode=pl.Buffered(1))
    return pl.pallas_call(
        functools.partial(_mlp_kernel, final_norm=final_norm),
        grid=(b, s // tm),
        in_specs=[
            pl.BlockSpec((1, tm, d), lambda i, j: (i, j, 0)),
            const((1, d)), const((d, D_FF)), const((D_FF, d)), const((1, d)),
        ],
        out_specs=pl.BlockSpec((1, tm, d), lambda i, j: (i, j, 0)),
        out_shape=jax.ShapeDtypeStruct((b, s, d), F32),
        compiler_params=_cparams(2),
        name="mlp_final" if final_norm else "mlp",
    )(x, g[None, :], w1.astype(BF16), w2.astype(BF16), g_final[None, :])


def _cmul(ar, ai, br, bi):
    return ar * br - ai * bi, ar * bi + ai * br


def _cpow(a, theta, e):
    mag = jnp.exp(e * a)
    ang = e * theta
    return mag * jnp.cos(ang), mag * jnp.sin(ang)


def _ssm_kernel(ut_ref, prow_ref, crow_ref, bcol_ref, shift_ref, y_ref,
                wall_ref, mmat_ref, cmat_ref, yacc_ref, xprev_ref, *, n_chunks):
    P = SSM_STATE
    C = SSM_GROUP_CH
    L = CHUNK
    rows = ut_ref.shape[2]

    prow = prow_ref[0]
    lre, lim = prow[0:1, :], prow[1:2, :]
    dt = jnp.exp(prow[2:3, :])
    a_row, th_row = lre * dt, lim * dt
    lbr, lbi = _cpow(a_row, th_row, 1.0)
    inv = 1.0 / (lre * lre + lim * lim)
    coef_r, coef_i = _cmul(lbr - 1.0, lbi, lre * inv, -lim * inv)
    bbr, bbi = _cmul(coef_r, coef_i, bcol_ref[0, 0].T, bcol_ref[0, 1].T)

    sq = [(lbr, lbi)]
    while len(sq) <= 7:
        sq.append(_cmul(*sq[-1], *sq[-1]))
    s8 = lax.broadcasted_iota(jnp.int32, (8, 2 * P), 0).astype(F32)
    up = _cpow(a_row, th_row, s8)
    dn = _cpow(a_row, th_row, 7.0 - s8)
    for k in range(3, 7):
        hi_up = _cmul(*up, *sq[k])
        hi_dn = _cmul(*dn, *sq[k])
        up = tuple(jnp.concatenate([lo, hi], axis=0) for lo, hi in zip(up, hi_up))
        dn = tuple(jnp.concatenate([hi, lo], axis=0) for lo, hi in zip(dn, hi_dn))
    up1 = _cmul(*up, lbr, lbi)
    dn1 = _cmul(*dn, lbr, lbi)
    fwd_lane = lax.broadcasted_iota(jnp.int32, (L, 2 * P), 1) < P
    row0 = lax.broadcasted_iota(jnp.int32, (L, 2 * P), 0) == 0

    pwr, pwi = jnp.where(fwd_lane, dn[0], up[0]), jnp.where(fwd_lane, dn[1], up[1])
    for c in range(C):
        xr, xi = _cmul(pwr, pwi, bbr[c:c + 1, :], bbi[c:c + 1, :])
        mmat_ref[c * L:(c + 1) * L, C * L:C * L + 2 * P] = xr.astype(BF16)
        mmat_ref[c * L:(c + 1) * L, C * L + 2 * P:] = xi.astype(BF16)

    qwr = jnp.where(fwd_lane, up1[0], dn1[0]).T
    qwi = jnp.where(fwd_lane, up1[1], dn1[1]).T
    ccr, cci = crow_ref[0, 0].T, crow_ref[0, 1].T
    for c in range(C):
        xr, xi = _cmul(qwr, qwi, ccr[:, c:c + 1], cci[:, c:c + 1])
        cmat_ref[:2 * P, c * L:(c + 1) * L] = xr.astype(BF16)
        cmat_ref[2 * P:, c * L:(c + 1) * L] = (-xi).astype(BF16)

    zero = jnp.zeros((L, 2 * P), F32)
    k_lo = (jnp.where(fwd_lane, up[0], jnp.where(row0, 1.0, 0.0)), jnp.where(fwd_lane, up[1], zero))
    k_hi = (jnp.where(fwd_lane, zero, dn1[0]), jnp.where(fwd_lane, zero, dn1[1]))
    kwr = jnp.concatenate([k_lo[0].T, k_hi[0].T], axis=1)
    kwi = jnp.concatenate([k_lo[1].T, k_hi[1].T], axis=1)
    crr, cri = crow_ref[0, 0], crow_ref[0, 1]
    cb_r, cb_i = [], []
    for c in range(C):
        xr, xi = _cmul(crr, cri, bbr[c:c + 1, :], bbi[c:c + 1, :])
        cb_r.append(xr)
        cb_i.append(xi)
    cb_r = jnp.concatenate(cb_r, axis=0)
    cb_i = jnp.concatenate(cb_i, axis=0)
    ktab = _dot3(cb_r, kwr) - _dot3(cb_i, kwi)

    def fine(blk):
        return pltpu.roll(jnp.broadcast_to(ktab[blk:blk + 1, :], (8, 2 * L)), 0, 1,
                          stride=1, stride_axis=0)
    for blk in range(0, C * C, 2):
        wall_ref[blk * 8:blk * 8 + 16, :] = jnp.concatenate(
            [fine(blk), fine(blk + 1)], axis=0).astype(BF16)

    def coarse(hq, carry):
        sh = _dot(wall_ref[...], shift_ref[hq])
        for q2 in range(SHIFTS_PER_DOT // 2):
            lo = 2 * q2 * L
            pair = jnp.concatenate([sh[:, lo:lo + L].reshape(C * C, 8, L),
                                    sh[:, lo + L:lo + 2 * L].reshape(C * C, 8, L)],
                                   axis=1).astype(BF16)
            for cp in range(C):
                r0 = pl.multiple_of(cp * L + 16 * ((SHIFTS_PER_DOT // 2) * hq + q2), 16)
                mmat_ref[pl.ds(r0, 16), :C * L] = jnp.concatenate(
                    [pair[cp * C + c] for c in range(C)], axis=1)
        return carry
    lax.fori_loop(0, L // (8 * SHIFTS_PER_DOT), coarse, 0)

    part_w = (C * L + 4 * P) // 3

    def intra(t, carry):
        u = jnp.concatenate([ut_ref[0, c] for c in range(C)], axis=1)
        off = pl.multiple_of(t * part_w, part_w)
        yacc_ref[:, pl.ds(off, part_w)] = _dot(u, mmat_ref[:, pl.ds(off, part_w)])
        return carry
    lax.fori_loop(0, 3, intra, 0)
    xr = yacc_ref[:, C * L:C * L + 2 * P]
    xi = yacc_ref[:, C * L + 2 * P:]

    k_idx = lax.broadcasted_iota(jnp.int32, (rows, 2 * P), 0) % n_chunks
    is_f = lax.broadcasted_iota(jnp.int32, (rows, 2 * P), 1) < P
    ar, ai = sq[7]

    def shifted(t, step):
        down = pltpu.roll(t, step, 0)
        up = pltpu.roll(t, rows - step, 0)
        return jnp.where(is_f, jnp.where(k_idx >= step, down, 0.0),
                         jnp.where(k_idx < n_chunks - step, up, 0.0))

    step = 1
    while step < n_chunks:
        sr, si = shifted(xr, step), shifted(xi, step)
        pr, pi = _cmul(sr, si, ar, ai)
        xr, xi = xr + pr, xi + pi
        ar, ai = _cmul(ar, ai, ar, ai)
        step *= 2
    xprev_ref[...] = jnp.concatenate([shifted(xr, 1), shifted(xi, 1)], axis=1).astype(BF16)

    half_w = C * L // 2

    def inter(hf, carry):
        off = pl.multiple_of(hf * half_w, half_w)
        y = yacc_ref[:, pl.ds(off, half_w)] + _dot(xprev_ref[...], cmat_ref[:, pl.ds(off, half_w)])
        y_ref[0, pl.ds(hf * (C // 2), C // 2)] = jnp.stack(
            [y[:, c * L:(c + 1) * L] for c in range(C // 2)], axis=0)
        return carry
    lax.fori_loop(0, 2, inter, 0)


def _shift_mats():
    L = CHUNK
    m = np.arange(2 * L)[:, None]
    col = np.arange(SHIFTS_PER_DOT * L)[None, :]
    q, i = col // L, col % L
    mats = [m == (i - 8 * (SHIFTS_PER_DOT * hq + q)) % (2 * L)
            for hq in range(L // (8 * SHIFTS_PER_DOT))]
    return jnp.asarray(np.stack(mats), BF16)


def _ssm(ut, lam_re, lam_im, log_dt, b_re, b_im, c_re, c_im, batch, seq):
    G, C, P, L = SSM_GROUPS, SSM_GROUP_CH, SSM_STATE, CHUNK
    n_chunks = seq // L
    rows = batch * n_chunks
    ut4 = ut.reshape(G, C, rows, L)
    f32 = lambda t: t.astype(F32)
    ldt = jnp.broadcast_to(f32(log_dt)[:, :, None], (2, G, P))
    prow = jnp.stack([f32(lam_re), f32(lam_im), ldt]).transpose(2, 0, 1, 3).reshape(G, 3, 2 * P)
    crow = jnp.stack([f32(c_re), f32(c_im)]).transpose(2, 0, 3, 1, 4).reshape(G, 2, C, 2 * P)
    bcol = jnp.stack([f32(b_re), f32(b_im)]).transpose(2, 0, 1, 3, 4).reshape(G, 2, 2 * P, C)
    return pl.pallas_call(
        functools.partial(_ssm_kernel, n_chunks=n_chunks),
        grid=(G,),
        in_specs=[
            pl.BlockSpec((1, C, rows, L), lambda g: (g, 0, 0, 0)),
            pl.BlockSpec((1, 3, 2 * P), lambda g: (g, 0, 0)),
            pl.BlockSpec((1, 2, C, 2 * P), lambda g: (g, 0, 0, 0)),
            pl.BlockSpec((1, 2, 2 * P, C), lambda g: (g, 0, 0, 0)),
            pl.BlockSpec((L // (8 * SHIFTS_PER_DOT), 2 * L, SHIFTS_PER_DOT * L), lambda g: (0, 0, 0),
                         pipeline_mode=pl.Buffered(1)),
        ],
        out_specs=pl.BlockSpec((1, C, rows, L), lambda g: (g, 0, 0, 0)),
        out_shape=jax.ShapeDtypeStruct((G, C, rows, L), F32),
        scratch_shapes=[
            pltpu.VMEM((C * C * 8, 2 * L), BF16),
            pltpu.VMEM((C * L, C * L + 4 * P), BF16),
            pltpu.VMEM((4 * P, C * L), BF16),
            pltpu.VMEM((rows, C * L + 4 * P), F32),
            pltpu.VMEM((rows, 4 * P), BF16),
        ],
        compiler_params=_cparams(1),
        name="s5_scan",
    )(ut4, prow, crow, bcol, _shift_mats())


def _gelu_tanh(y):
    return 0.5 * y * (1.0 + jnp.tanh(math.sqrt(2.0 / math.pi) * (y + 0.044715 * (y * y * y))))


def _gate_kernel(yt_ref, x_ref, g_ref, dsk_ref, wg_ref, bg_ref, o_ref):
    x = x_ref[0]
    u = _rms_rows(x, g_ref[...])
    n_rows = yt_ref.shape[2]
    y_rct = jnp.swapaxes(yt_ref[...].reshape(D_MODEL, n_rows, CHUNK), 0, 1)
    y = jnp.concatenate([y_rct[r].T for r in range(n_rows)], axis=0) + dsk_ref[...] * u
    gl = _gelu_tanh(y)
    gate = _dot(gl.astype(BF16), wg_ref[...]) + bg_ref[...]
    o_ref[0] = x + gl * (0.5 + 0.5 * jnp.tanh(0.5 * gate))


def _gate(y4, x, g, d_skip, w_gate, b_gate, ts):
    b, s, d = x.shape
    nblk = s // ts
    n_rows = ts // CHUNK
    const = lambda shape: pl.BlockSpec(shape, lambda i, j: (0,) * len(shape))
    return pl.pallas_call(
        _gate_kernel,
        grid=(b, nblk),
        in_specs=[
            pl.BlockSpec((SSM_GROUPS, SSM_GROUP_CH, n_rows, CHUNK), lambda i, j: (0, 0, i * nblk + j, 0)),
            pl.BlockSpec((1, ts, d), lambda i, j: (i, j, 0)),
            const((1, d)), const((1, d)), const((d, d)), const((1, d)),
        ],
        out_specs=pl.BlockSpec((1, ts, d), lambda i, j: (i, j, 0)),
        out_shape=jax.ShapeDtypeStruct((b, s, d), F32),
        compiler_params=_cparams(2),
        name="s5_gate",
    )(y4, x, g[None, :], d_skip.astype(F32)[None, :], w_gate.astype(BF16), b_gate.astype(F32)[None, :])


def _pick(n, pref):
    t = min(n, pref)
    while n % t:
        t //= 2
    return t


def kernel(x, norm_mix, norm_mlp, mlp_w1, mlp_w2, w_in, w_fnet, q_norm, k_norm, w_out, lam_re, lam_im, log_dt, b_re, b_im, c_re, c_im, d_skip, w_gate, b_gate, final_norm):
    b, s, d = x.shape
    assert d == D_MODEL and s % CHUNK == 0 and s % GRID_W == 0
    tm = _pick(s, 512)
    gmat = _fourier_fold(w_fnet[0])
    u, v, q, k, vv = _inproj(x, norm_mix[0], w_in[0], gmat, q_norm[0], k_norm[0], _pick(s, 1024))
    fa = _seqdft(u, v)
    att = _attention(q, k, vv, _pick(s, 128))
    x, ut = _tail0(fa, att, x, w_out[0], norm_mlp[0], mlp_w1[0], mlp_w2[0], norm_mix[1], tm)
    yt = _ssm(ut, lam_re[0], lam_im[0], log_dt[0], b_re[0], b_im[0], c_re[0], c_im[0], b, s)
    x = _gate(yt, x, norm_mix[1], d_skip[0], w_gate[0], b_gate[0], _pick(s, 1024))
    return _mlp(x, norm_mlp[1], mlp_w1[1], mlp_w2[1], final_norm, tm, True)
```

```python
import functools
import math

import numpy as np
import jax
import jax.numpy as jnp
from jax import lax
from jax.experimental import pallas as pl
from jax.experimental.pallas import tpu as pltpu

F32 = jnp.float32
BF16 = jnp.bfloat16

EPS = 1e-6
D_MODEL = 1024
MIX_A = 512
FNET_HEAD_DIM = 64
FNET_HEADS = MIX_A // FNET_HEAD_DIM
HEAD_DIM = 64
N_Q_HEADS = 8
N_KV_HEADS = 2
GQA_GROUP = N_Q_HEADS // N_KV_HEADS
Q_WIDTH = N_Q_HEADS * HEAD_DIM
KV_WIDTH = N_KV_HEADS * HEAD_DIM
IN_WIDTH = MIX_A + Q_WIDTH + 2 * KV_WIDTH
GRID_W = 64
ROPE_THETA = 10000.0
ROPE_AXIS_DIM = HEAD_DIM // 2
SSM_GROUP_CH = 16
SSM_GROUPS = D_MODEL // SSM_GROUP_CH
SSM_STATE = 64
D_FF = 4 * D_MODEL
CHUNK = 128
LANES = 128
LOG2_E = 1.4426950408889634
V_ROWS = HEAD_DIM + 16
ATTN_KV_CHUNK = 256
VMEM_LIMIT = 56 * 1024 * 1024


def _cparams(n_grid_dims):
    return pltpu.CompilerParams(
        dimension_semantics=("arbitrary",) * n_grid_dims, vmem_limit_bytes=VMEM_LIMIT)


def _dot(a, b):
    return jnp.dot(a, b, preferred_element_type=F32)


def _split_bf16(a):
    hi = a.astype(BF16)
    lo = (a - hi.astype(F32)).astype(BF16)
    return hi, lo


def _dot3(a, b):
    a_hi, a_lo = _split_bf16(a)
    b_hi, b_lo = _split_bf16(b)
    return _dot(a_hi, b_hi) + (_dot(a_hi, b_lo) + _dot(a_lo, b_hi))


def _rms_rows(x, g):
    ms = jnp.mean(x * x, axis=-1, keepdims=True)
    return x * lax.rsqrt(ms + EPS) * g


def _fold_kernel(cc_ref, sc_ref, w_ref, g_ref):
    w = w_ref[...]
    g_ref[:, :MIX_A] = _dot3(cc_ref[...], w).astype(BF16)
    g_ref[:, MIX_A:] = (-_dot3(sc_ref[...], w)).astype(BF16)


def _fourier_fold(w_fnet):
    n = np.arange(FNET_HEAD_DIM)
    ang = 2.0 * np.pi * np.outer(n, n) / FNET_HEAD_DIM
    scale = FNET_HEAD_DIM ** -0.5
    eye = np.eye(FNET_HEADS)
    cc = jnp.asarray(np.kron(eye, np.cos(ang) * scale), F32)
    sc = jnp.asarray(np.kron(eye, np.sin(ang) * scale), F32)
    eye_j = jnp.eye(FNET_HEADS, dtype=F32)
    w_bd = (eye_j[:, None, :, None] * w_fnet.astype(F32)[:, :, None, :]).reshape(MIX_A, MIX_A)
    return pl.pallas_call(
        _fold_kernel,
        out_shape=jax.ShapeDtypeStruct((MIX_A, 2 * MIX_A), BF16),
        name="fourier_fold",
    )(cc, sc, w_bd)


INPROJ_SPLIT = 4


def _head_norm(t, ones, gain):
    hi, lo = _split_bf16(t * t)
    ss = _dot(hi, ones) + _dot(lo, ones)
    return t * lax.rsqrt(ss * (1.0 / HEAD_DIM) + EPS) * gain


def _rope(t, cos, sin):
    width = t.shape[1]
    reps = width // LANES
    c = jnp.concatenate([cos] * reps, axis=1) if reps > 1 else cos
    s = jnp.concatenate([sin] * reps, axis=1) if reps > 1 else sin
    lane = lax.broadcasted_iota(jnp.int32, t.shape, 1)
    first_half = (lane % ROPE_AXIS_DIM) < (ROPE_AXIS_DIM // 2)
    partner = jnp.where(first_half,
                        pltpu.roll(t, width - ROPE_AXIS_DIM // 2, 1),
                        pltpu.roll(t, ROPE_AXIS_DIM // 2, 1))
    return t * c + partner * s


def _inproj_kernel(x_ref, g_ref, win_ref, gmat_ref, ones_ref, qg_ref, kg_ref,
                   cq_ref, sq_ref, ck_ref, sk_ref,
                   u_ref, v_ref, q_ref, k_ref, vv_ref):
    tm = x_ref.shape[1]
    ones = ones_ref[...]
    sub = tm // INPROJ_SPLIT
    rows = [slice(r * sub, (r + 1) * sub) for r in range(INPROJ_SPLIT)]

    def project(rs):
        h = _rms_rows(x_ref[0, rs, :], g_ref[...]).astype(BF16)
        return _dot(h, win_ref[...])

    def finish(rs, z):
        uv = _dot(z[:, :MIX_A].astype(BF16), gmat_ref[...])
        u_ref[0, rs, :] = uv[:, :MIX_A].astype(BF16)
        v_ref[0, rs, :] = uv[:, MIX_A:].astype(BF16)
        q = z[:, MIX_A:MIX_A + Q_WIDTH]
        k = z[:, MIX_A + Q_WIDTH:MIX_A + Q_WIDTH + KV_WIDTH]
        v = z[:, MIX_A + Q_WIDTH + KV_WIDTH:]
        q = _rope(_head_norm(q, ones, qg_ref[...]), cq_ref[rs, :], sq_ref[rs, :])
        k = _rope(_head_norm(k, ones[:KV_WIDTH, :KV_WIDTH], kg_ref[...]), ck_ref[rs, :], sk_ref[rs, :])
        q_ref[0, :, :, rs] = q.T.reshape(N_Q_HEADS, HEAD_DIM, sub).astype(BF16)
        for hh in range(N_KV_HEADS):
            k_ref[0, hh, rs, :] = k[:, hh * HEAD_DIM:(hh + 1) * HEAD_DIM].astype(BF16)
        vv_ref[0, :, :HEAD_DIM, rs] = v.T.reshape(N_KV_HEADS, HEAD_DIM, sub).astype(BF16)

    z_prev = project(rows[0])
    for r in range(1, INPROJ_SPLIT):
        z_next = project(rows[r])
        finish(rows[r - 1], z_prev)
        z_prev = z_next
    finish(rows[-1], z_prev)
    vv_ref[0, :, HEAD_DIM:, :] = jnp.ones((N_KV_HEADS, V_ROWS - HEAD_DIM, tm), BF16)


def _rope_tables(seq_len):
    t = np.arange(seq_len)
    inv_freq = ROPE_THETA ** (-np.arange(0, ROPE_AXIS_DIM, 2, dtype=np.float64) / ROPE_AXIS_DIM)
    ar = (t // GRID_W)[:, None] * inv_freq
    ac = (t % GRID_W)[:, None] * inv_freq
    cos = np.concatenate([np.cos(ar), np.cos(ar), np.cos(ac), np.cos(ac)], axis=1)
    sin = np.concatenate([-np.sin(ar), np.sin(ar), -np.sin(ac), np.sin(ac)], axis=1)
    cos = np.concatenate([cos, cos], axis=1)
    sin = np.concatenate([sin, sin], axis=1)
    scale = HEAD_DIM ** -0.5 * LOG2_E
    return tuple(jnp.asarray(a, F32) for a in (cos * scale, sin * scale, cos, sin))


def _inproj(x, g, w_in, gmat, q_norm, k_norm, tm):
    b, s, d = x.shape
    cq, sq, ck, sk = _rope_tables(s)
    head = np.arange(Q_WIDTH) // HEAD_DIM
    ones = jnp.asarray(head[:, None] == head[None, :], BF16)
    qg = jnp.tile(q_norm.astype(F32), N_Q_HEADS)[None, :]
    kg = jnp.tile(k_norm.astype(F32), N_KV_HEADS)[None, :]
    const = lambda shape: pl.BlockSpec(shape, lambda i, j: (0,) * len(shape))
    tab = pl.BlockSpec((tm, LANES), lambda i, j: (j, 0))
    return pl.pallas_call(
        _inproj_kernel,
        grid=(b, s // tm),
        in_specs=[
            pl.BlockSpec((1, tm, d), lambda i, j: (i, j, 0)),
            const((1, d)), const((d, IN_WIDTH)), const((MIX_A, 2 * MIX_A)), const((Q_WIDTH, Q_WIDTH)),
            const((1, Q_WIDTH)), const((1, KV_WIDTH)), tab, tab, tab, tab,
        ],
        out_specs=[
            pl.BlockSpec((1, tm, MIX_A), lambda i, j: (i, j, 0)),
            pl.BlockSpec((1, tm, MIX_A), lambda i, j: (i, j, 0)),
            pl.BlockSpec((1, N_Q_HEADS, HEAD_DIM, tm), lambda i, j: (i, 0, 0, j)),
            pl.BlockSpec((1, N_KV_HEADS, tm, HEAD_DIM), lambda i, j: (i, 0, j, 0)),
            pl.BlockSpec((1, N_KV_HEADS, V_ROWS, tm), lambda i, j: (i, 0, 0, j)),
        ],
        out_shape=[
            jax.ShapeDtypeStruct((b, s, MIX_A), BF16),
            jax.ShapeDtypeStruct((b, s, MIX_A), BF16),
            jax.ShapeDtypeStruct((b, N_Q_HEADS, HEAD_DIM, s), BF16),
            jax.ShapeDtypeStruct((b, N_KV_HEADS, s, HEAD_DIM), BF16),
            jax.ShapeDtypeStruct((b, N_KV_HEADS, V_ROWS, s), BF16),
        ],
        compiler_params=_cparams(2),
        name="inproj",
    )(x, g[None, :], w_in.astype(BF16), gmat, ones, qg, kg, cq, sq, ck, sk)


def _dft_tables_kernel(bc_ref, bs_ref, rot_ref, c_ref, s_ref):
    re, im = bc_ref[...], bs_ref[...]
    c_ref[:, :LANES] = re.astype(BF16)
    s_ref[:, :LANES] = im.astype(BF16)
    w, stage = LANES, 0
    while w < c_ref.shape[1]:
        cr = rot_ref[:, 2 * stage:2 * stage + 1]
        sr = rot_ref[:, 2 * stage + 1:2 * stage + 2]
        nre, nim = _cmul(re, im, cr, sr)
        c_ref[:, w:2 * w] = nre.astype(BF16)
        s_ref[:, w:2 * w] = nim.astype(BF16)
        re = jnp.concatenate([re, nre], axis=1)
        im = jnp.concatenate([im, nim], axis=1)
        w, stage = 2 * w, stage + 1


def _dft_tables(seq, tr):
    h = seq // 2
    scale = seq ** -0.5
    j = np.arange(h)[:, None]
    k = np.arange(LANES)[None, :]
    unit = 2.0 * math.pi / seq
    ang = ((j * k) % seq) * unit
    n_stage = int(round(math.log2(h // LANES)))
    widths = np.asarray([LANES << st for st in range(n_stage)] + [0] * (4 - n_stage))
    rang = ((j * widths[None, :]) % seq) * unit
    rot = jnp.asarray(np.stack([np.cos(rang), np.sin(rang)], axis=-1).reshape(h, 8), F32)
    return pl.pallas_call(
        _dft_tables_kernel,
        grid=(h // tr,),
        in_specs=[pl.BlockSpec((tr, LANES), lambda i: (i, 0)),
                  pl.BlockSpec((tr, LANES), lambda i: (i, 0)),
                  pl.BlockSpec((tr, 8), lambda i: (i, 0))],
        out_specs=[pl.BlockSpec((tr, h), lambda i: (i, 0)), pl.BlockSpec((tr, h), lambda i: (i, 0))],
        out_shape=[jax.ShapeDtypeStruct((h, h), BF16), jax.ShapeDtypeStruct((h, h), BF16)],
        compiler_params=_cparams(1),
        name="dft_tables",
    )(jnp.asarray(np.cos(ang) * scale, F32), jnp.asarray(np.sin(ang) * scale, F32), rot)


def _seqdft_kernel(ct_ref, st_ref, u_ref, v_ref, o_ref, ue_ref, vo_ref, d_ref, *, tm, blk):
    t = pl.program_id(1)
    seq = u_ref.shape[1]
    h = seq // 2
    width = u_ref.shape[2]
    scale = seq ** -0.5
    rr = lax.broadcasted_iota(jnp.int32, (blk, blk), 0)
    cc = lax.broadcasted_iota(jnp.int32, (blk, blk), 1)
    flip = jnp.where(rr + cc == blk - 1, 1.0, 0.0).astype(BF16)
    row = lax.broadcasted_iota(jnp.int32, (h, width), 0)

    @pl.when(t == 0)
    def _fold_halves():
        for src_ref, dst_ref, sign in ((u_ref, ue_ref, 1.0), (v_ref, vo_ref, -1.0)):
            rev = jnp.concatenate(
                [_dot(flip, src_ref[0, seq - blk * (a + 1):seq - blk * a, :]) for a in range(h // blk)],
                axis=0)
            mirror = jnp.where(row == 0, 0.0, pltpu.roll(rev, 1, 0))
            dst_ref[...] = (src_ref[0, :h, :].astype(F32) + sign * mirror).astype(BF16)

    ue = ue_ref[...]
    x_mid = u_ref[0, h:h + 1, :].astype(F32) * scale
    p = _dot(ct_ref[...], ue)
    q = _dot(st_ref[...], vo_ref[...])
    j = t * tm + lax.broadcasted_iota(jnp.int32, (tm, 1), 0)
    p = p + (1 - 2 * (j & 1)).astype(F32) * x_mid
    off = pl.multiple_of(t * tm, tm)
    o_ref[0, pl.ds(off, tm), :] = (p + q).astype(BF16)
    d_ref[pl.ds(off, tm), :] = p - q

    @pl.when(t == pl.num_programs(1) - 1)
    def _mirror_half():
        kk = lax.broadcasted_iota(jnp.int32, (8, h), 1)
        alt = ((1 - 2 * (kk & 1)).astype(F32) * scale).astype(BF16)
        p_mid = _dot(alt, ue)[0:1, :] + x_mid
        d_up = jnp.where(row == h - 1, p_mid, pltpu.roll(d_ref[...], h - 1, 0)).astype(BF16)
        for a in range(h // blk):
            o_ref[0, h + blk * a:h + blk * (a + 1), :] = _dot(
                flip, d_up[h - blk * (a + 1):h - blk * a, :]).astype(BF16)


def _seqdft(u, v):
    b, s, w = u.shape
    h = s // 2
    tm = _pick(h, 512)
    blk = _pick(h, 256)
    ct, st = _dft_tables(s, _pick(h, 256))
    return pl.pallas_call(
        functools.partial(_seqdft_kernel, tm=tm, blk=blk),
        grid=(b, h // tm),
        in_specs=[
            pl.BlockSpec((tm, h), lambda i, j: (j, 0)),
            pl.BlockSpec((tm, h), lambda i, j: (j, 0)),
            pl.BlockSpec((1, s, w), lambda i, j: (i, 0, 0)),
            pl.BlockSpec((1, s, w), lambda i, j: (i, 0, 0)),
        ],
        out_specs=pl.BlockSpec((1, s, w), lambda i, j: (i, 0, 0)),
        out_shape=jax.ShapeDtypeStruct((b, s, w), BF16),
        scratch_shapes=[pltpu.VMEM((h, w), BF16), pltpu.VMEM((h, w), BF16), pltpu.VMEM((h, w), F32)],
        compiler_params=_cparams(2),
        name="seqdft",
    )(ct, st, u, v)


def _attn_kernel(qt_ref, k_ref, vt_ref, o_ref, sa_ref, sb_ref, ma_ref, mb_ref, *, tq):
    seq = k_ref.shape[2]
    per_head = seq // tq
    nblk = N_KV_HEADS * per_head
    ck = min(seq, ATTN_KV_CHUNK)
    chunks = [slice(c * ck, (c + 1) * ck) for c in range(seq // ck)]

    def locate(i):
        head = i // per_head
        return head, pl.multiple_of((i - head * per_head) * tq, tq)

    def query_block(i):
        head, off = locate(i)
        return jnp.concatenate(
            [qt_ref[0, head * GQA_GROUP + g, :, pl.ds(off, tq)] for g in range(GQA_GROUP)], axis=1)

    def score_chunk(i, qt, rows, s_ref, m):
        s = _dot(k_ref[0, locate(i)[0], rows, :], qt)
        s_ref[rows, :] = s
        mc = jnp.max(s, axis=0, keepdims=True)
        return mc if m is None else jnp.maximum(m, mc)

    def value_chunk(i, rows, s_ref, m, acc):
        p = jnp.exp2(s_ref[rows, :] - m).astype(BF16)
        part = _dot(vt_ref[0, locate(i)[0], :, rows], p)
        return part if acc is None else acc + part

    def write_out(i, acc):
        o = acc[:HEAD_DIM] * (1.0 / acc[HEAD_DIM:HEAD_DIM + 1])
        head, off = locate(i)
        for g in range(GQA_GROUP):
            r0 = pl.multiple_of((head * GQA_GROUP + g) * HEAD_DIM, HEAD_DIM)
            o_ref[0, pl.ds(r0, HEAD_DIM), pl.ds(off, tq)] = o[:, g * tq:(g + 1) * tq].astype(BF16)

    def step(i_next, s_next, m_next, i_cur, s_cur, m_cur):
        qt = None if i_next is None else query_block(i_next)
        m_old = None if i_cur is None else m_cur[...]
        m_new, acc = None, None
        for rows in chunks:
            if i_next is not None:
                m_new = score_chunk(i_next, qt, rows, s_next, m_new)
            if i_cur is not None:
                acc = value_chunk(i_cur, rows, s_cur, m_old, acc)
        if i_next is not None:
            m_next[...] = m_new
        if i_cur is not None:
            write_out(i_cur, acc)

    step(0, sa_ref, ma_ref, None, None, None)

    def pair(j, carry):
        step(2 * j + 1, sb_ref, mb_ref, 2 * j, sa_ref, ma_ref)
        step(2 * j + 2, sa_ref, ma_ref, 2 * j + 1, sb_ref, mb_ref)
        return carry

    lax.fori_loop(0, nblk // 2 - 1, pair, 0)
    step(nblk - 1, sb_ref, mb_ref, nblk - 2, sa_ref, ma_ref)
    step(None, None, None, nblk - 1, sb_ref, mb_ref)


def _attention(qt, k, vt, tq):
    b, _, _, s = qt.shape
    assert (s // tq) % 2 == 0
    width = GQA_GROUP * tq
    return pl.pallas_call(
        functools.partial(_attn_kernel, tq=tq),
        grid=(b,),
        in_specs=[
            pl.BlockSpec((1, N_Q_HEADS, HEAD_DIM, s), lambda i: (i, 0, 0, 0)),
            pl.BlockSpec((1, N_KV_HEADS, s, HEAD_DIM), lambda i: (i, 0, 0, 0)),
            pl.BlockSpec((1, N_KV_HEADS, V_ROWS, s), lambda i: (i, 0, 0, 0)),
        ],
        out_specs=pl.BlockSpec((1, Q_WIDTH, s), lambda i: (i, 0, 0)),
        out_shape=jax.ShapeDtypeStruct((b, Q_WIDTH, s), BF16),
        scratch_shapes=[pltpu.VMEM((s, width), F32), pltpu.VMEM((s, width), F32),
                        pltpu.VMEM((1, width), F32), pltpu.VMEM((1, width), F32)],
        compiler_params=_cparams(1),
        name="gqa_attention",
    )(qt, k, vt)


def _mlp_block(x, g, w1_ref, w2_ref):
    h = _rms_rows(x, g).astype(BF16)
    a = jnp.maximum(_dot(h, w1_ref[...]), 0.0)
    return x + _dot((a * a).astype(BF16), w2_ref[...])


def _tail0_kernel(fa_ref, att_ref, x_ref, wa_ref, wb_ref, g_ref, w1_ref, w2_ref, gn_ref,
                  o_ref, ut_ref):
    att = lax.dot_general(att_ref[0], wb_ref[...], (((0,), (0,)), ((), ())),
                          preferred_element_type=F32)
    x1 = x_ref[0] + (_dot(fa_ref[0], wa_ref[...]) + att)
    x2 = _mlp_block(x1, g_ref[...], w1_ref, w2_ref)
    o_ref[0] = x2
    ut_ref[...] = _rms_rows(x2, gn_ref[...]).T.astype(BF16)


def _tail0(fa, att, x, w_out, g_mlp, w1, w2, g_next, tm):
    b, s, d = x.shape
    nblk = s // tm
    w = w_out.astype(BF16)
    const = lambda shape: pl.BlockSpec(shape, lambda i, j: (0,) * len(shape),
                                       pipeline_mode=pl.Buffered(1))
    return pl.pallas_call(
        _tail0_kernel,
        grid=(b, nblk),
        in_specs=[
            pl.BlockSpec((1, tm, MIX_A), lambda i, j: (i, j, 0)),
            pl.BlockSpec((1, Q_WIDTH, tm), lambda i, j: (i, 0, j)),
            pl.BlockSpec((1, tm, d), lambda i, j: (i, j, 0)),
            const((MIX_A, d)), const((Q_WIDTH, d)),
            const((1, d)), const((d, D_FF)), const((D_FF, d)), const((1, d)),
        ],
        out_specs=[pl.BlockSpec((1, tm, d), lambda i, j: (i, j, 0)),
                   pl.BlockSpec((d, tm), lambda i, j: (0, i * nblk + j))],
        out_shape=[jax.ShapeDtypeStruct((b, s, d), F32),
                   jax.ShapeDtypeStruct((d, b * s), BF16)],
        compiler_params=_cparams(2),
        name="outproj_mlp",
    )(fa, att, x, w[:MIX_A], w[MIX_A:], g_mlp[None, :], w1.astype(BF16), w2.astype(BF16),
      g_next[None, :])


def _mlp_kernel(x_ref, g_ref, w1_ref, w2_ref, gf_ref, o_ref, *, final_norm):
    y = _mlp_block(x_ref[0], g_ref[...], w1_ref, w2_ref)
    if final_norm:
        y = _rms_rows(y, gf_ref[...])
    o_ref[0] = y


def _mlp(x, g, w1, w2, g_final, tm, final_norm):
    b, s, d = x.shape
    const = lambda shape: pl.BlockSpec(shape, lambda i, j: (0,) * len(shape),
                                       pipeline_mode=pl.Buffered(1))
    return pl.pallas_call(
        functools.partial(_mlp_kernel, final_norm=final_norm),
        grid=(b, s // tm),
        in_specs=[
            pl.BlockSpec((1, tm, d), lambda i, j: (i, j, 0)),
            const((1, d)), const((d, D_FF)), const((D_FF, d)), const((1, d)),
        ],
        out_specs=pl.BlockSpec((1, tm, d), lambda i, j: (i, j, 0)),
        out_shape=jax.ShapeDtypeStruct((b, s, d), F32),
        compiler_params=_cparams(2),
        name="mlp_final" if final_norm else "mlp",
    )(x, g[None, :], w1.astype(BF16), w2.astype(BF16), g_final[None, :])


def _cmul(ar, ai, br, bi):
    return ar * br - ai * bi, ar * bi + ai * br


def _cpow(a, theta, e):
    mag = jnp.exp(e * a)
    ang = e * theta
    return mag * jnp.cos(ang), mag * jnp.sin(ang)


def _ssm_kernel(ut_ref, prow_ref, crow_ref, bcol_ref, shift_ref, y_ref,
                wall_ref, mmat_ref, cmat_ref, yacc_ref, xprev_ref, *, n_chunks):
    P = SSM_STATE
    C = SSM_GROUP_CH
    L = CHUNK
    rows = ut_ref.shape[2]

    prow = prow_ref[0]
    lre, lim = prow[0:1, :], prow[1:2, :]
    dt = jnp.exp(prow[2:3, :])
    a_row, th_row = lre * dt, lim * dt
    lbr, lbi = _cpow(a_row, th_row, 1.0)
    inv = 1.0 / (lre * lre + lim * lim)
    coef_r, coef_i = _cmul(lbr - 1.0, lbi, lre * inv, -lim * inv)
    bbr, bbi = _cmul(coef_r, coef_i, bcol_ref[0, 0].T, bcol_ref[0, 1].T)

    sq = [(lbr, lbi)]
    while len(sq) <= 7:
        sq.append(_cmul(*sq[-1], *sq[-1]))
    s8 = lax.broadcasted_iota(jnp.int32, (8, 2 * P), 0).astype(F32)
    up = _cpow(a_row, th_row, s8)
    dn = _cpow(a_row, th_row, 7.0 - s8)
    for k in range(3, 7):
        hi_up = _cmul(*up, *sq[k])
        hi_dn = _cmul(*dn, *sq[k])
        up = tuple(jnp.concatenate([lo, hi], axis=0) for lo, hi in zip(up, hi_up))
        dn = tuple(jnp.concatenate([hi, lo], axis=0) for lo, hi in zip(dn, hi_dn))
    up1 = _cmul(*up, lbr, lbi)
    dn1 = _cmul(*dn, lbr, lbi)
    fwd_lane = lax.broadcasted_iota(jnp.int32, (L, 2 * P), 1) < P
    row0 = lax.broadcasted_iota(jnp.int32, (L, 2 * P), 0) == 0

    pwr, pwi = jnp.where(fwd_lane, dn[0], up[0]), jnp.where(fwd_lane, dn[1], up[1])
    for c in range(C):
        xr, xi = _cmul(pwr, pwi, bbr[c:c + 1, :], bbi[c:c + 1, :])
        mmat_ref[c * L:(c + 1) * L, C * L:C * L + 2 * P] = xr.astype(BF16)
        mmat_ref[c * L:(c + 1) * L, C * L + 2 * P:] = xi.astype(BF16)

    qwr = jnp.where(fwd_lane, up1[0], dn1[0]).T
    qwi = jnp.where(fwd_lane, up1[1], dn1[1]).T
    ccr, cci = crow_ref[0, 0].T, crow_ref[0, 1].T
    for c in range(C):
        xr, xi = _cmul(qwr, qwi, ccr[:, c:c + 1], cci[:, c:c + 1])
        cmat_ref[:2 * P, c * L:(c + 1) * L] = xr.astype(BF16)
        cmat_ref[2 * P:, c * L:(c + 1) * L] = (-xi).astype(BF16)

    zero = jnp.zeros((L, 2 * P), F32)
    k_lo = (jnp.where(fwd_lane, up[0], jnp.where(row0, 1.0, 0.0)), jnp.where(fwd_lane, up[1], zero))
    k_hi = (jnp.where(fwd_lane, zero, dn1[0]), jnp.where(fwd_lane, zero, dn1[1]))
    kwr = jnp.concatenate([k_lo[0].T, k_hi[0].T], axis=1)
    kwi = jnp.concatenate([k_lo[1].T, k_hi[1].T], axis=1)
    crr, cri = crow_ref[0, 0], crow_ref[0, 1]
    cb_r, cb_i = [], []
    for c in range(C):
        xr, xi = _cmul(crr, cri, bbr[c:c + 1, :], bbi[c:c + 1, :])
        cb_r.append(xr)
        cb_i.append(xi)
    cb_r = jnp.concatenate(cb_r, axis=0)
    cb_i = jnp.concatenate(cb_i, axis=0)
    ktab = _dot3(cb_r, kwr) - _dot3(cb_i, kwi)

    def fine(blk):
        return pltpu.roll(jnp.broadcast_to(ktab[blk:blk + 1, :], (8, 2 * L)), 0, 1,
                          stride=1, stride_axis=0)
    for blk in range(0, C * C, 2):
        wall_ref[blk * 8:blk * 8 + 16, :] = jnp.concatenate(
            [fine(blk), fine(blk + 1)], axis=0).astype(BF16)

    own_region = pl.when(pl.program_id(0) >= 0)

    @own_region
    def _coarse():
        sh = _dot(wall_ref[...], shift_ref[...])
        for hh in range(L // 16):
            lo = 2 * hh * L
            pair = jnp.concatenate([sh[:, lo:lo + L].reshape(C * C, 8, L),
                                    sh[:, lo + L:lo + 2 * L].reshape(C * C, 8, L)],
                                   axis=1).astype(BF16)
            for cp in range(C):
                mmat_ref[cp * L + 16 * hh:cp * L + 16 * (hh + 1), :C * L] = jnp.concatenate(
                    [pair[cp * C + c] for c in range(C)], axis=1)


    @own_region
    def _intra():
        u = jnp.concatenate([ut_ref[0, c] for c in range(C)], axis=1)
        yacc_ref[...] = _dot(u, mmat_ref[...])
    xr = yacc_ref[:, C * L:C * L + 2 * P]
    xi = yacc_ref[:, C * L + 2 * P:]

    k_idx = lax.broadcasted_iota(jnp.int32, (rows, 2 * P), 0) % n_chunks
    is_f = lax.broadcasted_iota(jnp.int32, (rows, 2 * P), 1) < P
    ar, ai = sq[7]

    def shifted(t, step):
        down = pltpu.roll(t, step, 0)
        up = pltpu.roll(t, rows - step, 0)
        return jnp.where(is_f, jnp.where(k_idx >= step, down, 0.0),
                         jnp.where(k_idx < n_chunks - step, up, 0.0))

    step = 1
    while step < n_chunks:
        sr, si = shifted(xr, step), shifted(xi, step)
        pr, pi = _cmul(sr, si, ar, ai)
        xr, xi = xr + pr, xi + pi
        ar, ai = _cmul(ar, ai, ar, ai)
        step *= 2
    xprev_ref[...] = jnp.concatenate([shifted(xr, 1), shifted(xi, 1)], axis=1).astype(BF16)

    @own_region
    def _inter():
        y = yacc_ref[:, :C * L] + _dot(xprev_ref[...], cmat_ref[...])
        for c in range(C):
            y_ref[0, c] = y[:, c * L:(c + 1) * L]


def _shift_mats():
    L = CHUNK
    m = np.arange(2 * L)[:, None]
    col = np.arange((L // 8) * L)[None, :]
    h, i = col // L, col % L
    return jnp.asarray(m == (i - 8 * h) % (2 * L), BF16)


def _ssm(ut, lam_re, lam_im, log_dt, b_re, b_im, c_re, c_im, batch, seq):
    G, C, P, L = SSM_GROUPS, SSM_GROUP_CH, SSM_STATE, CHUNK
    n_chunks = seq // L
    rows = batch * n_chunks
    ut4 = ut.reshape(G, C, rows, L)
    f32 = lambda t: t.astype(F32)
    ldt = jnp.broadcast_to(f32(log_dt)[:, :, None], (2, G, P))
    prow = jnp.stack([f32(lam_re), f32(lam_im), ldt]).transpose(2, 0, 1, 3).reshape(G, 3, 2 * P)
    crow = jnp.stack([f32(c_re), f32(c_im)]).transpose(2, 0, 3, 1, 4).reshape(G, 2, C, 2 * P)
    bcol = jnp.stack([f32(b_re), f32(b_im)]).transpose(2, 0, 1, 3, 4).reshape(G, 2, 2 * P, C)
    return pl.pallas_call(
        functools.partial(_ssm_kernel, n_chunks=n_chunks),
        grid=(G,),
        in_specs=[
            pl.BlockSpec((1, C, rows, L), lambda g: (g, 0, 0, 0)),
            pl.BlockSpec((1, 3, 2 * P), lambda g: (g, 0, 0)),
            pl.BlockSpec((1, 2, C, 2 * P), lambda g: (g, 0, 0, 0)),
            pl.BlockSpec((1, 2, 2 * P, C), lambda g: (g, 0, 0, 0)),
            pl.BlockSpec((2 * L, (L // 8) * L), lambda g: (0, 0), pipeline_mode=pl.Buffered(1)),
        ],
        out_specs=pl.BlockSpec((1, C, rows, L), lambda g: (g, 0, 0, 0)),
        out_shape=jax.ShapeDtypeStruct((G, C, rows, L), F32),
        scratch_shapes=[
            pltpu.VMEM((C * C * 8, 2 * L), BF16),
            pltpu.VMEM((C * L, C * L + 4 * P), BF16),
            pltpu.VMEM((4 * P, C * L), BF16),
            pltpu.VMEM((rows, C * L + 4 * P), F32),
            pltpu.VMEM((rows, 4 * P), BF16),
        ],
        compiler_params=_cparams(1),
        name="s5_scan",
    )(ut4, prow, crow, bcol, _shift_mats())


def _gelu_tanh(y):
    return 0.5 * y * (1.0 + jnp.tanh(math.sqrt(2.0 / math.pi) * (y + 0.044715 * (y * y * y))))


def _gate_kernel(yt_ref, x_ref, g_ref, dsk_ref, wg_ref, bg_ref, o_ref):
    x = x_ref[0]
    u = _rms_rows(x, g_ref[...])
    n_rows = yt_ref.shape[2]
    y_rct = jnp.swapaxes(yt_ref[...].reshape(D_MODEL, n_rows, CHUNK), 0, 1)
    y = jnp.concatenate([y_rct[r].T for r in range(n_rows)], axis=0) + dsk_ref[...] * u
    gl = _gelu_tanh(y)
    gate = _dot(gl.astype(BF16), wg_ref[...]) + bg_ref[...]
    o_ref[0] = x + gl * (0.5 + 0.5 * jnp.tanh(0.5 * gate))


def _gate(y4, x, g, d_skip, w_gate, b_gate, ts):
    b, s, d = x.shape
    nblk = s // ts
    n_rows = ts // CHUNK
    const = lambda shape: pl.BlockSpec(shape, lambda i, j: (0,) * len(shape))
    return pl.pallas_call(
        _gate_kernel,
        grid=(b, nblk),
        in_specs=[
            pl.BlockSpec((SSM_GROUPS, SSM_GROUP_CH, n_rows, CHUNK), lambda i, j: (0, 0, i * nblk + j, 0)),
            pl.BlockSpec((1, ts, d), lambda i, j: (i, j, 0)),
            const((1, d)), const((1, d)), const((d, d)), const((1, d)),
        ],
        out_specs=pl.BlockSpec((1, ts, d), lambda i, j: (i, j, 0)),
        out_shape=jax.ShapeDtypeStruct((b, s, d), F32),
        compiler_params=_cparams(2),
        name="s5_gate",
    )(y4, x, g[None, :], d_skip.astype(F32)[None, :], w_gate.astype(BF16), b_gate.astype(F32)[None, :])


def _pick(n, pref):
    t = min(n, pref)
    while n % t:
        t //= 2
    return t


def kernel(x, norm_mix, norm_mlp, mlp_w1, mlp_w2, w_in, w_fnet, q_norm, k_norm, w_out, lam_re, lam_im, log_dt, b_re, b_im, c_re, c_im, d_skip, w_gate, b_gate, final_norm):
    b, s, d = x.shape
    assert d == D_MODEL and s % CHUNK == 0 and s % GRID_W == 0
    tm = _pick(s, 512)
    gmat = _fourier_fold(w_fnet[0])
    u, v, q, k, vv = _inproj(x, norm_mix[0], w_in[0], gmat, q_norm[0], k_norm[0], _pick(s, 1024))
    fa = _seqdft(u, v)
    att = _attention(q, k, vv, _pick(s, 128))
    x, ut = _tail0(fa, att, x, w_out[0], norm_mlp[0], mlp_w1[0], mlp_w2[0], norm_mix[1], tm)
    yt = _ssm(ut, lam_re[0], lam_im[0], log_dt[0], b_re[0], b_im[0], c_re[0], c_im[0], b, s)
    x = _gate(yt, x, norm_mix[1], d_skip[0], w_gate[0], b_gate[0], _pick(s, 1024))
    return _mlp(x, norm_mlp[1], mlp_w1[1], mlp_w2[1], final_norm, tm, True)
```

```python
import functools
import math

import numpy as np
import jax
import jax.numpy as jnp
from jax import lax
from jax.experimental import pallas as pl
from jax.experimental.pallas import tpu as pltpu

F32 = jnp.float32
BF16 = jnp.bfloat16

EPS = 1e-6
D_MODEL = 1024
MIX_A = 512
FNET_HEAD_DIM = 64
FNET_HEADS = MIX_A // FNET_HEAD_DIM
HEAD_DIM = 64
N_Q_HEADS = 8
N_KV_HEADS = 2
GQA_GROUP = N_Q_HEADS // N_KV_HEADS
Q_WIDTH = N_Q_HEADS * HEAD_DIM
KV_WIDTH = N_KV_HEADS * HEAD_DIM
IN_WIDTH = MIX_A + Q_WIDTH + 2 * KV_WIDTH
GRID_W = 64
ROPE_THETA = 10000.0
ROPE_AXIS_DIM = HEAD_DIM // 2
SSM_GROUP_CH = 16
SSM_GROUPS = D_MODEL // SSM_GROUP_CH
SSM_STATE = 64
D_FF = 4 * D_MODEL
CHUNK = 128
LANES = 128
LOG2_E = 1.4426950408889634
V_ROWS = HEAD_DIM + 16
ATTN_KV_CHUNK = 256
VMEM_LIMIT = 56 * 1024 * 1024


def _cparams(n_grid_dims):
    return pltpu.CompilerParams(
        dimension_semantics=("arbitrary",) * n_grid_dims, vmem_limit_bytes=VMEM_LIMIT)


def _dot(a, b):
    return jnp.dot(a, b, preferred_element_type=F32)


def _split_bf16(a):
    hi = a.astype(BF16)
    lo = (a - hi.astype(F32)).astype(BF16)
    return hi, lo


def _dot3(a, b):
    a_hi, a_lo = _split_bf16(a)
    b_hi, b_lo = _split_bf16(b)
    return _dot(a_hi, b_hi) + (_dot(a_hi, b_lo) + _dot(a_lo, b_hi))


def _rms_rows(x, g):
    ms = jnp.mean(x * x, axis=-1, keepdims=True)
    return x * lax.rsqrt(ms + EPS) * g


def _fold_kernel(cc_ref, sc_ref, w_ref, g_ref):
    w = w_ref[...]
    g_ref[:, :MIX_A] = _dot3(cc_ref[...], w).astype(BF16)
    g_ref[:, MIX_A:] = (-_dot3(sc_ref[...], w)).astype(BF16)


def _fourier_fold(w_fnet):
    n = np.arange(FNET_HEAD_DIM)
    ang = 2.0 * np.pi * np.outer(n, n) / FNET_HEAD_DIM
    scale = FNET_HEAD_DIM ** -0.5
    eye = np.eye(FNET_HEADS)
    cc = jnp.asarray(np.kron(eye, np.cos(ang) * scale), F32)
    sc = jnp.asarray(np.kron(eye, np.sin(ang) * scale), F32)
    eye_j = jnp.eye(FNET_HEADS, dtype=F32)
    w_bd = (eye_j[:, None, :, None] * w_fnet.astype(F32)[:, :, None, :]).reshape(MIX_A, MIX_A)
    return pl.pallas_call(
        _fold_kernel,
        out_shape=jax.ShapeDtypeStruct((MIX_A, 2 * MIX_A), BF16),
        name="fourier_fold",
    )(cc, sc, w_bd)


INPROJ_SPLIT = 4


def _head_norm(t, ones, gain):
    hi, lo = _split_bf16(t * t)
    ss = _dot(hi, ones) + _dot(lo, ones)
    return t * lax.rsqrt(ss * (1.0 / HEAD_DIM) + EPS) * gain


def _rope(t, cos, sin):
    width = t.shape[1]
    reps = width // LANES
    c = jnp.concatenate([cos] * reps, axis=1) if reps > 1 else cos
    s = jnp.concatenate([sin] * reps, axis=1) if reps > 1 else sin
    lane = lax.broadcasted_iota(jnp.int32, t.shape, 1)
    first_half = (lane % ROPE_AXIS_DIM) < (ROPE_AXIS_DIM // 2)
    partner = jnp.where(first_half,
                        pltpu.roll(t, width - ROPE_AXIS_DIM // 2, 1),
                        pltpu.roll(t, ROPE_AXIS_DIM // 2, 1))
    return t * c + partner * s


def _inproj_kernel(x_ref, g_ref, win_ref, gmat_ref, ones_ref, qg_ref, kg_ref,
                   cq_ref, sq_ref, ck_ref, sk_ref,
                   u_ref, v_ref, q_ref, k_ref, vv_ref):
    tm = x_ref.shape[1]
    ones = ones_ref[...]
    sub = tm // INPROJ_SPLIT
    rows = [slice(r * sub, (r + 1) * sub) for r in range(INPROJ_SPLIT)]

    def project(rs):
        h = _rms_rows(x_ref[0, rs, :], g_ref[...]).astype(BF16)
        return _dot(h, win_ref[...])

    def finish(rs, z):
        uv = _dot(z[:, :MIX_A].astype(BF16), gmat_ref[...])
        u_ref[0, rs, :] = uv[:, :MIX_A].astype(BF16)
        v_ref[0, rs, :] = uv[:, MIX_A:].astype(BF16)
        q = z[:, MIX_A:MIX_A + Q_WIDTH]
        k = z[:, MIX_A + Q_WIDTH:MIX_A + Q_WIDTH + KV_WIDTH]
        v = z[:, MIX_A + Q_WIDTH + KV_WIDTH:]
        q = _rope(_head_norm(q, ones, qg_ref[...]), cq_ref[rs, :], sq_ref[rs, :])
        k = _rope(_head_norm(k, ones[:KV_WIDTH, :KV_WIDTH], kg_ref[...]), ck_ref[rs, :], sk_ref[rs, :])
        q_ref[0, :, :, rs] = q.T.reshape(N_Q_HEADS, HEAD_DIM, sub).astype(BF16)
        for hh in range(N_KV_HEADS):
            k_ref[0, hh, rs, :] = k[:, hh * HEAD_DIM:(hh + 1) * HEAD_DIM].astype(BF16)
        vv_ref[0, :, :HEAD_DIM, rs] = v.T.reshape(N_KV_HEADS, HEAD_DIM, sub).astype(BF16)

    z_prev = project(rows[0])
    for r in range(1, INPROJ_SPLIT):
        z_next = project(rows[r])
        finish(rows[r - 1], z_prev)
        z_prev = z_next
    finish(rows[-1], z_prev)
    vv_ref[0, :, HEAD_DIM:, :] = jnp.ones((N_KV_HEADS, V_ROWS - HEAD_DIM, tm), BF16)


def _rope_tables(seq_len):
    t = np.arange(seq_len)
    inv_freq = ROPE_THETA ** (-np.arange(0, ROPE_AXIS_DIM, 2, dtype=np.float64) / ROPE_AXIS_DIM)
    ar = (t // GRID_W)[:, None] * inv_freq
    ac = (t % GRID_W)[:, None] * inv_freq
    cos = np.concatenate([np.cos(ar), np.cos(ar), np.cos(ac), np.cos(ac)], axis=1)
    sin = np.concatenate([-np.sin(ar), np.sin(ar), -np.sin(ac), np.sin(ac)], axis=1)
    cos = np.concatenate([cos, cos], axis=1)
    sin = np.concatenate([sin, sin], axis=1)
    scale = HEAD_DIM ** -0.5 * LOG2_E
    return tuple(jnp.asarray(a, F32) for a in (cos * scale, sin * scale, cos, sin))


def _inproj(x, g, w_in, gmat, q_norm, k_norm, tm):
    b, s, d = x.shape
    cq, sq, ck, sk = _rope_tables(s)
    head = np.arange(Q_WIDTH) // HEAD_DIM
    ones = jnp.asarray(head[:, None] == head[None, :], BF16)
    qg = jnp.tile(q_norm.astype(F32), N_Q_HEADS)[None, :]
    kg = jnp.tile(k_norm.astype(F32), N_KV_HEADS)[None, :]
    const = lambda shape: pl.BlockSpec(shape, lambda i, j: (0,) * len(shape))
    tab = pl.BlockSpec((tm, LANES), lambda i, j: (j, 0))
    return pl.pallas_call(
        _inproj_kernel,
        grid=(b, s // tm),
        in_specs=[
            pl.BlockSpec((1, tm, d), lambda i, j: (i, j, 0)),
            const((1, d)), const((d, IN_WIDTH)), const((MIX_A, 2 * MIX_A)), const((Q_WIDTH, Q_WIDTH)),
            const((1, Q_WIDTH)), const((1, KV_WIDTH)), tab, tab, tab, tab,
        ],
        out_specs=[
            pl.BlockSpec((1, tm, MIX_A), lambda i, j: (i, j, 0)),
            pl.BlockSpec((1, tm, MIX_A), lambda i, j: (i, j, 0)),
            pl.BlockSpec((1, N_Q_HEADS, HEAD_DIM, tm), lambda i, j: (i, 0, 0, j)),
            pl.BlockSpec((1, N_KV_HEADS, tm, HEAD_DIM), lambda i, j: (i, 0, j, 0)),
            pl.BlockSpec((1, N_KV_HEADS, V_ROWS, tm), lambda i, j: (i, 0, 0, j)),
        ],
        out_shape=[
            jax.ShapeDtypeStruct((b, s, MIX_A), BF16),
            jax.ShapeDtypeStruct((b, s, MIX_A), BF16),
            jax.ShapeDtypeStruct((b, N_Q_HEADS, HEAD_DIM, s), BF16),
            jax.ShapeDtypeStruct((b, N_KV_HEADS, s, HEAD_DIM), BF16),
            jax.ShapeDtypeStruct((b, N_KV_HEADS, V_ROWS, s), BF16),
        ],
        compiler_params=_cparams(2),
        name="inproj",
    )(x, g[None, :], w_in.astype(BF16), gmat, ones, qg, kg, cq, sq, ck, sk)


def _dft_tables_kernel(bc_ref, bs_ref, rot_ref, c_ref, s_ref):
    re, im = bc_ref[...], bs_ref[...]
    c_ref[:, :LANES] = re.astype(BF16)
    s_ref[:, :LANES] = im.astype(BF16)
    w, stage = LANES, 0
    while w < c_ref.shape[1]:
        cr = rot_ref[:, 2 * stage:2 * stage + 1]
        sr = rot_ref[:, 2 * stage + 1:2 * stage + 2]
        nre, nim = _cmul(re, im, cr, sr)
        c_ref[:, w:2 * w] = nre.astype(BF16)
        s_ref[:, w:2 * w] = nim.astype(BF16)
        re = jnp.concatenate([re, nre], axis=1)
        im = jnp.concatenate([im, nim], axis=1)
        w, stage = 2 * w, stage + 1


def _dft_tables(seq, tr):
    h = seq // 2
    scale = seq ** -0.5
    j = np.arange(h)[:, None]
    k = np.arange(LANES)[None, :]
    unit = 2.0 * math.pi / seq
    ang = ((j * k) % seq) * unit
    n_stage = int(round(math.log2(h // LANES)))
    widths = np.asarray([LANES << st for st in range(n_stage)] + [0] * (4 - n_stage))
    rang = ((j * widths[None, :]) % seq) * unit
    rot = jnp.asarray(np.stack([np.cos(rang), np.sin(rang)], axis=-1).reshape(h, 8), F32)
    return pl.pallas_call(
        _dft_tables_kernel,
        grid=(h // tr,),
        in_specs=[pl.BlockSpec((tr, LANES), lambda i: (i, 0)),
                  pl.BlockSpec((tr, LANES), lambda i: (i, 0)),
                  pl.BlockSpec((tr, 8), lambda i: (i, 0))],
        out_specs=[pl.BlockSpec((tr, h), lambda i: (i, 0)), pl.BlockSpec((tr, h), lambda i: (i, 0))],
        out_shape=[jax.ShapeDtypeStruct((h, h), BF16), jax.ShapeDtypeStruct((h, h), BF16)],
        compiler_params=_cparams(1),
        name="dft_tables",
    )(jnp.asarray(np.cos(ang) * scale, F32), jnp.asarray(np.sin(ang) * scale, F32), rot)


def _seqdft_kernel(ct_ref, st_ref, u_ref, v_ref, o_ref, ue_ref, vo_ref, d_ref, *, tm, blk):
    t = pl.program_id(1)
    seq = u_ref.shape[1]
    h = seq // 2
    width = u_ref.shape[2]
    scale = seq ** -0.5
    rr = lax.broadcasted_iota(jnp.int32, (blk, blk), 0)
    cc = lax.broadcasted_iota(jnp.int32, (blk, blk), 1)
    flip = jnp.where(rr + cc == blk - 1, 1.0, 0.0).astype(BF16)
    row = lax.broadcasted_iota(jnp.int32, (h, width), 0)

    @pl.when(t == 0)
    def _fold_halves():
        for src_ref, dst_ref, sign in ((u_ref, ue_ref, 1.0), (v_ref, vo_ref, -1.0)):
            rev = jnp.concatenate(
                [_dot(flip, src_ref[0, seq - blk * (a + 1):seq - blk * a, :]) for a in range(h // blk)],
                axis=0)
            mirror = jnp.where(row == 0, 0.0, pltpu.roll(rev, 1, 0))
            dst_ref[...] = (src_ref[0, :h, :].astype(F32) + sign * mirror).astype(BF16)

    ue = ue_ref[...]
    x_mid = u_ref[0, h:h + 1, :].astype(F32) * scale
    p = _dot(ct_ref[...], ue)
    q = _dot(st_ref[...], vo_ref[...])
    j = t * tm + lax.broadcasted_iota(jnp.int32, (tm, 1), 0)
    p = p + (1 - 2 * (j & 1)).astype(F32) * x_mid
    off = pl.multiple_of(t * tm, tm)
    o_ref[0, pl.ds(off, tm), :] = (p + q).astype(BF16)
    d_ref[pl.ds(off, tm), :] = p - q

    @pl.when(t == pl.num_programs(1) - 1)
    def _mirror_half():
        kk = lax.broadcasted_iota(jnp.int32, (8, h), 1)
        alt = ((1 - 2 * (kk & 1)).astype(F32) * scale).astype(BF16)
        p_mid = _dot(alt, ue)[0:1, :] + x_mid
        d_up = jnp.where(row == h - 1, p_mid, pltpu.roll(d_ref[...], h - 1, 0)).astype(BF16)
        for a in range(h // blk):
            o_ref[0, h + blk * a:h + blk * (a + 1), :] = _dot(
                flip, d_up[h - blk * (a + 1):h - blk * a, :]).astype(BF16)


def _seqdft(u, v):
    b, s, w = u.shape
    h = s // 2
    tm = _pick(h, 512)
    blk = _pick(h, 256)
    ct, st = _dft_tables(s, _pick(h, 256))
    return pl.pallas_call(
        functools.partial(_seqdft_kernel, tm=tm, blk=blk),
        grid=(b, h // tm),
        in_specs=[
            pl.BlockSpec((tm, h), lambda i, j: (j, 0)),
            pl.BlockSpec((tm, h), lambda i, j: (j, 0)),
            pl.BlockSpec((1, s, w), lambda i, j: (i, 0, 0)),
            pl.BlockSpec((1, s, w), lambda i, j: (i, 0, 0)),
        ],
        out_specs=pl.BlockSpec((1, s, w), lambda i, j: (i, 0, 0)),
        out_shape=jax.ShapeDtypeStruct((b, s, w), BF16),
        scratch_shapes=[pltpu.VMEM((h, w), BF16), pltpu.VMEM((h, w), BF16), pltpu.VMEM((h, w), F32)],
        compiler_params=_cparams(2),
        name="seqdft",
    )(ct, st, u, v)


def _attn_kernel(qt_ref, k_ref, vt_ref, o_ref, sa_ref, sb_ref, ma_ref, mb_ref, *, tq):
    seq = k_ref.shape[2]
    per_head = seq // tq
    nblk = N_KV_HEADS * per_head
    ck = min(seq, ATTN_KV_CHUNK)
    chunks = [slice(c * ck, (c + 1) * ck) for c in range(seq // ck)]

    def locate(i):
        head = i // per_head
        return head, pl.multiple_of((i - head * per_head) * tq, tq)

    def query_block(i):
        head, off = locate(i)
        return jnp.concatenate(
            [qt_ref[0, head * GQA_GROUP + g, :, pl.ds(off, tq)] for g in range(GQA_GROUP)], axis=1)

    def score_chunk(i, qt, rows, s_ref, m):
        s = _dot(k_ref[0, locate(i)[0], rows, :], qt)
        s_ref[rows, :] = s
        mc = jnp.max(s, axis=0, keepdims=True)
        return mc if m is None else jnp.maximum(m, mc)

    def value_chunk(i, rows, s_ref, m, acc):
        p = jnp.exp2(s_ref[rows, :] - m).astype(BF16)
        part = _dot(vt_ref[0, locate(i)[0], :, rows], p)
        return part if acc is None else acc + part

    def write_out(i, acc):
        o = acc[:HEAD_DIM] * (1.0 / acc[HEAD_DIM:HEAD_DIM + 1])
        head, off = locate(i)
        for g in range(GQA_GROUP):
            r0 = pl.multiple_of((head * GQA_GROUP + g) * HEAD_DIM, HEAD_DIM)
            o_ref[0, pl.ds(r0, HEAD_DIM), pl.ds(off, tq)] = o[:, g * tq:(g + 1) * tq].astype(BF16)

    def step(i_next, s_next, m_next, i_cur, s_cur, m_cur):
        qt = None if i_next is None else query_block(i_next)
        m_old = None if i_cur is None else m_cur[...]
        m_new, acc = None, None
        for rows in chunks:
            if i_next is not None:
                m_new = score_chunk(i_next, qt, rows, s_next, m_new)
            if i_cur is not None:
                acc = value_chunk(i_cur, rows, s_cur, m_old, acc)
        if i_next is not None:
            m_next[...] = m_new
        if i_cur is not None:
            write_out(i_cur, acc)

    step(0, sa_ref, ma_ref, None, None, None)

    def pair(j, carry):
        step(2 * j + 1, sb_ref, mb_ref, 2 * j, sa_ref, ma_ref)
        step(2 * j + 2, sa_ref, ma_ref, 2 * j + 1, sb_ref, mb_ref)
        return carry

    lax.fori_loop(0, nblk // 2 - 1, pair, 0)
    step(nblk - 1, sb_ref, mb_ref, nblk - 2, sa_ref, ma_ref)
    step(None, None, None, nblk - 1, sb_ref, mb_ref)


def _attention(qt, k, vt, tq):
    b, _, _, s = qt.shape
    assert (s // tq) % 2 == 0
    width = GQA_GROUP * tq
    return pl.pallas_call(
        functools.partial(_attn_kernel, tq=tq),
        grid=(b,),
        in_specs=[
            pl.BlockSpec((1, N_Q_HEADS, HEAD_DIM, s), lambda i: (i, 0, 0, 0)),
            pl.BlockSpec((1, N_KV_HEADS, s, HEAD_DIM), lambda i: (i, 0, 0, 0)),
            pl.BlockSpec((1, N_KV_HEADS, V_ROWS, s), lambda i: (i, 0, 0, 0)),
        ],
        out_specs=pl.BlockSpec((1, Q_WIDTH, s), lambda i: (i, 0, 0)),
        out_shape=jax.ShapeDtypeStruct((b, Q_WIDTH, s), BF16),
        scratch_shapes=[pltpu.VMEM((s, width), F32), pltpu.VMEM((s, width), F32),
                        pltpu.VMEM((1, width), F32), pltpu.VMEM((1, width), F32)],
        compiler_params=_cparams(1),
        name="gqa_attention",
    )(qt, k, vt)


def _mlp_block(x, g, w1_ref, w2_ref):
    h = _rms_rows(x, g).astype(BF16)
    a = jnp.maximum(_dot(h, w1_ref[...]), 0.0)
    return x + _dot((a * a).astype(BF16), w2_ref[...])


TAIL_SPLIT = 2


def _tail0_kernel(fa_ref, att_ref, x_ref, wa_ref, wb_ref, g_ref, w1_ref, w2_ref, gn_ref,
                  o_ref, ut_ref):
    tm = x_ref.shape[1]
    sub = tm // TAIL_SPLIT
    rows = [slice(r * sub, (r + 1) * sub) for r in range(TAIL_SPLIT)]

    def project(rs):
        att = lax.dot_general(att_ref[0, :, rs], wb_ref[...], (((0,), (0,)), ((), ())),
                              preferred_element_type=F32)
        return x_ref[0, rs, :] + (_dot(fa_ref[0, rs, :], wa_ref[...]) + att)

    def finish(rs, x1):
        x2 = _mlp_block(x1, g_ref[...], w1_ref, w2_ref)
        o_ref[0, rs, :] = x2
        ut_ref[:, rs] = _rms_rows(x2, gn_ref[...]).T.astype(BF16)

    x_prev = project(rows[0])
    for r in range(1, TAIL_SPLIT):
        x_next = project(rows[r])
        finish(rows[r - 1], x_prev)
        x_prev = x_next
    finish(rows[-1], x_prev)


def _tail0(fa, att, x, w_out, g_mlp, w1, w2, g_next, tm):
    b, s, d = x.shape
    nblk = s // tm
    w = w_out.astype(BF16)
    const = lambda shape: pl.BlockSpec(shape, lambda i, j: (0,) * len(shape),
                                       pipeline_mode=pl.Buffered(1))
    return pl.pallas_call(
        _tail0_kernel,
        grid=(b, nblk),
        in_specs=[
            pl.BlockSpec((1, tm, MIX_A), lambda i, j: (i, j, 0)),
            pl.BlockSpec((1, Q_WIDTH, tm), lambda i, j: (i, 0, j)),
            pl.BlockSpec((1, tm, d), lambda i, j: (i, j, 0)),
            const((MIX_A, d)), const((Q_WIDTH, d)),
            const((1, d)), const((d, D_FF)), const((D_FF, d)), const((1, d)),
        ],
        out_specs=[pl.BlockSpec((1, tm, d), lambda i, j: (i, j, 0)),
                   pl.BlockSpec((d, tm), lambda i, j: (0, i * nblk + j))],
        out_shape=[jax.ShapeDtypeStruct((b, s, d), F32),
                   jax.ShapeDtypeStruct((d, b * s), BF16)],
        compiler_params=_cparams(2),
        name="outproj_mlp",
    )(fa, att, x, w[:MIX_A], w[MIX_A:], g_mlp[None, :], w1.astype(BF16), w2.astype(BF16),
      g_next[None, :])


def _mlp_kernel(x_ref, g_ref, w1_ref, w2_ref, gf_ref, o_ref, *, final_norm):
    y = _mlp_block(x_ref[0], g_ref[...], w1_ref, w2_ref)
    if final_norm:
        y = _rms_rows(y, gf_ref[...])
    o_ref[0] = y


def _mlp(x, g, w1, w2, g_final, tm, final_norm):
    b, s, d = x.shape
    const = lambda shape: pl.BlockSpec(shape, lambda i, j: (0,) * len(shape),
                                       pipeline_mode=pl.Buffered(1))
    return pl.pallas_call(
        functools.partial(_mlp_kernel, final_norm=final_norm),
        grid=(b, s // tm),
        in_specs=[
            pl.BlockSpec((1, tm, d), lambda i, j: (i, j, 0)),
            const((1, d)), const((d, D_FF)), const((D_FF, d)), const((1, d)),
        ],
        out_specs=pl.BlockSpec((1, tm, d), lambda i, j: (i, j, 0)),
        out_shape=jax.ShapeDtypeStruct((b, s, d), F32),
        compiler_params=_cparams(2),
        name="mlp_final" if final_norm else "mlp",
    )(x, g[None, :], w1.astype(BF16), w2.astype(BF16), g_final[None, :])


def _cmul(ar, ai, br, bi):
    return ar * br - ai * bi, ar * bi + ai * br


def _cpow(a, theta, e):
    mag = jnp.exp(e * a)
    ang = e * theta
    return mag * jnp.cos(ang), mag * jnp.sin(ang)


def _ssm_kernel(ut_ref, prow_ref, crow_ref, bcol_ref, shift_ref, y_ref,
                wall_ref, mmat_ref, cmat_ref, yacc_ref, xprev_ref, *, n_chunks):
    P = SSM_STATE
    C = SSM_GROUP_CH
    L = CHUNK
    rows = ut_ref.shape[2]

    prow = prow_ref[0]
    lre, lim = prow[0:1, :], prow[1:2, :]
    dt = jnp.exp(prow[2:3, :])
    a_row, th_row = lre * dt, lim * dt
    lbr, lbi = _cpow(a_row, th_row, 1.0)
    inv = 1.0 / (lre * lre + lim * lim)
    coef_r, coef_i = _cmul(lbr - 1.0, lbi, lre * inv, -lim * inv)
    bbr, bbi = _cmul(coef_r, coef_i, bcol_ref[0, 0].T, bcol_ref[0, 1].T)

    sq = [(lbr, lbi)]
    while len(sq) <= 7:
        sq.append(_cmul(*sq[-1], *sq[-1]))
    s8 = lax.broadcasted_iota(jnp.int32, (8, 2 * P), 0).astype(F32)
    up = _cpow(a_row, th_row, s8)
    dn = _cpow(a_row, th_row, 7.0 - s8)
    for k in range(3, 7):
        hi_up = _cmul(*up, *sq[k])
        hi_dn = _cmul(*dn, *sq[k])
        up = tuple(jnp.concatenate([lo, hi], axis=0) for lo, hi in zip(up, hi_up))
        dn = tuple(jnp.concatenate([hi, lo], axis=0) for lo, hi in zip(dn, hi_dn))
    up1 = _cmul(*up, lbr, lbi)
    dn1 = _cmul(*dn, lbr, lbi)
    fwd_lane = lax.broadcasted_iota(jnp.int32, (L, 2 * P), 1) < P
    row0 = lax.broadcasted_iota(jnp.int32, (L, 2 * P), 0) == 0

    pwr, pwi = jnp.where(fwd_lane, dn[0], up[0]), jnp.where(fwd_lane, dn[1], up[1])
    for c in range(C):
        xr, xi = _cmul(pwr, pwi, bbr[c:c + 1, :], bbi[c:c + 1, :])
        mmat_ref[c * L:(c + 1) * L, C * L:C * L + 2 * P] = xr.astype(BF16)
        mmat_ref[c * L:(c + 1) * L, C * L + 2 * P:] = xi.astype(BF16)

    qwr = jnp.where(fwd_lane, up1[0], dn1[0]).T
    qwi = jnp.where(fwd_lane, up1[1], dn1[1]).T
    ccr, cci = crow_ref[0, 0].T, crow_ref[0, 1].T
    for c in range(C):
        xr, xi = _cmul(qwr, qwi, ccr[:, c:c + 1], cci[:, c:c + 1])
        cmat_ref[:2 * P, c * L:(c + 1) * L] = xr.astype(BF16)
        cmat_ref[2 * P:, c * L:(c + 1) * L] = (-xi).astype(BF16)

    zero = jnp.zeros((L, 2 * P), F32)
    k_lo = (jnp.where(fwd_lane, up[0], jnp.where(row0, 1.0, 0.0)), jnp.where(fwd_lane, up[1], zero))
    k_hi = (jnp.where(fwd_lane, zero, dn1[0]), jnp.where(fwd_lane, zero, dn1[1]))
    kwr = jnp.concatenate([k_lo[0].T, k_hi[0].T], axis=1)
    kwi = jnp.concatenate([k_lo[1].T, k_hi[1].T], axis=1)
    crr, cri = crow_ref[0, 0], crow_ref[0, 1]
    cb_r, cb_i = [], []
    for c in range(C):
        xr, xi = _cmul(crr, cri, bbr[c:c + 1, :], bbi[c:c + 1, :])
        cb_r.append(xr)
        cb_i.append(xi)
    cb_r = jnp.concatenate(cb_r, axis=0)
    cb_i = jnp.concatenate(cb_i, axis=0)
    ktab = _dot3(cb_r, kwr) - _dot3(cb_i, kwi)

    def fine(blk):
        return pltpu.roll(jnp.broadcast_to(ktab[blk:blk + 1, :], (8, 2 * L)), 0, 1,
                          stride=1, stride_axis=0)
    for blk in range(0, C * C, 2):
        wall_ref[blk * 8:blk * 8 + 16, :] = jnp.concatenate(
            [fine(blk), fine(blk + 1)], axis=0).astype(BF16)

    own_region = pl.when(pl.program_id(0) >= 0)

    @own_region
    def _coarse():
        sh = _dot(wall_ref[...], shift_ref[...])
        for hh in range(L // 16):
            lo = 2 * hh * L
            pair = jnp.concatenate([sh[:, lo:lo + L].reshape(C * C, 8, L),
                                    sh[:, lo + L:lo + 2 * L].reshape(C * C, 8, L)],
                                   axis=1).astype(BF16)
            for cp in range(C):
                mmat_ref[cp * L + 16 * hh:cp * L + 16 * (hh + 1), :C * L] = jnp.concatenate(
                    [pair[cp * C + c] for c in range(C)], axis=1)


    @own_region
    def _intra():
        u = jnp.concatenate([ut_ref[0, c] for c in range(C)], axis=1)
        yacc_ref[...] = _dot(u, mmat_ref[...])
    xr = yacc_ref[:, C * L:C * L + 2 * P]
    xi = yacc_ref[:, C * L + 2 * P:]

    k_idx = lax.broadcasted_iota(jnp.int32, (rows, 2 * P), 0) % n_chunks
    is_f = lax.broadcasted_iota(jnp.int32, (rows, 2 * P), 1) < P
    ar, ai = sq[7]

    def shifted(t, step):
        down = pltpu.roll(t, step, 0)
        up = pltpu.roll(t, rows - step, 0)
        return jnp.where(is_f, jnp.where(k_idx >= step, down, 0.0),
                         jnp.where(k_idx < n_chunks - step, up, 0.0))

    step = 1
    while step < n_chunks:
        sr, si = shifted(xr, step), shifted(xi, step)
        pr, pi = _cmul(sr, si, ar, ai)
        xr, xi = xr + pr, xi + pi
        ar, ai = _cmul(ar, ai, ar, ai)
        step *= 2
    xprev_ref[...] = jnp.concatenate([shifted(xr, 1), shifted(xi, 1)], axis=1).astype(BF16)

    @own_region
    def _inter():
        y = yacc_ref[:, :C * L] + _dot(xprev_ref[...], cmat_ref[...])
        for c in range(C):
            y_ref[0, c] = y[:, c * L:(c + 1) * L]


def _shift_mats():
    L = CHUNK
    m = np.arange(2 * L)[:, None]
    col = np.arange((L // 8) * L)[None, :]
    h, i = col // L, col % L
    return jnp.asarray(m == (i - 8 * h) % (2 * L), BF16)


def _ssm(ut, lam_re, lam_im, log_dt, b_re, b_im, c_re, c_im, batch, seq):
    G, C, P, L = SSM_GROUPS, SSM_GROUP_CH, SSM_STATE, CHUNK
    n_chunks = seq // L
    rows = batch * n_chunks
    ut4 = ut.reshape(G, C, rows, L)
    f32 = lambda t: t.astype(F32)
    ldt = jnp.broadcast_to(f32(log_dt)[:, :, None], (2, G, P))
    prow = jnp.stack([f32(lam_re), f32(lam_im), ldt]).transpose(2, 0, 1, 3).reshape(G, 3, 2 * P)
    crow = jnp.stack([f32(c_re), f32(c_im)]).transpose(2, 0, 3, 1, 4).reshape(G, 2, C, 2 * P)
    bcol = jnp.stack([f32(b_re), f32(b_im)]).transpose(2, 0, 1, 3, 4).reshape(G, 2, 2 * P, C)
    return pl.pallas_call(
        functools.partial(_ssm_kernel, n_chunks=n_chunks),
        grid=(G,),
        in_specs=[
            pl.BlockSpec((1, C, rows, L), lambda g: (g, 0, 0, 0)),
            pl.BlockSpec((1, 3, 2 * P), lambda g: (g, 0, 0)),
            pl.BlockSpec((1, 2, C, 2 * P), lambda g: (g, 0, 0, 0)),
            pl.BlockSpec((1, 2, 2 * P, C), lambda g: (g, 0, 0, 0)),
            pl.BlockSpec((2 * L, (L // 8) * L), lambda g: (0, 0), pipeline_mode=pl.Buffered(1)),
        ],
        out_specs=pl.BlockSpec((1, C, rows, L), lambda g: (g, 0, 0, 0)),
        out_shape=jax.ShapeDtypeStruct((G, C, rows, L), F32),
        scratch_shapes=[
            pltpu.VMEM((C * C * 8, 2 * L), BF16),
            pltpu.VMEM((C * L, C * L + 4 * P), BF16),
            pltpu.VMEM((4 * P, C * L), BF16),
            pltpu.VMEM((rows, C * L + 4 * P), F32),
            pltpu.VMEM((rows, 4 * P), BF16),
        ],
        compiler_params=_cparams(1),
        name="s5_scan",
    )(ut4, prow, crow, bcol, _shift_mats())


def _gelu_tanh(y):
    return 0.5 * y * (1.0 + jnp.tanh(math.sqrt(2.0 / math.pi) * (y + 0.044715 * (y * y * y))))


def _gate_kernel(yt_ref, x_ref, g_ref, dsk_ref, wg_ref, bg_ref, o_ref):
    x = x_ref[0]
    u = _rms_rows(x, g_ref[...])
    n_rows = yt_ref.shape[2]
    y_rct = jnp.swapaxes(yt_ref[...].reshape(D_MODEL, n_rows, CHUNK), 0, 1)
    y = jnp.concatenate([y_rct[r].T for r in range(n_rows)], axis=0) + dsk_ref[...] * u
    gl = _gelu_tanh(y)
    gate = _dot(gl.astype(BF16), wg_ref[...]) + bg_ref[...]
    o_ref[0] = x + gl * (0.5 + 0.5 * jnp.tanh(0.5 * gate))


def _gate(y4, x, g, d_skip, w_gate, b_gate, ts):
    b, s, d = x.shape
    nblk = s // ts
    n_rows = ts // CHUNK
    const = lambda shape: pl.BlockSpec(shape, lambda i, j: (0,) * len(shape))
    return pl.pallas_call(
        _gate_kernel,
        grid=(b, nblk),
        in_specs=[
            pl.BlockSpec((SSM_GROUPS, SSM_GROUP_CH, n_rows, CHUNK), lambda i, j: (0, 0, i * nblk + j, 0)),
            pl.BlockSpec((1, ts, d), lambda i, j: (i, j, 0)),
            const((1, d)), const((1, d)), const((d, d)), const((1, d)),
        ],
        out_specs=pl.BlockSpec((1, ts, d), lambda i, j: (i, j, 0)),
        out_shape=jax.ShapeDtypeStruct((b, s, d), F32),
        compiler_params=_cparams(2),
        name="s5_gate",
    )(y4, x, g[None, :], d_skip.astype(F32)[None, :], w_gate.astype(BF16), b_gate.astype(F32)[None, :])


def _pick(n, pref):
    t = min(n, pref)
    while n % t:
        t //= 2
    return t


def kernel(x, norm_mix, norm_mlp, mlp_w1, mlp_w2, w_in, w_fnet, q_norm, k_norm, w_out, lam_re, lam_im, log_dt, b_re, b_im, c_re, c_im, d_skip, w_gate, b_gate, final_norm):
    b, s, d = x.shape
    assert d == D_MODEL and s % CHUNK == 0 and s % GRID_W == 0
    tm = _pick(s, 512)
    gmat = _fourier_fold(w_fnet[0])
    u, v, q, k, vv = _inproj(x, norm_mix[0], w_in[0], gmat, q_norm[0], k_norm[0], _pick(s, 1024))
    fa = _seqdft(u, v)
    att = _attention(q, k, vv, _pick(s, 128))
    x, ut = _tail0(fa, att, x, w_out[0], norm_mlp[0], mlp_w1[0], mlp_w2[0], norm_mix[1], tm)
    yt = _ssm(ut, lam_re[0], lam_im[0], log_dt[0], b_re[0], b_im[0], c_re[0], c_im[0], b, s)
    x = _gate(yt, x, norm_mix[1], d_skip[0], w_gate[0], b_gate[0], _pick(s, 1024))
    return _mlp(x, norm_mlp[1], mlp_w1[1], mlp_w2[1], final_norm, tm, True)
```

```python
import functools
import math

import numpy as np
import jax
import jax.numpy as jnp
from jax import lax
from jax.experimental import pallas as pl
from jax.experimental.pallas import tpu as pltpu

F32 = jnp.float32
BF16 = jnp.bfloat16

EPS = 1e-6
D_MODEL = 1024
MIX_A = 512
FNET_HEAD_DIM = 64
FNET_HEADS = MIX_A // FNET_HEAD_DIM
HEAD_DIM = 64
N_Q_HEADS = 8
N_KV_HEADS = 2
GQA_GROUP = N_Q_HEADS // N_KV_HEADS
Q_WIDTH = N_Q_HEADS * HEAD_DIM
KV_WIDTH = N_KV_HEADS * HEAD_DIM
IN_WIDTH = MIX_A + Q_WIDTH + 2 * KV_WIDTH
GRID_W = 64
ROPE_THETA = 10000.0
ROPE_AXIS_DIM = HEAD_DIM // 2
SSM_GROUP_CH = 16
SSM_GROUPS = D_MODEL // SSM_GROUP_CH
SSM_STATE = 64
D_FF = 4 * D_MODEL
CHUNK = 128
LANES = 128
LOG2_E = 1.4426950408889634
V_ROWS = HEAD_DIM + 16
ATTN_KV_CHUNK = 256
VMEM_LIMIT = 56 * 1024 * 1024


def _cparams(n_grid_dims):
    return pltpu.CompilerParams(
        dimension_semantics=("arbitrary",) * n_grid_dims, vmem_limit_bytes=VMEM_LIMIT)


def _dot(a, b):
    return jnp.dot(a, b, preferred_element_type=F32)


def _split_bf16(a):
    hi = a.astype(BF16)
    lo = (a - hi.astype(F32)).astype(BF16)
    return hi, lo


def _dot3(a, b):
    a_hi, a_lo = _split_bf16(a)
    b_hi, b_lo = _split_bf16(b)
    return _dot(a_hi, b_hi) + (_dot(a_hi, b_lo) + _dot(a_lo, b_hi))


def _rms_rows(x, g):
    ms = jnp.mean(x * x, axis=-1, keepdims=True)
    return x * lax.rsqrt(ms + EPS) * g


def _fold_kernel(cc_ref, sc_ref, w_ref, g_ref):
    w = w_ref[...]
    g_ref[:, :MIX_A] = _dot3(cc_ref[...], w).astype(BF16)
    g_ref[:, MIX_A:] = (-_dot3(sc_ref[...], w)).astype(BF16)


def _fourier_fold(w_fnet):
    n = np.arange(FNET_HEAD_DIM)
    ang = 2.0 * np.pi * np.outer(n, n) / FNET_HEAD_DIM
    scale = FNET_HEAD_DIM ** -0.5
    eye = np.eye(FNET_HEADS)
    cc = jnp.asarray(np.kron(eye, np.cos(ang) * scale), F32)
    sc = jnp.asarray(np.kron(eye, np.sin(ang) * scale), F32)
    eye_j = jnp.eye(FNET_HEADS, dtype=F32)
    w_bd = (eye_j[:, None, :, None] * w_fnet.astype(F32)[:, :, None, :]).reshape(MIX_A, MIX_A)
    return pl.pallas_call(
        _fold_kernel,
        out_shape=jax.ShapeDtypeStruct((MIX_A, 2 * MIX_A), BF16),
        name="fourier_fold",
    )(cc, sc, w_bd)


INPROJ_SPLIT = 4


def _head_norm(t, ones, gain):
    hi, lo = _split_bf16(t * t)
    ss = _dot(hi, ones) + _dot(lo, ones)
    return t * lax.rsqrt(ss * (1.0 / HEAD_DIM) + EPS) * gain


def _rope(t, cos, sin):
    width = t.shape[1]
    reps = width // LANES
    c = jnp.concatenate([cos] * reps, axis=1) if reps > 1 else cos
    s = jnp.concatenate([sin] * reps, axis=1) if reps > 1 else sin
    lane = lax.broadcasted_iota(jnp.int32, t.shape, 1)
    first_half = (lane % ROPE_AXIS_DIM) < (ROPE_AXIS_DIM // 2)
    partner = jnp.where(first_half,
                        pltpu.roll(t, width - ROPE_AXIS_DIM // 2, 1),
                        pltpu.roll(t, ROPE_AXIS_DIM // 2, 1))
    return t * c + partner * s


def _inproj_kernel(x_ref, g_ref, win_ref, gmat_ref, ones_ref, qg_ref, kg_ref,
                   cq_ref, sq_ref, ck_ref, sk_ref,
                   u_ref, v_ref, q_ref, k_ref, vv_ref):
    tm = x_ref.shape[1]
    ones = ones_ref[...]
    sub = tm // INPROJ_SPLIT
    rows = [slice(r * sub, (r + 1) * sub) for r in range(INPROJ_SPLIT)]

    def project(rs):
        h = _rms_rows(x_ref[0, rs, :], g_ref[...]).astype(BF16)
        return _dot(h, win_ref[...])

    def finish(rs, z):
        uv = _dot(z[:, :MIX_A].astype(BF16), gmat_ref[...])
        u_ref[0, rs, :] = uv[:, :MIX_A].astype(BF16)
        v_ref[0, rs, :] = uv[:, MIX_A:].astype(BF16)
        q = z[:, MIX_A:MIX_A + Q_WIDTH]
        k = z[:, MIX_A + Q_WIDTH:MIX_A + Q_WIDTH + KV_WIDTH]
        v = z[:, MIX_A + Q_WIDTH + KV_WIDTH:]
        q = _rope(_head_norm(q, ones, qg_ref[...]), cq_ref[rs, :], sq_ref[rs, :])
        k = _rope(_head_norm(k, ones[:KV_WIDTH, :KV_WIDTH], kg_ref[...]), ck_ref[rs, :], sk_ref[rs, :])
        q_ref[0, :, :, rs] = q.T.reshape(N_Q_HEADS, HEAD_DIM, sub).astype(BF16)
        for hh in range(N_KV_HEADS):
            k_ref[0, hh, rs, :] = k[:, hh * HEAD_DIM:(hh + 1) * HEAD_DIM].astype(BF16)
        vv_ref[0, :, :HEAD_DIM, rs] = v.T.reshape(N_KV_HEADS, HEAD_DIM, sub).astype(BF16)

    z_prev = project(rows[0])
    for r in range(1, INPROJ_SPLIT):
        z_next = project(rows[r])
        finish(rows[r - 1], z_prev)
        z_prev = z_next
    finish(rows[-1], z_prev)
    vv_ref[0, :, HEAD_DIM:, :] = jnp.ones((N_KV_HEADS, V_ROWS - HEAD_DIM, tm), BF16)


def _rope_tables(seq_len):
    t = np.arange(seq_len)
    inv_freq = ROPE_THETA ** (-np.arange(0, ROPE_AXIS_DIM, 2, dtype=np.float64) / ROPE_AXIS_DIM)
    ar = (t // GRID_W)[:, None] * inv_freq
    ac = (t % GRID_W)[:, None] * inv_freq
    cos = np.concatenate([np.cos(ar), np.cos(ar), np.cos(ac), np.cos(ac)], axis=1)
    sin = np.concatenate([-np.sin(ar), np.sin(ar), -np.sin(ac), np.sin(ac)], axis=1)
    cos = np.concatenate([cos, cos], axis=1)
    sin = np.concatenate([sin, sin], axis=1)
    scale = HEAD_DIM ** -0.5 * LOG2_E
    return tuple(jnp.asarray(a, F32) for a in (cos * scale, sin * scale, cos, sin))


def _inproj(x, g, w_in, gmat, q_norm, k_norm, tm):
    b, s, d = x.shape
    cq, sq, ck, sk = _rope_tables(s)
    head = np.arange(Q_WIDTH) // HEAD_DIM
    ones = jnp.asarray(head[:, None] == head[None, :], BF16)
    qg = jnp.tile(q_norm.astype(F32), N_Q_HEADS)[None, :]
    kg = jnp.tile(k_norm.astype(F32), N_KV_HEADS)[None, :]
    const = lambda shape: pl.BlockSpec(shape, lambda i, j: (0,) * len(shape))
    tab = pl.BlockSpec((tm, LANES), lambda i, j: (j, 0))
    return pl.pallas_call(
        _inproj_kernel,
        grid=(b, s // tm),
        in_specs=[
            pl.BlockSpec((1, tm, d), lambda i, j: (i, j, 0)),
            const((1, d)), const((d, IN_WIDTH)), const((MIX_A, 2 * MIX_A)), const((Q_WIDTH, Q_WIDTH)),
            const((1, Q_WIDTH)), const((1, KV_WIDTH)), tab, tab, tab, tab,
        ],
        out_specs=[
            pl.BlockSpec((1, tm, MIX_A), lambda i, j: (i, j, 0)),
            pl.BlockSpec((1, tm, MIX_A), lambda i, j: (i, j, 0)),
            pl.BlockSpec((1, N_Q_HEADS, HEAD_DIM, tm), lambda i, j: (i, 0, 0, j)),
            pl.BlockSpec((1, N_KV_HEADS, tm, HEAD_DIM), lambda i, j: (i, 0, j, 0)),
            pl.BlockSpec((1, N_KV_HEADS, V_ROWS, tm), lambda i, j: (i, 0, 0, j)),
        ],
        out_shape=[
            jax.ShapeDtypeStruct((b, s, MIX_A), BF16),
            jax.ShapeDtypeStruct((b, s, MIX_A), BF16),
            jax.ShapeDtypeStruct((b, N_Q_HEADS, HEAD_DIM, s), BF16),
            jax.ShapeDtypeStruct((b, N_KV_HEADS, s, HEAD_DIM), BF16),
            jax.ShapeDtypeStruct((b, N_KV_HEADS, V_ROWS, s), BF16),
        ],
        compiler_params=_cparams(2),
        name="inproj",
    )(x, g[None, :], w_in.astype(BF16), gmat, ones, qg, kg, cq, sq, ck, sk)


def _dft_tables_kernel(bc_ref, bs_ref, rot_ref, c_ref, s_ref):
    re, im = bc_ref[...], bs_ref[...]
    c_ref[:, :LANES] = re.astype(BF16)
    s_ref[:, :LANES] = im.astype(BF16)
    w, stage = LANES, 0
    while w < c_ref.shape[1]:
        cr = rot_ref[:, 2 * stage:2 * stage + 1]
        sr = rot_ref[:, 2 * stage + 1:2 * stage + 2]
        nre, nim = _cmul(re, im, cr, sr)
        c_ref[:, w:2 * w] = nre.astype(BF16)
        s_ref[:, w:2 * w] = nim.astype(BF16)
        re = jnp.concatenate([re, nre], axis=1)
        im = jnp.concatenate([im, nim], axis=1)
        w, stage = 2 * w, stage + 1


def _dft_tables(seq, tr):
    h = seq // 2
    scale = seq ** -0.5
    j = np.arange(h)[:, None]
    k = np.arange(LANES)[None, :]
    unit = 2.0 * math.pi / seq
    ang = ((j * k) % seq) * unit
    n_stage = int(round(math.log2(h // LANES)))
    widths = np.asarray([LANES << st for st in range(n_stage)] + [0] * (4 - n_stage))
    rang = ((j * widths[None, :]) % seq) * unit
    rot = jnp.asarray(np.stack([np.cos(rang), np.sin(rang)], axis=-1).reshape(h, 8), F32)
    return pl.pallas_call(
        _dft_tables_kernel,
        grid=(h // tr,),
        in_specs=[pl.BlockSpec((tr, LANES), lambda i: (i, 0)),
                  pl.BlockSpec((tr, LANES), lambda i: (i, 0)),
                  pl.BlockSpec((tr, 8), lambda i: (i, 0))],
        out_specs=[pl.BlockSpec((tr, h), lambda i: (i, 0)), pl.BlockSpec((tr, h), lambda i: (i, 0))],
        out_shape=[jax.ShapeDtypeStruct((h, h), BF16), jax.ShapeDtypeStruct((h, h), BF16)],
        compiler_params=_cparams(1),
        name="dft_tables",
    )(jnp.asarray(np.cos(ang) * scale, F32), jnp.asarray(np.sin(ang) * scale, F32), rot)


def _seqdft_kernel(ct_ref, st_ref, u_ref, v_ref, o_ref, ue_ref, vo_ref, d_ref, *, tm, blk):
    t = pl.program_id(1)
    seq = u_ref.shape[1]
    h = seq // 2
    width = u_ref.shape[2]
    scale = seq ** -0.5
    rr = lax.broadcasted_iota(jnp.int32, (blk, blk), 0)
    cc = lax.broadcasted_iota(jnp.int32, (blk, blk), 1)
    flip = jnp.where(rr + cc == blk - 1, 1.0, 0.0).astype(BF16)
    row = lax.broadcasted_iota(jnp.int32, (h, width), 0)

    @pl.when(t == 0)
    def _fold_halves():
        for src_ref, dst_ref, sign in ((u_ref, ue_ref, 1.0), (v_ref, vo_ref, -1.0)):
            rev = jnp.concatenate(
                [_dot(flip, src_ref[0, seq - blk * (a + 1):seq - blk * a, :]) for a in range(h // blk)],
                axis=0)
            mirror = jnp.where(row == 0, 0.0, pltpu.roll(rev, 1, 0))
            dst_ref[...] = (src_ref[0, :h, :].astype(F32) + sign * mirror).astype(BF16)

    ue = ue_ref[...]
    x_mid = u_ref[0, h:h + 1, :].astype(F32) * scale
    p = _dot(ct_ref[...], ue)
    q = _dot(st_ref[...], vo_ref[...])
    j = t * tm + lax.broadcasted_iota(jnp.int32, (tm, 1), 0)
    p = p + (1 - 2 * (j & 1)).astype(F32) * x_mid
    off = pl.multiple_of(t * tm, tm)
    o_ref[0, pl.ds(off, tm), :] = (p + q).astype(BF16)
    d_ref[pl.ds(off, tm), :] = p - q

    @pl.when(t == pl.num_programs(1) - 1)
    def _mirror_half():
        kk = lax.broadcasted_iota(jnp.int32, (8, h), 1)
        alt = ((1 - 2 * (kk & 1)).astype(F32) * scale).astype(BF16)
        p_mid = _dot(alt, ue)[0:1, :] + x_mid
        d_up = jnp.where(row == h - 1, p_mid, pltpu.roll(d_ref[...], h - 1, 0)).astype(BF16)
        for a in range(h // blk):
            o_ref[0, h + blk * a:h + blk * (a + 1), :] = _dot(
                flip, d_up[h - blk * (a + 1):h - blk * a, :]).astype(BF16)


def _seqdft(u, v):
    b, s, w = u.shape
    h = s // 2
    tm = _pick(h, 512)
    blk = _pick(h, 256)
    ct, st = _dft_tables(s, _pick(h, 256))
    return pl.pallas_call(
        functools.partial(_seqdft_kernel, tm=tm, blk=blk),
        grid=(b, h // tm),
        in_specs=[
            pl.BlockSpec((tm, h), lambda i, j: (j, 0)),
            pl.BlockSpec((tm, h), lambda i, j: (j, 0)),
            pl.BlockSpec((1, s, w), lambda i, j: (i, 0, 0)),
            pl.BlockSpec((1, s, w), lambda i, j: (i, 0, 0)),
        ],
        out_specs=pl.BlockSpec((1, s, w), lambda i, j: (i, 0, 0)),
        out_shape=jax.ShapeDtypeStruct((b, s, w), BF16),
        scratch_shapes=[pltpu.VMEM((h, w), BF16), pltpu.VMEM((h, w), BF16), pltpu.VMEM((h, w), F32)],
        compiler_params=_cparams(2),
        name="seqdft",
    )(ct, st, u, v)


def _attn_kernel(qt_ref, k_ref, vt_ref, o_ref, sa_ref, sb_ref, ma_ref, mb_ref, *, tq):
    seq = k_ref.shape[2]
    per_head = seq // tq
    nblk = N_KV_HEADS * per_head
    ck = min(seq, ATTN_KV_CHUNK)
    chunks = [slice(c * ck, (c + 1) * ck) for c in range(seq // ck)]

    def locate(i):
        head = i // per_head
        return head, pl.multiple_of((i - head * per_head) * tq, tq)

    def query_block(i):
        head, off = locate(i)
        return jnp.concatenate(
            [qt_ref[0, head * GQA_GROUP + g, :, pl.ds(off, tq)] for g in range(GQA_GROUP)], axis=1)

    def score_chunk(i, qt, rows, s_ref, m):
        s = _dot(k_ref[0, locate(i)[0], rows, :], qt)
        s_ref[rows, :] = s
        mc = jnp.max(s, axis=0, keepdims=True)
        return mc if m is None else jnp.maximum(m, mc)

    def value_chunk(i, rows, s_ref, m, acc):
        p = jnp.exp2(s_ref[rows, :] - m).astype(BF16)
        part = _dot(vt_ref[0, locate(i)[0], :, rows], p)
        return part if acc is None else acc + part

    def write_out(i, acc):
        o = acc[:HEAD_DIM] * (1.0 / acc[HEAD_DIM:HEAD_DIM + 1])
        head, off = locate(i)
        for g in range(GQA_GROUP):
            r0 = pl.multiple_of((head * GQA_GROUP + g) * HEAD_DIM, HEAD_DIM)
            o_ref[0, pl.ds(r0, HEAD_DIM), pl.ds(off, tq)] = o[:, g * tq:(g + 1) * tq].astype(BF16)

    def step(i_next, s_next, m_next, i_cur, s_cur, m_cur):
        qt = None if i_next is None else query_block(i_next)
        m_old = None if i_cur is None else m_cur[...]
        m_new, acc = None, None
        for rows in chunks:
            if i_next is not None:
                m_new = score_chunk(i_next, qt, rows, s_next, m_new)
            if i_cur is not None:
                acc = value_chunk(i_cur, rows, s_cur, m_old, acc)
        if i_next is not None:
            m_next[...] = m_new
        if i_cur is not None:
            write_out(i_cur, acc)

    step(0, sa_ref, ma_ref, None, None, None)

    def pair(j, carry):
        step(2 * j + 1, sb_ref, mb_ref, 2 * j, sa_ref, ma_ref)
        step(2 * j + 2, sa_ref, ma_ref, 2 * j + 1, sb_ref, mb_ref)
        return carry

    lax.fori_loop(0, nblk // 2 - 1, pair, 0)
    step(nblk - 1, sb_ref, mb_ref, nblk - 2, sa_ref, ma_ref)
    step(None, None, None, nblk - 1, sb_ref, mb_ref)


def _attention(qt, k, vt, tq):
    b, _, _, s = qt.shape
    assert (s // tq) % 2 == 0
    width = GQA_GROUP * tq
    return pl.pallas_call(
        functools.partial(_attn_kernel, tq=tq),
        grid=(b,),
        in_specs=[
            pl.BlockSpec((1, N_Q_HEADS, HEAD_DIM, s), lambda i: (i, 0, 0, 0)),
            pl.BlockSpec((1, N_KV_HEADS, s, HEAD_DIM), lambda i: (i, 0, 0, 0)),
            pl.BlockSpec((1, N_KV_HEADS, V_ROWS, s), lambda i: (i, 0, 0, 0)),
        ],
        out_specs=pl.BlockSpec((1, Q_WIDTH, s), lambda i: (i, 0, 0)),
        out_shape=jax.ShapeDtypeStruct((b, Q_WIDTH, s), BF16),
        scratch_shapes=[pltpu.VMEM((s, width), F32), pltpu.VMEM((s, width), F32),
                        pltpu.VMEM((1, width), F32), pltpu.VMEM((1, width), F32)],
        compiler_params=_cparams(1),
        name="gqa_attention",
    )(qt, k, vt)


def _mlp_block(x, g, w1_ref, w2_ref):
    h = _rms_rows(x, g).astype(BF16)
    a = jnp.maximum(_dot(h, w1_ref[...]), 0.0)
    return x + _dot((a * a).astype(BF16), w2_ref[...])


TAIL_SPLIT = 2


def _tail0_kernel(fa_ref, att_ref, x_ref, wa_ref, wb_ref, g_ref, w1_ref, w2_ref, gn_ref,
                  o_ref, ut_ref):
    tm = x_ref.shape[1]
    sub = tm // TAIL_SPLIT
    rows = [slice(r * sub, (r + 1) * sub) for r in range(TAIL_SPLIT)]

    def project(rs):
        att = lax.dot_general(att_ref[0, :, rs], wb_ref[...], (((0,), (0,)), ((), ())),
                              preferred_element_type=F32)
        return x_ref[0, rs, :] + (_dot(fa_ref[0, rs, :], wa_ref[...]) + att)

    def finish(rs, x1):
        x2 = _mlp_block(x1, g_ref[...], w1_ref, w2_ref)
        o_ref[0, rs, :] = x2
        ut_ref[:, rs] = _rms_rows(x2, gn_ref[...]).T.astype(BF16)

    x_prev = project(rows[0])
    for r in range(1, TAIL_SPLIT):
        x_next = project(rows[r])
        finish(rows[r - 1], x_prev)
        x_prev = x_next
    finish(rows[-1], x_prev)


def _tail0(fa, att, x, w_out, g_mlp, w1, w2, g_next, tm):
    b, s, d = x.shape
    nblk = s // tm
    w = w_out.astype(BF16)
    const = lambda shape: pl.BlockSpec(shape, lambda i, j: (0,) * len(shape),
                                       pipeline_mode=pl.Buffered(1))
    return pl.pallas_call(
        _tail0_kernel,
        grid=(b, nblk),
        in_specs=[
            pl.BlockSpec((1, tm, MIX_A), lambda i, j: (i, j, 0)),
            pl.BlockSpec((1, Q_WIDTH, tm), lambda i, j: (i, 0, j)),
            pl.BlockSpec((1, tm, d), lambda i, j: (i, j, 0)),
            const((MIX_A, d)), const((Q_WIDTH, d)),
            const((1, d)), const((d, D_FF)), const((D_FF, d)), const((1, d)),
        ],
        out_specs=[pl.BlockSpec((1, tm, d), lambda i, j: (i, j, 0)),
                   pl.BlockSpec((d, tm), lambda i, j: (0, i * nblk + j))],
        out_shape=[jax.ShapeDtypeStruct((b, s, d), F32),
                   jax.ShapeDtypeStruct((d, b * s), BF16)],
        compiler_params=_cparams(2),
        name="outproj_mlp",
    )(fa, att, x, w[:MIX_A], w[MIX_A:], g_mlp[None, :], w1.astype(BF16), w2.astype(BF16),
      g_next[None, :])


MLP_SUB_ROWS = 512


def _mlp_kernel(x_ref, g_ref, w1_ref, w2_ref, gf_ref, o_ref, *, final_norm):
    tm = x_ref.shape[1]
    sub = min(tm, MLP_SUB_ROWS)
    for r in range(tm // sub):
        rs = slice(r * sub, (r + 1) * sub)
        y = _mlp_block(x_ref[0, rs, :], g_ref[...], w1_ref, w2_ref)
        if final_norm:
            y = _rms_rows(y, gf_ref[...])
        o_ref[0, rs, :] = y


def _mlp(x, g, w1, w2, g_final, tm, final_norm):
    b, s, d = x.shape
    const = lambda shape: pl.BlockSpec(shape, lambda i, j: (0,) * len(shape),
                                       pipeline_mode=pl.Buffered(1))
    return pl.pallas_call(
        functools.partial(_mlp_kernel, final_norm=final_norm),
        grid=(b, s // tm),
        in_specs=[
            pl.BlockSpec((1, tm, d), lambda i, j: (i, j, 0)),
            const((1, d)), const((d, D_FF)), const((D_FF, d)), const((1, d)),
        ],
        out_specs=pl.BlockSpec((1, tm, d), lambda i, j: (i, j, 0)),
        out_shape=jax.ShapeDtypeStruct((b, s, d), F32),
        compiler_params=_cparams(2),
        name="mlp_final" if final_norm else "mlp",
    )(x, g[None, :], w1.astype(BF16), w2.astype(BF16), g_final[None, :])


def _cmul(ar, ai, br, bi):
    return ar * br - ai * bi, ar * bi + ai * br


def _cpow(a, theta, e):
    mag = jnp.exp(e * a)
    ang = e * theta
    return mag * jnp.cos(ang), mag * jnp.sin(ang)


def _ssm_kernel(ut_ref, prow_ref, crow_ref, bcol_ref, shift_ref, y_ref,
                wall_ref, mmat_ref, cmat_ref, yacc_ref, xprev_ref, *, n_chunks):
    P = SSM_STATE
    C = SSM_GROUP_CH
    L = CHUNK
    rows = ut_ref.shape[2]

    prow = prow_ref[0]
    lre, lim = prow[0:1, :], prow[1:2, :]
    dt = jnp.exp(prow[2:3, :])
    a_row, th_row = lre * dt, lim * dt
    lbr, lbi = _cpow(a_row, th_row, 1.0)
    inv = 1.0 / (lre * lre + lim * lim)
    coef_r, coef_i = _cmul(lbr - 1.0, lbi, lre * inv, -lim * inv)
    bbr, bbi = _cmul(coef_r, coef_i, bcol_ref[0, 0].T, bcol_ref[0, 1].T)

    sq = [(lbr, lbi)]
    while len(sq) <= 7:
        sq.append(_cmul(*sq[-1], *sq[-1]))
    s8 = lax.broadcasted_iota(jnp.int32, (8, 2 * P), 0).astype(F32)
    up = _cpow(a_row, th_row, s8)
    dn = _cpow(a_row, th_row, 7.0 - s8)
    for k in range(3, 7):
        hi_up = _cmul(*up, *sq[k])
        hi_dn = _cmul(*dn, *sq[k])
        up = tuple(jnp.concatenate([lo, hi], axis=0) for lo, hi in zip(up, hi_up))
        dn = tuple(jnp.concatenate([hi, lo], axis=0) for lo, hi in zip(dn, hi_dn))
    up1 = _cmul(*up, lbr, lbi)
    dn1 = _cmul(*dn, lbr, lbi)
    fwd_lane = lax.broadcasted_iota(jnp.int32, (L, 2 * P), 1) < P
    row0 = lax.broadcasted_iota(jnp.int32, (L, 2 * P), 0) == 0

    pwr, pwi = jnp.where(fwd_lane, dn[0], up[0]), jnp.where(fwd_lane, dn[1], up[1])
    for c in range(C):
        xr, xi = _cmul(pwr, pwi, bbr[c:c + 1, :], bbi[c:c + 1, :])
        mmat_ref[c * L:(c + 1) * L, C * L:C * L + 2 * P] = xr.astype(BF16)
        mmat_ref[c * L:(c + 1) * L, C * L + 2 * P:] = xi.astype(BF16)

    qwr = jnp.where(fwd_lane, up1[0], dn1[0]).T
    qwi = jnp.where(fwd_lane, up1[1], dn1[1]).T
    ccr, cci = crow_ref[0, 0].T, crow_ref[0, 1].T
    for c in range(C):
        xr, xi = _cmul(qwr, qwi, ccr[:, c:c + 1], cci[:, c:c + 1])
        cmat_ref[:2 * P, c * L:(c + 1) * L] = xr.astype(BF16)
        cmat_ref[2 * P:, c * L:(c + 1) * L] = (-xi).astype(BF16)

    zero = jnp.zeros((L, 2 * P), F32)
    k_lo = (jnp.where(fwd_lane, up[0], jnp.where(row0, 1.0, 0.0)), jnp.where(fwd_lane, up[1], zero))
    k_hi = (jnp.where(fwd_lane, zero, dn1[0]), jnp.where(fwd_lane, zero, dn1[1]))
    kwr = jnp.concatenate([k_lo[0].T, k_hi[0].T], axis=1)
    kwi = jnp.concatenate([k_lo[1].T, k_hi[1].T], axis=1)
    crr, cri = crow_ref[0, 0], crow_ref[0, 1]
    cb_r, cb_i = [], []
    for c in range(C):
        xr, xi = _cmul(crr, cri, bbr[c:c + 1, :], bbi[c:c + 1, :])
        cb_r.append(xr)
        cb_i.append(xi)
    cb_r = jnp.concatenate(cb_r, axis=0)
    cb_i = jnp.concatenate(cb_i, axis=0)
    ktab = _dot3(cb_r, kwr) - _dot3(cb_i, kwi)

    def fine(blk):
        return pltpu.roll(jnp.broadcast_to(ktab[blk:blk + 1, :], (8, 2 * L)), 0, 1,
                          stride=1, stride_axis=0)
    for blk in range(0, C * C, 2):
        wall_ref[blk * 8:blk * 8 + 16, :] = jnp.concatenate(
            [fine(blk), fine(blk + 1)], axis=0).astype(BF16)

    own_region = pl.when(pl.program_id(0) >= 0)

    @own_region
    def _coarse():
        sh = _dot(wall_ref[...], shift_ref[...])
        for hh in range(L // 16):
            lo = 2 * hh * L
            pair = jnp.concatenate([sh[:, lo:lo + L].reshape(C * C, 8, L),
                                    sh[:, lo + L:lo + 2 * L].reshape(C * C, 8, L)],
                                   axis=1).astype(BF16)
            for cp in range(C):
                mmat_ref[cp * L + 16 * hh:cp * L + 16 * (hh + 1), :C * L] = jnp.concatenate(
                    [pair[cp * C + c] for c in range(C)], axis=1)


    @own_region
    def _intra():
        u = jnp.concatenate([ut_ref[0, c] for c in range(C)], axis=1)
        yacc_ref[...] = _dot(u, mmat_ref[...])
    xr = yacc_ref[:, C * L:C * L + 2 * P]
    xi = yacc_ref[:, C * L + 2 * P:]

    k_idx = lax.broadcasted_iota(jnp.int32, (rows, 2 * P), 0) % n_chunks
    is_f = lax.broadcasted_iota(jnp.int32, (rows, 2 * P), 1) < P
    ar, ai = sq[7]

    def shifted(t, step):
        down = pltpu.roll(t, step, 0)
        up = pltpu.roll(t, rows - step, 0)
        return jnp.where(is_f, jnp.where(k_idx >= step, down, 0.0),
                         jnp.where(k_idx < n_chunks - step, up, 0.0))

    step = 1
    while step < n_chunks:
        sr, si = shifted(xr, step), shifted(xi, step)
        pr, pi = _cmul(sr, si, ar, ai)
        xr, xi = xr + pr, xi + pi
        ar, ai = _cmul(ar, ai, ar, ai)
        step *= 2
    xprev_ref[...] = jnp.concatenate([shifted(xr, 1), shifted(xi, 1)], axis=1).astype(BF16)

    @own_region
    def _inter():
        y = yacc_ref[:, :C * L] + _dot(xprev_ref[...], cmat_ref[...])
        for c in range(C):
            y_ref[0, c] = y[:, c * L:(c + 1) * L]


def _shift_mats():
    L = CHUNK
    m = np.arange(2 * L)[:, None]
    col = np.arange((L // 8) * L)[None, :]
    h, i = col // L, col % L
    return jnp.asarray(m == (i - 8 * h) % (2 * L), BF16)


def _ssm(ut, lam_re, lam_im, log_dt, b_re, b_im, c_re, c_im, batch, seq):
    G, C, P, L = SSM_GROUPS, SSM_GROUP_CH, SSM_STATE, CHUNK
    n_chunks = seq // L
    rows = batch * n_chunks
    ut4 = ut.reshape(G, C, rows, L)
    f32 = lambda t: t.astype(F32)
    ldt = jnp.broadcast_to(f32(log_dt)[:, :, None], (2, G, P))
    prow = jnp.stack([f32(lam_re), f32(lam_im), ldt]).transpose(2, 0, 1, 3).reshape(G, 3, 2 * P)
    crow = jnp.stack([f32(c_re), f32(c_im)]).transpose(2, 0, 3, 1, 4).reshape(G, 2, C, 2 * P)
    bcol = jnp.stack([f32(b_re), f32(b_im)]).transpose(2, 0, 1, 3, 4).reshape(G, 2, 2 * P, C)
    return pl.pallas_call(
        functools.partial(_ssm_kernel, n_chunks=n_chunks),
        grid=(G,),
        in_specs=[
            pl.BlockSpec((1, C, rows, L), lambda g: (g, 0, 0, 0)),
            pl.BlockSpec((1, 3, 2 * P), lambda g: (g, 0, 0)),
            pl.BlockSpec((1, 2, C, 2 * P), lambda g: (g, 0, 0, 0)),
            pl.BlockSpec((1, 2, 2 * P, C), lambda g: (g, 0, 0, 0)),
            pl.BlockSpec((2 * L, (L // 8) * L), lambda g: (0, 0), pipeline_mode=pl.Buffered(1)),
        ],
        out_specs=pl.BlockSpec((1, C, rows, L), lambda g: (g, 0, 0, 0)),
        out_shape=jax.ShapeDtypeStruct((G, C, rows, L), F32),
        scratch_shapes=[
            pltpu.VMEM((C * C * 8, 2 * L), BF16),
            pltpu.VMEM((C * L, C * L + 4 * P), BF16),
            pltpu.VMEM((4 * P, C * L), BF16),
            pltpu.VMEM((rows, C * L + 4 * P), F32),
            pltpu.VMEM((rows, 4 * P), BF16),
        ],
        compiler_params=_cparams(1),
        name="s5_scan",
    )(ut4, prow, crow, bcol, _shift_mats())


def _gelu_tanh(y):
    return 0.5 * y * (1.0 + jnp.tanh(math.sqrt(2.0 / math.pi) * (y + 0.044715 * (y * y * y))))


def _gate_kernel(yt_ref, x_ref, g_ref, dsk_ref, wg_ref, bg_ref, o_ref):
    x = x_ref[0]
    u = _rms_rows(x, g_ref[...])
    n_rows = yt_ref.shape[2]
    y_rct = jnp.swapaxes(yt_ref[...].reshape(D_MODEL, n_rows, CHUNK), 0, 1)
    y = jnp.concatenate([y_rct[r].T for r in range(n_rows)], axis=0) + dsk_ref[...] * u
    gl = _gelu_tanh(y)
    gate = _dot(gl.astype(BF16), wg_ref[...]) + bg_ref[...]
    o_ref[0] = x + gl * (0.5 + 0.5 * jnp.tanh(0.5 * gate))


def _gate(y4, x, g, d_skip, w_gate, b_gate, ts):
    b, s, d = x.shape
    nblk = s // ts
    n_rows = ts // CHUNK
    const = lambda shape: pl.BlockSpec(shape, lambda i, j: (0,) * len(shape))
    return pl.pallas_call(
        _gate_kernel,
        grid=(b, nblk),
        in_specs=[
            pl.BlockSpec((SSM_GROUPS, SSM_GROUP_CH, n_rows, CHUNK), lambda i, j: (0, 0, i * nblk + j, 0)),
            pl.BlockSpec((1, ts, d), lambda i, j: (i, j, 0)),
            const((1, d)), const((1, d)), const((d, d)), const((1, d)),
        ],
        out_specs=pl.BlockSpec((1, ts, d), lambda i, j: (i, j, 0)),
        out_shape=jax.ShapeDtypeStruct((b, s, d), F32),
        compiler_params=_cparams(2),
        name="s5_gate",
    )(y4, x, g[None, :], d_skip.astype(F32)[None, :], w_gate.astype(BF16), b_gate.astype(F32)[None, :])


def _pick(n, pref):
    t = min(n, pref)
    while n % t:
        t //= 2
    return t


def kernel(x, norm_mix, norm_mlp, mlp_w1, mlp_w2, w_in, w_fnet, q_norm, k_norm, w_out, lam_re, lam_im, log_dt, b_re, b_im, c_re, c_im, d_skip, w_gate, b_gate, final_norm):
    b, s, d = x.shape
    assert d == D_MODEL and s % CHUNK == 0 and s % GRID_W == 0
    tm = _pick(s, 512)
    gmat = _fourier_fold(w_fnet[0])
    u, v, q, k, vv = _inproj(x, norm_mix[0], w_in[0], gmat, q_norm[0], k_norm[0], _pick(s, 1024))
    fa = _seqdft(u, v)
    att = _attention(q, k, vv, _pick(s, 128))
    x, ut = _tail0(fa, att, x, w_out[0], norm_mlp[0], mlp_w1[0], mlp_w2[0], norm_mix[1], tm)
    yt = _ssm(ut, lam_re[0], lam_im[0], log_dt[0], b_re[0], b_im[0], c_re[0], c_im[0], b, s)
    x = _gate(yt, x, norm_mix[1], d_skip[0], w_gate[0], b_gate[0], _pick(s, 1024))
    return _mlp(x, norm_mlp[1], mlp_w1[1], mlp_w2[1], final_norm, _pick(s, 1024), True)
```
